```python
import jax, jax.numpy as jnp
from jax import lax
import numpy as np

D_MODEL = 1024
BATCH = 2
SEQ = 8192
DEPTH = 2

N_META = 16
BLOCK = 128
N_PAD = BLOCK - N_META
FOX_HEADS = 8
FOX_HEAD_DIM = D_MODEL // 16
FOX_WIDTH = FOX_HEADS * FOX_HEAD_DIM
GLA_HEADS = 4
GLA_WIDTH = D_MODEL // 2
GLA_DV = GLA_WIDTH // GLA_HEADS
GLA_DK = GLA_DV // 2
GLA_KWIDTH = GLA_HEADS * GLA_DK
GLA_GATE_RANK = 16
GLA_GATE_TEMP = 16.0
D_FF = ((8 * D_MODEL // 3 + 127) // 128) * 128
RMS_EPS = 1e-6
SPLIT_SIZES = (FOX_WIDTH, FOX_WIDTH, FOX_WIDTH, FOX_HEADS, GLA_KWIDTH, GLA_KWIDTH, GLA_WIDTH, GLA_GATE_RANK, GLA_WIDTH, D_MODEL, D_MODEL)
N_IN = 3 * FOX_WIDTH + FOX_HEADS + 2 * GLA_KWIDTH + 2 * GLA_WIDTH + GLA_GATE_RANK + 2 * D_MODEL

kernel_name = "hybrid_fox_gla_macaron_meta"


def rmsnorm(x, gain):
    xf = x.astype(jnp.float32)
    y = xf * lax.rsqrt(jnp.mean(xf * xf, axis=-1, keepdims=True) + RMS_EPS)
    return (y * gain.astype(jnp.float32)).astype(x.dtype)


def swiglu(h, w_gu, w_down):
    gate, up = jnp.split(h @ w_gu, 2, axis=-1)
    return (jax.nn.silu(gate) * up) @ w_down


def to_heads(t, n_heads):
    b, l, _ = t.shape
    return t.reshape(b, l, n_heads, -1).transpose(0, 2, 1, 3)


def from_heads(t):
    b, h, l, d = t.shape
    return t.transpose(0, 2, 1, 3).reshape(b, l, h * d)


def fox_attention(q, k, v, log_f, key_valid):
    lp = q.shape[2]
    cum = jnp.cumsum(log_f, axis=-1)
    scale = FOX_HEAD_DIM ** -0.5
    outs = []
    for start in range(0, lp, BLOCK):
        end = start + BLOCK
        s = jnp.einsum('bhqd,bhkd->bhqk', q[:, :, start:end], k[:, :, :end]).astype(jnp.float32) * scale
        s = s + cum[:, :, start:end, None] - cum[:, :, None, :end]
        qpos = jnp.arange(start, end)[:, None]
        kpos = jnp.arange(end)[None, :]
        allowed = (kpos <= qpos) & (key_valid[None, :end] | (kpos == qpos))
        p = jax.nn.softmax(jnp.where(allowed, s, -jnp.inf), axis=-1)
        outs.append(jnp.einsum('bhqk,bhkd->bhqd', p.astype(v.dtype), v[:, :, :end]))
    return jnp.concatenate(outs, axis=2)


def gla_chunked(q, k, v, log_a):
    b, h, lp, dk = q.shape
    dv = v.shape[-1]
    n = lp // BLOCK

    def to_chunks(t):
        return jnp.moveaxis(t.reshape(b, h, n, BLOCK, t.shape[-1]), 2, 0)

    causal = jnp.tril(jnp.ones((BLOCK, BLOCK), dtype=bool))[:, :, None]

    def step(state, inp):
        qi, ki, vi, ai = inp
        cb = jnp.cumsum(ai, axis=2)
        qf, kf, vf = qi.astype(jnp.float32), ki.astype(jnp.float32), vi.astype(jnp.float32)
        inter = jnp.einsum('bhtk,bhkv->bhtv', qf * jnp.exp(cb), state)
        diff = cb[:, :, :, None, :] - cb[:, :, None, :, :]
        decay = jnp.exp(jnp.where(causal, diff, -jnp.inf))
        att = jnp.einsum('bhtk,bhsk,bhtsk->bhts', qf, kf, decay)
        intra = jnp.einsum('bhts,bhsv->bhtv', att, vf)
        last = cb[:, :, -1:, :]
        new_state = jnp.exp(last[:, :, 0, :, None]) * state + jnp.einsum('bhsk,bhsv->bhkv', kf * jnp.exp(last - cb), vf)
        return new_state, (inter + intra).astype(vi.dtype)

    s0 = jnp.zeros((b, h, dk, dv), jnp.float32)
    _, o = lax.scan(step, s0, (to_chunks(q), to_chunks(k), to_chunks(v), to_chunks(log_a)))
    return jnp.moveaxis(o, 0, 2).reshape(b, h, lp, dv)


def token_mixer(h, key_valid, w_in, w_alpha_up, b_alpha, b_f, g_gla_out, w_fox_o, w_gla_o, w_out):
    valid = key_valid.astype(jnp.float32)[None, :, None]
    points = np.cumsum(SPLIT_SIZES)[:-1].tolist()
    fq, fk, fv, ff, gq, gk, gv, ga, gr, g_merge_a, g_merge_b = jnp.split(h @ w_in, points, axis=-1)
    log_f = jax.nn.log_sigmoid(ff.astype(jnp.float32) + b_f.astype(jnp.float32)) * valid
    o_fox = fox_attention(to_heads(fq, FOX_HEADS), to_heads(fk, FOX_HEADS), to_heads(fv, FOX_HEADS),
                          log_f.transpose(0, 2, 1), key_valid)
    log_a = jax.nn.log_sigmoid((ga @ w_alpha_up).astype(jnp.float32) + b_alpha.astype(jnp.float32)) * (valid / GLA_GATE_TEMP)
    gk = gk * valid.astype(gk.dtype)
    o_gla = gla_chunked(to_heads(gq * (GLA_DK ** -0.5), GLA_HEADS), to_heads(gk, GLA_HEADS),
                        to_heads(gv, GLA_HEADS), to_heads(log_a, GLA_HEADS))
    o_gla = rmsnorm(o_gla, g_gla_out.reshape(GLA_HEADS, 1, GLA_DV))
    o_gla = from_heads(o_gla) * jax.nn.silu(gr)
    y = jax.nn.sigmoid(g_merge_a) * (from_heads(o_fox) @ w_fox_o) + jax.nn.sigmoid(g_merge_b) * (o_gla @ w_gla_o)
    return y @ w_out


def setup_inputs(seed: int = 0) -> dict:
    key = jax.random.key(seed)
    ks = jax.random.split(key, 21)

    def nrm(k, shape, scale):
        return jax.random.normal(k, shape, jnp.float32) * scale

    def gain(k):
        return 1.0 + nrm(k, (DEPTH, D_MODEL), 0.02)

    return {
        "x": nrm(ks[0], (BATCH, SEQ, D_MODEL), 1.0),
        "meta_tokens": nrm(ks[1], (N_META, D_MODEL), 1.0),
        "w_in": nrm(ks[2], (DEPTH, D_MODEL, N_IN), D_MODEL ** -0.5),
        "w_alpha_up": nrm(ks[3], (DEPTH, GLA_GATE_RANK, GLA_KWIDTH), GLA_GATE_RANK ** -0.5),
        "b_alpha": nrm(ks[4], (DEPTH, GLA_KWIDTH), 0.1),
        "b_f": 2.0 + nrm(ks[5], (DEPTH, FOX_HEADS), 0.1),
        "g_gla_out": 1.0 + nrm(ks[6], (DEPTH, GLA_WIDTH), 0.02),
        "w_fox_o": nrm(ks[7], (DEPTH, FOX_WIDTH, D_MODEL), FOX_WIDTH ** -0.5),
        "w_gla_o": nrm(ks[8], (DEPTH, GLA_WIDTH, D_MODEL), GLA_WIDTH ** -0.5),
        "w_out": nrm(ks[9], (DEPTH, D_MODEL, D_MODEL), D_MODEL ** -0.5),
        "w_ffn1_gu": nrm(ks[10], (DEPTH, D_MODEL, 2 * D_FF), D_MODEL ** -0.5),
        "w_ffn1_down": nrm(ks[11], (DEPTH, D_FF, D_MODEL), D_FF ** -0.5),
        "w_ffn2_gu": nrm(ks[12], (DEPTH, D_MODEL, 2 * D_FF), D_MODEL ** -0.5),
        "w_ffn2_down": nrm(ks[13], (DEPTH, D_FF, D_MODEL), D_FF ** -0.5),
        "g_pre_ffn1": gain(ks[14]),
        "g_post_ffn1": gain(ks[15]),
        "g_pre_mix": gain(ks[16]),
        "g_post_mix": gain(ks[17]),
        "g_pre_ffn2": gain(ks[18]),
        "g_post_ffn2": gain(ks[19]),
    }


def reference(x, meta_tokens, w_in, w_alpha_up, b_alpha, b_f, g_gla_out, w_fox_o, w_gla_o, w_out,
              w_ffn1_gu, w_ffn1_down, w_ffn2_gu, w_ffn2_down,
              g_pre_ffn1, g_post_ffn1, g_pre_mix, g_post_mix, g_pre_ffn2, g_post_ffn2):
    b = x.shape[0]
    pad = jnp.zeros((b, N_PAD, D_MODEL), x.dtype)
    meta = jnp.broadcast_to(meta_tokens.astype(x.dtype)[None], (b, N_META, D_MODEL))
    h = jnp.concatenate([pad, meta, x], axis=1)
    key_valid = jnp.arange(h.shape[1]) >= N_PAD
    for l in range(DEPTH):
        h = h + 0.5 * rmsnorm(swiglu(rmsnorm(h, g_pre_ffn1[l]), w_ffn1_gu[l], w_ffn1_down[l]), g_post_ffn1[l])
        mix = token_mixer(rmsnorm(h, g_pre_mix[l]), key_valid, w_in[l], w_alpha_up[l], b_alpha[l], b_f[l],
                          g_gla_out[l], w_fox_o[l], w_gla_o[l], w_out[l])
        h = h + rmsnorm(mix, g_post_mix[l])
        h = h + 0.5 * rmsnorm(swiglu(rmsnorm(h, g_pre_ffn2[l]), w_ffn2_gu[l], w_ffn2_down[l]), g_post_ffn2[l])
    return h[:, BLOCK:]
```

```python
import functools

import jax
import jax.numpy as jnp
import numpy as np
from jax import lax
from jax.experimental import pallas as pl
from jax.experimental.pallas import tpu as pltpu

D_MODEL = 1024
N_META = 16
BLOCK = 128
N_PAD = BLOCK - N_META
FOX_HEADS = 8
FOX_HEAD_DIM = 64
FOX_WIDTH = FOX_HEADS * FOX_HEAD_DIM
GLA_HEADS = 4
GLA_WIDTH = 512
GLA_DV = 128
GLA_DK = 64
GLA_KWIDTH = GLA_HEADS * GLA_DK
GLA_GATE_RANK = 16
GLA_GATE_TEMP = 16.0
D_FF = 2816
RMS_EPS = 1e-6

LANES = 128
SUB = 16
N_SUB = BLOCK // SUB
EXP_CAP = 80.0
NEG = -1e30

FFN_TM = 1280
FFN_TF = 256
ROW_TM = 640
FOX_T = 640
VMEM_LIMIT = 56 * 1024 * 1024

BF16 = jnp.bfloat16
F32 = jnp.float32

C_FQ, C_FK, C_FV = 0, 512, 1024
C_GQ, C_GK, C_GV = 1536, 1792, 2048
C_GR, C_GMA, C_GMB = 2560, 3072, 4096
C_SMALL = 5120
N_IN_PERM = 5248


def _dot(a, b):
    return jnp.dot(a, b, preferred_element_type=F32)


def _dot_nt(a, b):
    return lax.dot_general(a, b, (((1,), (1,)), ((), ())), preferred_element_type=F32)


def _dot_tn(a, b):
    return lax.dot_general(a, b, (((0,), (0,)), ((), ())), preferred_element_type=F32)


def _rms(x, g):
    return x * lax.rsqrt(jnp.mean(x * x, axis=-1, keepdims=True) + RMS_EPS) * g


def _log_sigmoid(x):
    return jnp.minimum(x, 0.0) - jnp.log1p(jnp.exp(-jnp.abs(x)))


def _split3(x):
    hi = x.astype(BF16)
    r1 = x - hi.astype(F32)
    mid = r1.astype(BF16)
    lo = (r1 - mid.astype(F32)).astype(BF16)
    return hi, mid, lo


def _dot_exact_lhs(m, parts):
    return _dot(m, parts[0]) + _dot(m, parts[1]) + _dot(m, parts[2])


def _ffn_kernel(h_ref, gpre_ref, wg_ref, wu_ref, wd_ref, gpost_ref, o_ref, xn_ref, acc_ref):
    j = pl.program_id(1)

    @pl.when(j == 0)
    def _():
        xn_ref[...] = _rms(h_ref[...], gpre_ref[...]).astype(BF16)
        acc_ref[...] = jnp.zeros_like(acc_ref)

    xn = xn_ref[...]
    g = _dot(xn, wg_ref[...])
    u = _dot(xn, wu_ref[...])
    a = (g * jax.nn.sigmoid(g) * u).astype(BF16)
    acc_ref[...] += _dot(a, wd_ref[...])

    @pl.when(j == pl.num_programs(1) - 1)
    def _():
        o_ref[...] = h_ref[...] + 0.5 * _rms(acc_ref[...], gpost_ref[...])


def _ffn(h, gpre, w_gu, w_down, gpost):
    n = h.shape[0]
    tm, tf = FFN_TM, FFN_TF
    nf = D_FF // tf
    return pl.pallas_call(
        _ffn_kernel,
        grid=(n // tm, nf),
        in_specs=[
            pl.BlockSpec((tm, D_MODEL), lambda i, j: (i, 0)),
            pl.BlockSpec((1, D_MODEL), lambda i, j: (0, 0)),
            pl.BlockSpec((D_MODEL, tf), lambda i, j: (0, j)),
            pl.BlockSpec((D_MODEL, tf), lambda i, j: (0, j + nf)),
            pl.BlockSpec((tf, D_MODEL), lambda i, j: (j, 0)),
            pl.BlockSpec((1, D_MODEL), lambda i, j: (0, 0)),
        ],
        out_specs=pl.BlockSpec((tm, D_MODEL), lambda i, j: (i, 0)),
        out_shape=jax.ShapeDtypeStruct((n, D_MODEL), F32),
        scratch_shapes=[pltpu.VMEM((tm, D_MODEL), BF16), pltpu.VMEM((tm, D_MODEL), F32)],
        compiler_params=pltpu.CompilerParams(
            dimension_semantics=("parallel", "arbitrary"), vmem_limit_bytes=VMEM_LIMIT),
    )(h, gpre, w_gu, w_gu, w_down, gpost)


def _proj_kernel(h_ref, g_ref, w_ref, wa_ref, ba_ref, bf_ref,
                 fq_ref, fk_ref, fv_ref, gq_ref, gk_ref, gv_ref, gr_ref, gma_ref, gmb_ref,
                 la_ref, lf_ref, *, tiles_per_seq):
    tm = h_ref.shape[0]
    xn = _rms(h_ref[...], g_ref[...]).astype(BF16)
    pos = (pl.program_id(0) % tiles_per_seq) * tm + lax.broadcasted_iota(jnp.int32, (tm, 1), 0)
    valid = (pos >= N_PAD).astype(F32)

    def proj(c0, width):
        return _dot(xn, w_ref[:, c0:c0 + width])

    fq_ref[...] = (proj(C_FQ, FOX_WIDTH) * (FOX_HEAD_DIM ** -0.5)).astype(BF16)
    fk_ref[...] = proj(C_FK, FOX_WIDTH).astype(BF16)
    fv_ref[...] = proj(C_FV, FOX_WIDTH).astype(BF16)
    gq_ref[...] = proj(C_GQ, GLA_KWIDTH) * (GLA_DK ** -0.5)
    gk_ref[...] = proj(C_GK, GLA_KWIDTH) * valid
    gv_ref[...] = proj(C_GV, GLA_WIDTH).astype(BF16)
    gr_ref[...] = proj(C_GR, GLA_WIDTH).astype(BF16)
    gma_ref[...] = proj(C_GMA, D_MODEL).astype(BF16)
    gmb_ref[...] = proj(C_GMB, D_MODEL).astype(BF16)
    small = proj(C_SMALL, LANES)
    lane = lax.broadcasted_iota(jnp.int32, (tm, LANES), 1)
    lf = _log_sigmoid(small + bf_ref[...]) * valid
    lf_ref[...] = jnp.where(lane < FOX_HEADS, lf, 0.0)
    xa = _dot(small.astype(BF16), wa_ref[...]) + ba_ref[...]
    la_ref[...] = _log_sigmoid(xa) * (valid * (1.0 / GLA_GATE_TEMP))


def _proj(h, g, w_perm, wa_pad, ba, bf_pad, seq_len):
    n = h.shape[0]
    tm = ROW_TM
    row = lambda width: pl.BlockSpec((tm, width), lambda i: (i, 0))
    full = lambda a: pl.BlockSpec(a.shape, lambda i: (0, 0))
    outs = [
        (FOX_WIDTH, BF16), (FOX_WIDTH, BF16), (FOX_WIDTH, BF16),
        (GLA_KWIDTH, F32), (GLA_KWIDTH, F32), (GLA_WIDTH, BF16),
        (GLA_WIDTH, BF16), (D_MODEL, BF16), (D_MODEL, BF16),
        (GLA_KWIDTH, F32), (LANES, F32),
    ]
    return pl.pallas_call(
        functools.partial(_proj_kernel, tiles_per_seq=seq_len // tm),
        grid=(n // tm,),
        in_specs=[row(D_MODEL), full(g), full(w_perm), full(wa_pad), full(ba), full(bf_pad)],
        out_specs=[row(w) for w, _ in outs],
        out_shape=[jax.ShapeDtypeStruct((n, w), dt) for w, dt in outs],
        compiler_params=pltpu.CompilerParams(
            dimension_semantics=("parallel",), vmem_limit_bytes=VMEM_LIMIT),
    )(h, g, w_perm, wa_pad, ba, bf_pad)


def _gla_kernel(q_ref, k_ref, v_ref, la_ref, lf_ref, o_ref, cc_ref, cr_ref, st_ref, carry_ref):
    c = pl.program_id(1)

    @pl.when(c == 0)
    def _():
        st_ref[...] = jnp.zeros_like(st_ref)
        carry_ref[...] = jnp.zeros_like(carry_ref)

    row = lax.broadcasted_iota(jnp.int32, (BLOCK, BLOCK), 0)
    col = lax.broadcasted_iota(jnp.int32, (BLOCK, BLOCK), 1)
    causal = row >= col
    tril = causal.astype(BF16)
    tril_sub = (causal & ((row // SUB) == (col // SUB))).astype(BF16)

    cl = _dot_exact_lhs(tril, _split3(lf_ref[0])) + carry_ref[...]
    carry_ref[...] = cl[BLOCK - 1:BLOCK, :]
    cc_ref[0] = cl
    cr_ref[0] = cl.T[0:FOX_HEADS, :]

    la_parts = _split3(la_ref[0])
    cb = _dot_exact_lhs(tril, la_parts)
    cw = _dot_exact_lhs(tril_sub, la_parts)
    last = cb[BLOCK - 1:BLOCK, :]
    q = q_ref[0]
    k = k_ref[0]
    q_glob = (q * jnp.exp(cb)).astype(BF16)
    q_loc = (q * jnp.exp(cw)).astype(BF16)
    k_end = (k * jnp.exp(last - cb)).astype(BF16)
    decay = jnp.exp(last)

    row_w = lax.broadcasted_iota(jnp.int32, (BLOCK, GLA_KWIDTH), 0)
    k_sub = []
    for i in range(N_SUB):
        ref = cb[i * SUB - 1:i * SUB, :] if i > 0 else jnp.zeros_like(last)
        e = jnp.exp(jnp.minimum(ref - cb, EXP_CAP))
        k_sub.append(jnp.where(row_w < (i + 1) * SUB, k * e, 0.0).astype(BF16))

    row_blk = row // SUB
    v = v_ref[0]
    for pair in range(GLA_HEADS // 2):
        sl = slice(pair * LANES, (pair + 1) * LANES)
        k_cat = jnp.concatenate([ks[:, sl] for ks in k_sub], axis=1)
        for half in range(2):
            h = 2 * pair + half
            head_lanes = (col < GLA_DK) if half == 0 else (col >= GLA_DK)
            q_h = jnp.where(head_lanes, q_loc[:, sl], 0.0)
            q_cat = jnp.concatenate(
                [jnp.where(row_blk == i, q_h, 0.0) for i in range(N_SUB)], axis=1)
            att = jnp.where(causal, _dot_nt(q_cat, k_cat), 0.0).astype(BF16)
            v_h = v[:, h * GLA_DV:(h + 1) * GLA_DV]
            st = st_ref[h]
            inter = _dot_nt(jnp.where(head_lanes, q_glob[:, sl], 0.0), st.astype(BF16))
            o_ref[0, :, h * GLA_DV:(h + 1) * GLA_DV] = inter + _dot(att, v_h)
            st_ref[h] = st * decay[:, sl] + _dot_tn(v_h, k_end[:, sl])


def _gla(gq, gk, gv, la, lf, batch, seq_len):
    nc = seq_len // BLOCK
    blk = lambda width: pl.BlockSpec((1, BLOCK, width), lambda b, c: (b, c, 0))
    r3 = lambda a: a.reshape(batch, seq_len, a.shape[-1])
    return pl.pallas_call(
        _gla_kernel,
        grid=(batch, nc),
        in_specs=[blk(GLA_KWIDTH), blk(GLA_KWIDTH), blk(GLA_WIDTH), blk(GLA_KWIDTH), blk(LANES)],
        out_specs=[blk(GLA_WIDTH), blk(LANES),
                   pl.BlockSpec((1, FOX_HEADS, BLOCK), lambda b, c: (b, 0, c))],
        out_shape=[jax.ShapeDtypeStruct((batch, seq_len, GLA_WIDTH), F32),
                   jax.ShapeDtypeStruct((batch, seq_len, LANES), F32),
                   jax.ShapeDtypeStruct((batch, FOX_HEADS, seq_len), F32)],
        scratch_shapes=[pltpu.VMEM((GLA_HEADS, GLA_DV, LANES), F32), pltpu.VMEM((1, LANES), F32)],
        compiler_params=pltpu.CompilerParams(
            dimension_semantics=("parallel", "arbitrary"), vmem_limit_bytes=VMEM_LIMIT),
    )(r3(gq), r3(gk), r3(gv), r3(la), r3(lf))


def _fox_kernel(q_ref, k_ref, v_ref, cc_ref, cr_ref, o_ref, m_ref, l_ref, acc_ref):
    t = q_ref.shape[1]
    pair = pl.program_id(1)
    i = pl.program_id(2)

    lane = lax.broadcasted_iota(jnp.int32, (t, LANES), 1)
    first = lane < FOX_HEAD_DIM
    q = q_ref[0]
    q2 = jnp.concatenate([jnp.where(first, q, 0.0), jnp.where(first, 0.0, q)], axis=0)
    cc = cc_ref[0]
    cq2 = jnp.concatenate(
        [jnp.sum(jnp.where(lane == 2 * pair + hh, cc, 0.0), axis=1, keepdims=True)
         for hh in range(2)], axis=0)

    m_ref[...] = jnp.full_like(m_ref, NEG)
    l_ref[...] = jnp.zeros_like(l_ref)
    acc_ref[...] = jnp.zeros_like(acc_ref)

    def step(j, masked):
        off = pl.multiple_of(j * t, t)
        kt = k_ref[0, pl.ds(off, t), :]
        vt = v_ref[0, pl.ds(off, t), :]
        ck = cr_ref[0, 0, j]
        s = _dot_nt(q2, kt)
        z = jnp.concatenate([s[:t] - ck[0:1], s[t:] - ck[1:2]], axis=0)
        if masked:
            qpos = i * t + lax.broadcasted_iota(jnp.int32, (t, t), 0)
            kpos = j * t + lax.broadcasted_iota(jnp.int32, (t, t), 1)
            ok = (kpos <= qpos) & ((kpos >= N_PAD) | (kpos == qpos))
            z = jnp.where(jnp.concatenate([ok, ok], axis=0), z, NEG)
        m_old = m_ref[...]
        m_new = jnp.maximum(m_old, jnp.max(z, axis=1, keepdims=True) + cq2)
        p = jnp.exp(z - (m_new - cq2))
        alpha = jnp.exp(m_old - m_new)
        l_ref[...] = alpha * l_ref[...] + jnp.sum(p, axis=1, keepdims=True)
        acc_ref[...] = alpha * acc_ref[...] + _dot(p.astype(BF16), vt)
        m_ref[...] = m_new

    step(0, True)

    def body(j, carry):
        step(j, False)
        return carry

    lax.fori_loop(1, i, body, 0)

    @pl.when(i > 0)
    def _():
        step(i, True)

    out = acc_ref[...] / l_ref[...]
    o_ref[0] = jnp.where(first, out[:t], out[t:]).astype(o_ref.dtype)


def _fox(fq, fk, fv, cum_col, cum_row, batch, seq_len):
    t = FOX_T
    nt = seq_len // t
    npairs = FOX_HEADS // 2
    r3 = lambda a: a.reshape(batch, seq_len, a.shape[-1])
    cr = cum_row.reshape(batch, npairs, 2, nt, t).transpose(0, 1, 3, 2, 4)
    tile = pl.BlockSpec((1, t, LANES), lambda b, p, i: (b, i, p))
    whole = pl.BlockSpec((1, seq_len, LANES), lambda b, p, i: (b, 0, p))
    out = pl.pallas_call(
        _fox_kernel,
        grid=(batch, npairs, nt),
        in_specs=[tile, whole, whole,
                  pl.BlockSpec((1, t, LANES), lambda b, p, i: (b, i, 0)),
                  pl.BlockSpec((1, 1, nt, 2, t), lambda b, p, i: (b, p, 0, 0, 0))],
        out_specs=tile,
        out_shape=jax.ShapeDtypeStruct((batch, seq_len, FOX_WIDTH), BF16),
        scratch_shapes=[pltpu.VMEM((2 * t, 1), F32), pltpu.VMEM((2 * t, 1), F32),
                        pltpu.VMEM((2 * t, LANES), F32)],
        compiler_params=pltpu.CompilerParams(
            dimension_semantics=("parallel", "parallel", "arbitrary"), vmem_limit_bytes=VMEM_LIMIT),
    )(r3(fq), r3(fk), r3(fv), cum_col, cr)
    return out.reshape(batch * seq_len, FOX_WIDTH)


def _post_kernel(of_ref, og_ref, gr_ref, gma_ref, gmb_ref, h_ref, ggla_ref, wfo_ref, wgo_ref,
                 wout_ref, gpost_ref, o_ref):
    og = og_ref[...]
    ggla = ggla_ref[...]
    heads = []
    for hh in range(GLA_HEADS):
        sl = slice(hh * GLA_DV, (hh + 1) * GLA_DV)
        heads.append(_rms(og[:, sl], ggla[:, sl]))
    gr = gr_ref[...].astype(F32)
    o_gla = (jnp.concatenate(heads, axis=1) * (gr * jax.nn.sigmoid(gr))).astype(BF16)
    a = _dot(of_ref[...], wfo_ref[...])
    b = _dot(o_gla, wgo_ref[...])
    y = jax.nn.sigmoid(gma_ref[...].astype(F32)) * a + jax.nn.sigmoid(gmb_ref[...].astype(F32)) * b
    mix = _dot(y.astype(BF16), wout_ref[...])
    o_ref[...] = h_ref[...] + _rms(mix, gpost_ref[...])


def _post(o_fox, o_gla, gr, gma, gmb, h, ggla, wfo, wgo, wout, gpost):
    n = h.shape[0]
    tm = ROW_TM
    row = lambda a: pl.BlockSpec((tm, a.shape[1]), lambda i: (i, 0))
    full = lambda a: pl.BlockSpec(a.shape, lambda i: (0, 0))
    args = (o_fox, o_gla, gr, gma, gmb, h, ggla, wfo, wgo, wout, gpost)
    return pl.pallas_call(
        _post_kernel,
        grid=(n // tm,),
        in_specs=[row(a) for a in args[:6]] + [full(a) for a in args[6:]],
        out_specs=pl.BlockSpec((tm, D_MODEL), lambda i: (i, 0)),
        out_shape=jax.ShapeDtypeStruct((n, D_MODEL), F32),
        compiler_params=pltpu.CompilerParams(
            dimension_semantics=("parallel",), vmem_limit_bytes=VMEM_LIMIT),
    )(*args)


def _permute_w_in(w):
    d = w.shape[0]
    pad = jnp.zeros((d, LANES - FOX_HEADS - GLA_GATE_RANK), w.dtype)
    return jnp.concatenate(
        [w[:, 0:1536], w[:, 1544:2568], w[:, 2584:5144], w[:, 1536:1544], w[:, 2568:2584], pad],
        axis=1)


def kernel(x, meta_tokens, w_in, w_alpha_up, b_alpha, b_f, g_gla_out, w_fox_o, w_gla_o, w_out,
           w_ffn1_gu, w_ffn1_down, w_ffn2_gu, w_ffn2_down,
           g_pre_ffn1, g_post_ffn1, g_pre_mix, g_post_mix, g_pre_ffn2, g_post_ffn2):
    batch, seq, d = x.shape
    seq_len = seq + BLOCK
    depth = w_in.shape[0]
    assert d == D_MODEL and seq_len % FOX_T == 0 and seq_len % ROW_TM == 0
    assert (batch * seq_len) % FFN_TM == 0

    pad = jnp.zeros((batch, N_PAD, d), x.dtype)
    meta = jnp.broadcast_to(meta_tokens.astype(x.dtype)[None], (batch, N_META, d))
    h = jnp.concatenate([pad, meta, x], axis=1).reshape(batch * seq_len, d)

    row2 = lambda a: a.reshape(1, -1).astype(F32)
    for l in range(depth):
        w_perm = _permute_w_in(w_in[l]).astype(BF16)
        wa_pad = jnp.zeros((LANES, GLA_KWIDTH), F32).at[
            FOX_HEADS:FOX_HEADS + GLA_GATE_RANK].set(w_alpha_up[l]).astype(BF16)
        bf_pad = jnp.zeros((1, LANES), F32).at[0, :FOX_HEADS].set(b_f[l])

        h = _ffn(h, row2(g_pre_ffn1[l]), w_ffn1_gu[l].astype(BF16), w_ffn1_down[l].astype(BF16),
                 row2(g_post_ffn1[l]))
        fq, fk, fv, gq, gk, gv, gr, gma, gmb, la, lf = _proj(
            h, row2(g_pre_mix[l]), w_perm, wa_pad, row2(b_alpha[l]), bf_pad, seq_len)
        o_gla, cum_col, cum_row = _gla(gq, gk, gv, la, lf, batch, seq_len)
        o_fox = _fox(fq, fk, fv, cum_col, cum_row, batch, seq_len)
        h = _post(o_fox, o_gla.reshape(batch * seq_len, GLA_WIDTH), gr, gma, gmb, h,
                  row2(g_gla_out[l]), w_fox_o[l].astype(BF16), w_gla_o[l].astype(BF16),
                  w_out[l].astype(BF16), row2(g_post_mix[l]))
        h = _ffn(h, row2(g_pre_ffn2[l]), w_ffn2_gu[l].astype(BF16), w_ffn2_down[l].astype(BF16),
                 row2(g_post_ffn2[l]))
    return h.reshape(batch, seq_len, d)[:, BLOCK:]
```

```python
import functools

import jax
import jax.numpy as jnp
import numpy as np
from jax import lax
from jax.experimental import pallas as pl
from jax.experimental.pallas import tpu as pltpu

D_MODEL = 1024
N_META = 16
BLOCK = 128
N_PAD = BLOCK - N_META
FOX_HEADS = 8
FOX_HEAD_DIM = 64
FOX_WIDTH = FOX_HEADS * FOX_HEAD_DIM
GLA_HEADS = 4
GLA_WIDTH = 512
GLA_DV = 128
GLA_DK = 64
GLA_KWIDTH = GLA_HEADS * GLA_DK
GLA_GATE_RANK = 16
GLA_GATE_TEMP = 16.0
D_FF = 2816
RMS_EPS = 1e-6

LANES = 128
SUB = 16
N_SUB = BLOCK // SUB
EXP_CAP = 80.0
NEG = -1e30

FFN_TM = 1280
FFN_TF = 256
ROW_TM = 640
FOX_T = 512
LOG2E = 1.4426950408889634
PRUNE_BITS = 152.0
NORM_UP = 1.01
FAST_MAX_BITS = 50.0
VMEM_LIMIT = 56 * 1024 * 1024

BF16 = jnp.bfloat16
F32 = jnp.float32

C_FQ, C_FK, C_FV = 0, 512, 1024
C_GQ, C_GK, C_GV = 1536, 1792, 2048
C_GR, C_GMA, C_GMB = 2560, 3072, 4096
C_SMALL = 5120
N_IN_PERM = 5248


def _dot(a, b):
    return jnp.dot(a, b, preferred_element_type=F32)


def _dot_nt(a, b):
    return lax.dot_general(a, b, (((1,), (1,)), ((), ())), preferred_element_type=F32)


def _dot_tn(a, b):
    return lax.dot_general(a, b, (((0,), (0,)), ((), ())), preferred_element_type=F32)


def _rms(x, g):
    return x * lax.rsqrt(jnp.mean(x * x, axis=-1, keepdims=True) + RMS_EPS) * g


def _log_sigmoid(x):
    return jnp.minimum(x, 0.0) - jnp.log1p(jnp.exp(-jnp.abs(x)))


def _split3(x):
    hi = x.astype(BF16)
    r1 = x - hi.astype(F32)
    mid = r1.astype(BF16)
    lo = (r1 - mid.astype(F32)).astype(BF16)
    return hi, mid, lo


def _dot_exact_lhs(m, parts):
    return _dot(m, parts[0]) + _dot(m, parts[1]) + _dot(m, parts[2])


def _ffn_kernel(h_ref, gpre_ref, wg_ref, wu_ref, wd_ref, gpost_ref, o_ref, xn_ref, acc_ref):
    j = pl.program_id(1)

    @pl.when(j == 0)
    def _():
        xn_ref[...] = _rms(h_ref[...], gpre_ref[...]).astype(BF16)
        acc_ref[...] = jnp.zeros_like(acc_ref)

    xn = xn_ref[...]
    g = _dot(xn, wg_ref[...])
    u = _dot(xn, wu_ref[...])
    a = (g * jax.nn.sigmoid(g) * u).astype(BF16)
    acc_ref[...] += _dot(a, wd_ref[...])

    @pl.when(j == pl.num_programs(1) - 1)
    def _():
        o_ref[...] = h_ref[...] + 0.5 * _rms(acc_ref[...], gpost_ref[...])


def _ffn(h, gpre, w_gu, w_down, gpost):
    n = h.shape[0]
    tm, tf = FFN_TM, FFN_TF
    nf = D_FF // tf
    return pl.pallas_call(
        _ffn_kernel,
        grid=(n // tm, nf),
        in_specs=[
            pl.BlockSpec((tm, D_MODEL), lambda i, j: (i, 0)),
            pl.BlockSpec((1, D_MODEL), lambda i, j: (0, 0)),
            pl.BlockSpec((D_MODEL, tf), lambda i, j: (0, j)),
            pl.BlockSpec((D_MODEL, tf), lambda i, j: (0, j + nf)),
            pl.BlockSpec((tf, D_MODEL), lambda i, j: (j, 0)),
            pl.BlockSpec((1, D_MODEL), lambda i, j: (0, 0)),
        ],
        out_specs=pl.BlockSpec((tm, D_MODEL), lambda i, j: (i, 0)),
        out_shape=jax.ShapeDtypeStruct((n, D_MODEL), F32),
        scratch_shapes=[pltpu.VMEM((tm, D_MODEL), BF16), pltpu.VMEM((tm, D_MODEL), F32)],
        compiler_params=pltpu.CompilerParams(
            dimension_semantics=("parallel", "arbitrary"), vmem_limit_bytes=VMEM_LIMIT),
    )(h, gpre, w_gu, w_gu, w_down, gpost)


def _proj_kernel(h_ref, g_ref, w_ref, wa_ref, ba_ref, bf_ref,
                 fq_ref, fk_ref, fv_ref, gq_ref, gk_ref, gv_ref, gr_ref, gma_ref, gmb_ref,
                 la_ref, lf_ref, *, tiles_per_seq):
    tm = h_ref.shape[0]
    xn = _rms(h_ref[...], g_ref[...]).astype(BF16)
    pos = (pl.program_id(0) % tiles_per_seq) * tm + lax.broadcasted_iota(jnp.int32, (tm, 1), 0)
    valid = (pos >= N_PAD).astype(F32)

    def proj(c0, width):
        return _dot(xn, w_ref[:, c0:c0 + width])

    fq_ref[...] = (proj(C_FQ, FOX_WIDTH) * (FOX_HEAD_DIM ** -0.5 * LOG2E)).astype(BF16)
    fk_ref[...] = proj(C_FK, FOX_WIDTH).astype(BF16)
    fv_ref[...] = proj(C_FV, FOX_WIDTH).astype(BF16)
    gq_ref[...] = proj(C_GQ, GLA_KWIDTH) * (GLA_DK ** -0.5)
    gk_ref[...] = proj(C_GK, GLA_KWIDTH) * valid
    gv_ref[...] = proj(C_GV, GLA_WIDTH).astype(BF16)
    gr_ref[...] = proj(C_GR, GLA_WIDTH).astype(BF16)
    gma_ref[...] = proj(C_GMA, D_MODEL).astype(BF16)
    gmb_ref[...] = proj(C_GMB, D_MODEL).astype(BF16)
    small = proj(C_SMALL, LANES)
    lane = lax.broadcasted_iota(jnp.int32, (tm, LANES), 1)
    lf = _log_sigmoid(small + bf_ref[...]) * (valid * LOG2E)
    lf_ref[...] = jnp.where(lane < FOX_HEADS, lf, 0.0)
    xa = _dot(small.astype(BF16), wa_ref[...]) + ba_ref[...]
    la_ref[...] = _log_sigmoid(xa) * (valid * (1.0 / GLA_GATE_TEMP))


def _proj(h, g, w_perm, wa_pad, ba, bf_pad, seq_len):
    n = h.shape[0]
    tm = ROW_TM
    row = lambda width: pl.BlockSpec((tm, width), lambda i: (i, 0))
    full = lambda a: pl.BlockSpec(a.shape, lambda i: (0, 0))
    outs = [
        (FOX_WIDTH, BF16), (FOX_WIDTH, BF16), (FOX_WIDTH, BF16),
        (GLA_KWIDTH, F32), (GLA_KWIDTH, F32), (GLA_WIDTH, BF16),
        (GLA_WIDTH, BF16), (D_MODEL, BF16), (D_MODEL, BF16),
        (GLA_KWIDTH, F32), (LANES, F32),
    ]
    return pl.pallas_call(
        functools.partial(_proj_kernel, tiles_per_seq=seq_len // tm),
        grid=(n // tm,),
        in_specs=[row(D_MODEL), full(g), full(w_perm), full(wa_pad), full(ba), full(bf_pad)],
        out_specs=[row(w) for w, _ in outs],
        out_shape=[jax.ShapeDtypeStruct((n, w), dt) for w, dt in outs],
        compiler_params=pltpu.CompilerParams(
            dimension_semantics=("parallel",), vmem_limit_bytes=VMEM_LIMIT),
    )(h, g, w_perm, wa_pad, ba, bf_pad)


def _gla_kernel(q_ref, k_ref, v_ref, la_ref, lf_ref, o_ref, cc_ref, cr_ref, st_ref, carry_ref):
    c = pl.program_id(1)

    @pl.when(c == 0)
    def _():
        st_ref[...] = jnp.zeros_like(st_ref)
        carry_ref[...] = jnp.zeros_like(carry_ref)

    row = lax.broadcasted_iota(jnp.int32, (BLOCK, BLOCK), 0)
    col = lax.broadcasted_iota(jnp.int32, (BLOCK, BLOCK), 1)
    causal = row >= col
    tril = causal.astype(BF16)
    tril_sub = (causal & ((row // SUB) == (col // SUB))).astype(BF16)

    cl = _dot_exact_lhs(tril, _split3(lf_ref[0])) + carry_ref[...]
    carry_ref[...] = cl[BLOCK - 1:BLOCK, :]
    cc_ref[0] = cl
    cr_ref[0] = cl.T[0:FOX_HEADS, :]

    la_parts = _split3(la_ref[0])
    cb = _dot_exact_lhs(tril, la_parts)
    cw = _dot_exact_lhs(tril_sub, la_parts)
    last = cb[BLOCK - 1:BLOCK, :]
    q = q_ref[0]
    k = k_ref[0]
    q_glob = (q * jnp.exp(cb)).astype(BF16)
    q_loc = (q * jnp.exp(cw)).astype(BF16)
    k_end = (k * jnp.exp(last - cb)).astype(BF16)
    decay = jnp.exp(last)

    row_w = lax.broadcasted_iota(jnp.int32, (BLOCK, GLA_KWIDTH), 0)
    k_sub = []
    for i in range(N_SUB):
        ref = cb[i * SUB - 1:i * SUB, :] if i > 0 else jnp.zeros_like(last)
        e = jnp.exp(jnp.minimum(ref - cb, EXP_CAP))
        k_sub.append(jnp.where(row_w < (i + 1) * SUB, k * e, 0.0).astype(BF16))

    row_blk = row // SUB
    v = v_ref[0]
    for pair in range(GLA_HEADS // 2):
        sl = slice(pair * LANES, (pair + 1) * LANES)
        k_cat = jnp.concatenate([ks[:, sl] for ks in k_sub], axis=1)
        for half in range(2):
            h = 2 * pair + half
            head_lanes = (col < GLA_DK) if half == 0 else (col >= GLA_DK)
            q_h = jnp.where(head_lanes, q_loc[:, sl], 0.0)
            q_cat = jnp.concatenate(
                [jnp.where(row_blk == i, q_h, 0.0) for i in range(N_SUB)], axis=1)
            att = jnp.where(causal, _dot_nt(q_cat, k_cat), 0.0).astype(BF16)
            v_h = v[:, h * GLA_DV:(h + 1) * GLA_DV]
            st = st_ref[h]
            inter = _dot_nt(jnp.where(head_lanes, q_glob[:, sl], 0.0), st.astype(BF16))
            o_ref[0, :, h * GLA_DV:(h + 1) * GLA_DV] = inter + _dot(att, v_h)
            st_ref[h] = st * decay[:, sl] + _dot_tn(v_h, k_end[:, sl])


def _gla(gq, gk, gv, la, lf, batch, seq_len):
    nc = seq_len // BLOCK
    blk = lambda width: pl.BlockSpec((1, BLOCK, width), lambda b, c: (b, c, 0))
    r3 = lambda a: a.reshape(batch, seq_len, a.shape[-1])
    return pl.pallas_call(
        _gla_kernel,
        grid=(batch, nc),
        in_specs=[blk(GLA_KWIDTH), blk(GLA_KWIDTH), blk(GLA_WIDTH), blk(GLA_KWIDTH), blk(LANES)],
        out_specs=[blk(GLA_WIDTH), blk(LANES),
                   pl.BlockSpec((1, FOX_HEADS, BLOCK), lambda b, c: (b, 0, c))],
        out_shape=[jax.ShapeDtypeStruct((batch, seq_len, GLA_WIDTH), F32),
                   jax.ShapeDtypeStruct((batch, seq_len, LANES), F32),
                   jax.ShapeDtypeStruct((batch, FOX_HEADS, seq_len), F32)],
        scratch_shapes=[pltpu.VMEM((GLA_HEADS, GLA_DV, LANES), F32), pltpu.VMEM((1, LANES), F32)],
        compiler_params=pltpu.CompilerParams(
            dimension_semantics=("parallel", "arbitrary"), vmem_limit_bytes=VMEM_LIMIT),
    )(r3(gq), r3(gk), r3(gv), r3(la), r3(lf))


def _fox_kernel(q_ref, k_ref, v_ref, cc_ref, crp_ref, crm_ref, cb_ref, o_ref,
                va_ref, vb_ref, m_ref, l_ref, acc_ref):
    t = FOX_T
    nt = crm_ref.shape[2]
    pair = pl.program_id(1)

    def stack_heads(x, lane_first):
        return jnp.concatenate([jnp.where(lane_first, x, 0.0), jnp.where(lane_first, 0.0, x)], axis=0)

    def head_cols(cc, lane):
        return jnp.concatenate(
            [jnp.sum(jnp.where(lane == 2 * pair + hh, cc, 0.0), axis=1, keepdims=True)
             for hh in range(2)], axis=0)

    sel = lax.broadcasted_iota(jnp.int32, (8, LANES), 0) == lax.broadcasted_iota(
        jnp.int32, (8, LANES), 1) // FOX_HEAD_DIM
    sel = sel.astype(BF16)

    def max_sq_norm(x):
        return jnp.max(_dot_nt(sel, x * x), axis=1, keepdims=True)

    lane_b = lax.broadcasted_iota(jnp.int32, (BLOCK, LANES), 1)
    first_b = lane_b < FOX_HEAD_DIM
    kp = k_ref[0, 0:BLOCK, :]
    vp = v_ref[0, 0:BLOCK, :]
    ckp = crp_ref[0, 0]
    row_b = lax.broadcasted_iota(jnp.int32, (BLOCK, BLOCK), 0)
    ok_b = (lane_b <= row_b) & ((lane_b >= N_PAD) | (lane_b == row_b))
    ok_b = jnp.concatenate([ok_b, ok_b], axis=0)
    s = _dot_nt(stack_heads(q_ref[0, 0:BLOCK, :], first_b), kp)
    z = jnp.concatenate([s[:BLOCK] - ckp[0:1], s[BLOCK:] - ckp[1:2]], axis=0)
    z = jnp.where(ok_b, z, NEG)
    cq2 = head_cols(cc_ref[0, 0:BLOCK, :], lane_b)
    m = jnp.max(z, axis=1, keepdims=True) + cq2
    p = jnp.exp2(z - (m - cq2))
    out = _dot(p.astype(BF16), vp) / jnp.sum(p, axis=1, keepdims=True)
    o_ref[0, 0:BLOCK, :] = jnp.where(first_b, out[:BLOCK], out[BLOCK:]).astype(o_ref.dtype)

    lane_s = lax.broadcasted_iota(jnp.int32, v_ref.shape[1:], 1)
    v_all = v_ref[0]
    va_ref[...] = jnp.where(lane_s < FOX_HEAD_DIM, v_all, jnp.where(lane_s == FOX_HEAD_DIM, 1.0, 0.0).astype(BF16))
    vb_ref[...] = jnp.where(lane_s >= FOX_HEAD_DIM, v_all, jnp.where(lane_s == 0, 1.0, 0.0).astype(BF16))

    k_max = jnp.sqrt(max_sq_norm(k_ref[0]))
    lane = lax.broadcasted_iota(jnp.int32, (t, LANES), 1)
    first = lane < FOX_HEAD_DIM
    lane2 = lax.broadcasted_iota(jnp.int32, (2, LANES), 1)
    lane1 = lane2[0:1]
    cb = cb_ref[0, 0]
    row_l = lax.broadcasted_iota(jnp.int32, (LANES, LANES), 0)
    ind_a = (row_l < FOX_HEAD_DIM).astype(BF16)
    ind_b = (row_l >= FOX_HEAD_DIM).astype(BF16)
    reps = t // LANES

    def main_tile(i, carry):
        r0 = pl.multiple_of(BLOCK + i * t, BLOCK)
        q = q_ref[0, pl.ds(r0, t), :]
        q2 = stack_heads(q, first)
        cq2 = head_cols(cc_ref[0, pl.ds(r0, t), :], lane)

        qsq = q * q
        ub2 = jnp.concatenate([jnp.sqrt(_dot(qsq, ind_a)) * k_max[0:1], jnp.sqrt(_dot(qsq, ind_b)) * k_max[1:2]],
                              axis=0) * NORM_UP
        ub_max = jnp.max(ub2)
        fast = ub_max <= FAST_MAX_BITS
        bound = jnp.where(fast, PRUNE_BITS, PRUNE_BITS + 2.0 * ub_max)
        c_start = jnp.sum(jnp.where(lane2 == nt + i, cb, 0.0), axis=1, keepdims=True)
        dead = (c_start - cb) < -bound
        dead = dead[0:1] & dead[1:2]
        n_skip = jnp.sum(jnp.where(dead & (lane1 < i), 1, 0))
        skip_prefix = jnp.sum(jnp.where(dead & (lane1 == 2 * nt), 1, 0))

        def logits(kt, ck, offs):
            s = _dot_nt(q2, kt)
            return jnp.concatenate([s[:t] - ck[0:1] - offs[:t], s[t:] - ck[1:2] - offs[t:]], axis=0)

        def run(update, offs_p, offs_m):
            acc_ref[...] = jnp.zeros_like(acc_ref)

            @pl.when(skip_prefix == 0)
            def _():
                z = logits(kp, ckp, offs_p)
                lane_k = lax.broadcasted_iota(jnp.int32, (2 * t, BLOCK), 1)
                update(jnp.where(lane_k >= N_PAD, z, NEG), slice(0, BLOCK))

            def step(j, diagonal):
                off = pl.multiple_of(BLOCK + j * t, BLOCK)
                z = logits(k_ref[0, pl.ds(off, t), :], crm_ref[0, 0, j], offs_m)
                if diagonal:
                    ok = lax.broadcasted_iota(jnp.int32, (t, t), 1) <= lax.broadcasted_iota(jnp.int32, (t, t), 0)
                    z = jnp.where(jnp.concatenate([ok, ok], axis=0), z, NEG)
                update(z, pl.ds(off, t))

            def body(j, c):
                step(j, False)
                return c

            lax.fori_loop(n_skip, i, body, 0)
            step(i, True)

        def finish(l_a, l_b):
            acc = acc_ref[...]
            out = jnp.where(first, acc[:t] / l_a, acc[t:] / l_b)
            o_ref[0, pl.ds(r0, t), :] = out.astype(o_ref.dtype)

        @pl.when(fast)
        def _():
            ref = ub2 - cq2

            def update(z, rows):
                p = jnp.exp2(z).astype(BF16)
                acc_ref[0:t] += _dot(p[:t], va_ref[rows, :])
                acc_ref[t:] += _dot(p[t:], vb_ref[rows, :])

            run(update, ref, jnp.concatenate([ref] * reps, axis=1))
            finish(acc_ref[0:t, FOX_HEAD_DIM:FOX_HEAD_DIM + 1], acc_ref[t:, 0:1])

        @pl.when(jnp.logical_not(fast))
        def _():
            m_ref[...] = jnp.full_like(m_ref, NEG)
            l_ref[...] = jnp.zeros_like(l_ref)

            def update(z, rows):
                m_old = m_ref[...]
                m_new = jnp.maximum(m_old, jnp.max(z, axis=1, keepdims=True) + cq2)
                p = jnp.exp2(z - (m_new - cq2))
                alpha = jnp.exp2(m_old - m_new)
                l_ref[...] = alpha * l_ref[...] + jnp.sum(p, axis=1, keepdims=True)
                acc_ref[...] = alpha * acc_ref[...] + _dot(p.astype(BF16), v_ref[0, rows, :])
                m_ref[...] = m_new

            run(update, jnp.zeros((2 * t, 1), F32), jnp.zeros((2 * t, 1), F32))
            finish(l_ref[0:t], l_ref[t:])

        return carry

    lax.fori_loop(0, nt, main_tile, 0)


def _fox(fq, fk, fv, cum_col, cum_row, batch, seq_len):
    t = FOX_T
    nt = (seq_len - BLOCK) // t
    npairs = FOX_HEADS // 2
    assert 2 * nt + 1 <= LANES
    r3 = lambda a: a.reshape(batch, seq_len, a.shape[-1])
    cr = cum_row.reshape(batch, npairs, 2, seq_len)
    cr_prefix = cr[..., :BLOCK]
    cr_main = cr[..., BLOCK:].reshape(batch, npairs, 2, nt, t).transpose(0, 1, 3, 2, 4)
    bounds = jnp.concatenate(
        [cr[..., BLOCK + t - 1::t], cr[..., BLOCK::t], cr[..., BLOCK - 1:BLOCK],
         jnp.zeros((batch, npairs, 2, LANES - 2 * nt - 1), F32)], axis=-1)
    whole = pl.BlockSpec((1, seq_len, LANES), lambda b, p: (b, 0, p))
    out = pl.pallas_call(
        _fox_kernel,
        grid=(batch, npairs),
        in_specs=[whole, whole, whole,
                  pl.BlockSpec((1, seq_len, LANES), lambda b, p: (b, 0, 0)),
                  pl.BlockSpec((1, 1, 2, BLOCK), lambda b, p: (b, p, 0, 0)),
                  pl.BlockSpec((1, 1, nt, 2, t), lambda b, p: (b, p, 0, 0, 0)),
                  pl.BlockSpec((1, 1, 2, LANES), lambda b, p: (b, p, 0, 0))],
        out_specs=whole,
        out_shape=jax.ShapeDtypeStruct((batch, seq_len, FOX_WIDTH), BF16),
        scratch_shapes=[pltpu.VMEM((seq_len, LANES), BF16), pltpu.VMEM((seq_len, LANES), BF16),
                        pltpu.VMEM((2 * t, 1), F32), pltpu.VMEM((2 * t, 1), F32),
                        pltpu.VMEM((2 * t, LANES), F32)],
        compiler_params=pltpu.CompilerParams(
            dimension_semantics=("parallel", "parallel"), vmem_limit_bytes=VMEM_LIMIT),
    )(r3(fq), r3(fk), r3(fv), cum_col, cr_prefix, cr_main, bounds)
    return out.reshape(batch * seq_len, FOX_WIDTH)


def _post_kernel(of_ref, og_ref, gr_ref, gma_ref, gmb_ref, h_ref, ggla_ref, wfo_ref, wgo_ref,
                 wout_ref, gpost_ref, o_ref):
    og = og_ref[...]
    ggla = ggla_ref[...]
    heads = []
    for hh in range(GLA_HEADS):
        sl = slice(hh * GLA_DV, (hh + 1) * GLA_DV)
        heads.append(_rms(og[:, sl], ggla[:, sl]))
    gr = gr_ref[...].astype(F32)
    o_gla = (jnp.concatenate(heads, axis=1) * (gr * jax.nn.sigmoid(gr))).astype(BF16)
    a = _dot(of_ref[...], wfo_ref[...])
    b = _dot(o_gla, wgo_ref[...])
    y = jax.nn.sigmoid(gma_ref[...].astype(F32)) * a + jax.nn.sigmoid(gmb_ref[...].astype(F32)) * b
    mix = _dot(y.astype(BF16), wout_ref[...])
    o_ref[...] = h_ref[...] + _rms(mix, gpost_ref[...])


def _post(o_fox, o_gla, gr, gma, gmb, h, ggla, wfo, wgo, wout, gpost):
    n = h.shape[0]
    tm = ROW_TM
    row = lambda a: pl.BlockSpec((tm, a.shape[1]), lambda i: (i, 0))
    full = lambda a: pl.BlockSpec(a.shape, lambda i: (0, 0))
    args = (o_fox, o_gla, gr, gma, gmb, h, ggla, wfo, wgo, wout, gpost)
    return pl.pallas_call(
        _post_kernel,
        grid=(n // tm,),
        in_specs=[row(a) for a in args[:6]] + [full(a) for a in args[6:]],
        out_specs=pl.BlockSpec((tm, D_MODEL), lambda i: (i, 0)),
        out_shape=jax.ShapeDtypeStruct((n, D_MODEL), F32),
        compiler_params=pltpu.CompilerParams(
            dimension_semantics=("parallel",), vmem_limit_bytes=VMEM_LIMIT),
    )(*args)


def _permute_w_in(w):
    d = w.shape[0]
    pad = jnp.zeros((d, LANES - FOX_HEADS - GLA_GATE_RANK), w.dtype)
    return jnp.concatenate(
        [w[:, 0:1536], w[:, 1544:2568], w[:, 2584:5144], w[:, 1536:1544], w[:, 2568:2584], pad],
        axis=1)


def kernel(x, meta_tokens, w_in, w_alpha_up, b_alpha, b_f, g_gla_out, w_fox_o, w_gla_o, w_out,
           w_ffn1_gu, w_ffn1_down, w_ffn2_gu, w_ffn2_down,
           g_pre_ffn1, g_post_ffn1, g_pre_mix, g_post_mix, g_pre_ffn2, g_post_ffn2):
    batch, seq, d = x.shape
    seq_len = seq + BLOCK
    depth = w_in.shape[0]
    assert d == D_MODEL and (seq_len - BLOCK) % FOX_T == 0 and seq_len % ROW_TM == 0
    assert (batch * seq_len) % FFN_TM == 0

    pad = jnp.zeros((batch, N_PAD, d), x.dtype)
    meta = jnp.broadcast_to(meta_tokens.astype(x.dtype)[None], (batch, N_META, d))
    h = jnp.concatenate([pad, meta, x], axis=1).reshape(batch * seq_len, d)

    row2 = lambda a: a.reshape(1, -1).astype(F32)
    for l in range(depth):
        w_perm = _permute_w_in(w_in[l]).astype(BF16)
        wa_pad = jnp.zeros((LANES, GLA_KWIDTH), F32).at[
            FOX_HEADS:FOX_HEADS + GLA_GATE_RANK].set(w_alpha_up[l]).astype(BF16)
        bf_pad = jnp.zeros((1, LANES), F32).at[0, :FOX_HEADS].set(b_f[l])

        h = _ffn(h, row2(g_pre_ffn1[l]), w_ffn1_gu[l].astype(BF16), w_ffn1_down[l].astype(BF16),
                 row2(g_post_ffn1[l]))
        fq, fk, fv, gq, gk, gv, gr, gma, gmb, la, lf = _proj(
            h, row2(g_pre_mix[l]), w_perm, wa_pad, row2(b_alpha[l]), bf_pad, seq_len)
        o_gla, cum_col, cum_row = _gla(gq, gk, gv, la, lf, batch, seq_len)
        o_fox = _fox(fq, fk, fv, cum_col, cum_row, batch, seq_len)
        h = _post(o_fox, o_gla.reshape(batch * seq_len, GLA_WIDTH), gr, gma, gmb, h,
                  row2(g_gla_out[l]), w_fox_o[l].astype(BF16), w_gla_o[l].astype(BF16),
                  w_out[l].astype(BF16), row2(g_post_mix[l]))
        h = _ffn(h, row2(g_pre_ffn2[l]), w_ffn2_gu[l].astype(BF16), w_ffn2_down[l].astype(BF16),
                 row2(g_post_ffn2[l]))
    return h.reshape(batch, seq_len, d)[:, BLOCK:]
```

```python
import functools

import jax
import jax.numpy as jnp
import numpy as np
from jax import lax
from jax.experimental import pallas as pl
from jax.experimental.pallas import tpu as pltpu

D_MODEL = 1024
N_META = 16
BLOCK = 128
N_PAD = BLOCK - N_META
FOX_HEADS = 8
FOX_HEAD_DIM = 64
FOX_WIDTH = FOX_HEADS * FOX_HEAD_DIM
GLA_HEADS = 4
GLA_WIDTH = 512
GLA_DV = 128
GLA_DK = 64
GLA_KWIDTH = GLA_HEADS * GLA_DK
GLA_GATE_RANK = 16
GLA_GATE_TEMP = 16.0
D_FF = 2816
RMS_EPS = 1e-6

LANES = 128
SUB = 16
N_SUB = BLOCK // SUB
EXP_CAP = 80.0
NEG = -1e30

FFN_TM = 640
FFN_TF = 256
ROW_TM = 640
FOX_T = 512
LOG2E = 1.4426950408889634
PRUNE_BITS = 152.0
NORM_UP = 1.01
FAST_MAX_BITS = 50.0
VMEM_LIMIT = 56 * 1024 * 1024

BF16 = jnp.bfloat16
F32 = jnp.float32

C_FQ, C_FK, C_FV = 0, 512, 1024
C_GQ, C_GK, C_GV = 1536, 1792, 2048
C_GR, C_GMA, C_GMB = 2560, 3072, 4096
C_SMALL = 5120
N_IN_PERM = 5248


def _dot(a, b):
    return jnp.dot(a, b, preferred_element_type=F32)


def _dot_nt(a, b):
    return lax.dot_general(a, b, (((1,), (1,)), ((), ())), preferred_element_type=F32)


def _dot_tn(a, b):
    return lax.dot_general(a, b, (((0,), (0,)), ((), ())), preferred_element_type=F32)


def _rms(x, g):
    return x * lax.rsqrt(jnp.mean(x * x, axis=-1, keepdims=True) + RMS_EPS) * g


def _log_sigmoid(x):
    return jnp.minimum(x, 0.0) - jnp.log1p(jnp.exp(-jnp.abs(x)))


def _split3(x):
    hi = x.astype(BF16)
    r1 = x - hi.astype(F32)
    mid = r1.astype(BF16)
    lo = (r1 - mid.astype(F32)).astype(BF16)
    return hi, mid, lo


def _dot_exact_lhs(m, parts):
    return _dot(m, parts[0]) + _dot(m, parts[1]) + _dot(m, parts[2])


def _ffn_kernel(h_ref, gpre_ref, wgu_ref, wd_ref, gpost_ref, o_ref, xn_ref, a_ref):
    xn_ref[...] = _rms(h_ref[...], gpre_ref[...]).astype(BF16)
    for c in range(D_FF // FFN_TF):
        lo = c * FFN_TF
        g = _dot(xn_ref[...], wgu_ref[:, lo:lo + FFN_TF])
        u = _dot(xn_ref[...], wgu_ref[:, D_FF + lo:D_FF + lo + FFN_TF])
        a_ref[:, lo:lo + FFN_TF] = (g * jax.nn.sigmoid(g) * u).astype(BF16)
    y = _dot(a_ref[...], wd_ref[...])
    o_ref[...] = h_ref[...] + 0.5 * _rms(y, gpost_ref[...])


def _resident(shape, layer):
    return pl.BlockSpec((None,) + shape, lambda i: (layer,) + (0,) * len(shape),
                        pipeline_mode=pl.Buffered(1))


def _ffn(h, gpre, w_gu, w_down, gpost, layer):
    n = h.shape[0]
    tm = FFN_TM
    return pl.pallas_call(
        _ffn_kernel,
        grid=(n // tm,),
        in_specs=[
            pl.BlockSpec((tm, D_MODEL), lambda i: (i, 0)),
            pl.BlockSpec((1, D_MODEL), lambda i: (0, 0)),
            _resident((D_MODEL, 2 * D_FF), layer),
            _resident((D_FF, D_MODEL), layer),
            pl.BlockSpec((1, D_MODEL), lambda i: (0, 0)),
        ],
        out_specs=pl.BlockSpec((tm, D_MODEL), lambda i: (i, 0)),
        out_shape=jax.ShapeDtypeStruct((n, D_MODEL), F32),
        scratch_shapes=[pltpu.VMEM((tm, D_MODEL), BF16), pltpu.VMEM((tm, D_FF), BF16)],
        compiler_params=pltpu.CompilerParams(
            dimension_semantics=("parallel",), vmem_limit_bytes=VMEM_LIMIT),
    )(h, gpre, w_gu, w_down, gpost)


def _proj_kernel(h_ref, g_ref, w_ref, wa_ref, ba_ref, bf_ref,
                 fq_ref, fk_ref, fv_ref, gq_ref, gk_ref, gv_ref, gr_ref, gma_ref, gmb_ref,
                 la_ref, lf_ref, *, tiles_per_seq):
    tm = h_ref.shape[0]
    xn = _rms(h_ref[...], g_ref[...]).astype(BF16)
    pos = (pl.program_id(0) % tiles_per_seq) * tm + lax.broadcasted_iota(jnp.int32, (tm, 1), 0)
    valid = (pos >= N_PAD).astype(F32)

    def proj(c0, width):
        return _dot(xn, w_ref[:, c0:c0 + width])

    fq_ref[...] = (proj(C_FQ, FOX_WIDTH) * (FOX_HEAD_DIM ** -0.5 * LOG2E)).astype(BF16)
    fk_ref[...] = proj(C_FK, FOX_WIDTH).astype(BF16)
    fv_ref[...] = proj(C_FV, FOX_WIDTH).astype(BF16)
    gq_ref[...] = proj(C_GQ, GLA_KWIDTH) * (GLA_DK ** -0.5)
    gk_ref[...] = proj(C_GK, GLA_KWIDTH) * valid
    gv_ref[...] = proj(C_GV, GLA_WIDTH).astype(BF16)
    gr_ref[...] = proj(C_GR, GLA_WIDTH).astype(BF16)
    gma_ref[...] = proj(C_GMA, D_MODEL).astype(BF16)
    gmb_ref[...] = proj(C_GMB, D_MODEL).astype(BF16)
    small = proj(C_SMALL, LANES)
    lane = lax.broadcasted_iota(jnp.int32, (tm, LANES), 1)
    lf = _log_sigmoid(small + bf_ref[...]) * (valid * LOG2E)
    lf_ref[...] = jnp.where(lane < FOX_HEADS, lf, 0.0)
    xa = _dot(small.astype(BF16), wa_ref[...]) + ba_ref[...]
    la_ref[...] = _log_sigmoid(xa) * (valid * (1.0 / GLA_GATE_TEMP))


def _proj(h, g, w_perm, wa_pad, ba, bf_pad, seq_len, layer):
    n = h.shape[0]
    tm = ROW_TM
    row = lambda width: pl.BlockSpec((tm, width), lambda i: (i, 0))
    full = lambda a: (_resident(a.shape[1:], layer) if a.ndim == 3
                      else pl.BlockSpec(a.shape, lambda i: (0, 0)))
    outs = [
        (FOX_WIDTH, BF16), (FOX_WIDTH, BF16), (FOX_WIDTH, BF16),
        (GLA_KWIDTH, F32), (GLA_KWIDTH, F32), (GLA_WIDTH, BF16),
        (GLA_WIDTH, BF16), (D_MODEL, BF16), (D_MODEL, BF16),
        (GLA_KWIDTH, F32), (LANES, F32),
    ]
    return pl.pallas_call(
        functools.partial(_proj_kernel, tiles_per_seq=seq_len // tm),
        grid=(n // tm,),
        in_specs=[row(D_MODEL), full(g), full(w_perm), full(wa_pad), full(ba), full(bf_pad)],
        out_specs=[row(w) for w, _ in outs],
        out_shape=[jax.ShapeDtypeStruct((n, w), dt) for w, dt in outs],
        compiler_params=pltpu.CompilerParams(
            dimension_semantics=("parallel",), vmem_limit_bytes=VMEM_LIMIT),
    )(h, g, w_perm, wa_pad, ba, bf_pad)


def _gla_kernel(q_ref, k_ref, v_ref, la_ref, lf_ref, o_ref, cc_ref, cr_ref, st_ref, carry_ref):
    c = pl.program_id(1)

    @pl.when(c == 0)
    def _():
        st_ref[...] = jnp.zeros_like(st_ref)
        carry_ref[...] = jnp.zeros_like(carry_ref)

    row = lax.broadcasted_iota(jnp.int32, (BLOCK, BLOCK), 0)
    col = lax.broadcasted_iota(jnp.int32, (BLOCK, BLOCK), 1)
    causal = row >= col
    tril = causal.astype(BF16)
    tril_sub = (causal & ((row // SUB) == (col // SUB))).astype(BF16)

    cl = _dot_exact_lhs(tril, _split3(lf_ref[0])) + carry_ref[...]
    carry_ref[...] = cl[BLOCK - 1:BLOCK, :]
    cc_ref[0] = cl
    cr_ref[0] = cl.T[0:FOX_HEADS, :]

    la_parts = _split3(la_ref[0])
    cb = _dot_exact_lhs(tril, la_parts)
    cw = _dot_exact_lhs(tril_sub, la_parts)
    last = cb[BLOCK - 1:BLOCK, :]
    q = q_ref[0]
    k = k_ref[0]
    q_glob = (q * jnp.exp(cb)).astype(BF16)
    q_loc = (q * jnp.exp(cw)).astype(BF16)
    k_end = (k * jnp.exp(last - cb)).astype(BF16)
    decay = jnp.exp(last)

    row_w = lax.broadcasted_iota(jnp.int32, (BLOCK, GLA_KWIDTH), 0)
    k_sub = []
    for i in range(N_SUB):
        ref = cb[i * SUB - 1:i * SUB, :] if i > 0 else jnp.zeros_like(last)
        e = jnp.exp(jnp.minimum(ref - cb, EXP_CAP))
        k_sub.append(jnp.where(row_w < (i + 1) * SUB, k * e, 0.0).astype(BF16))

    row_blk = row // SUB
    v = v_ref[0]
    for pair in range(GLA_HEADS // 2):
        sl = slice(pair * LANES, (pair + 1) * LANES)
        k_cat = jnp.concatenate([ks[:, sl] for ks in k_sub], axis=1)
        for half in range(2):
            h = 2 * pair + half
            head_lanes = (col < GLA_DK) if half == 0 else (col >= GLA_DK)
            q_h = jnp.where(head_lanes, q_loc[:, sl], 0.0)
            q_cat = jnp.concatenate(
                [jnp.where(row_blk == i, q_h, 0.0) for i in range(N_SUB)], axis=1)
            att = jnp.where(causal, _dot_nt(q_cat, k_cat), 0.0).astype(BF16)
            v_h = v[:, h * GLA_DV:(h + 1) * GLA_DV]
            st = st_ref[h]
            inter = _dot_nt(jnp.where(head_lanes, q_glob[:, sl], 0.0), st.astype(BF16))
            o_ref[0, :, h * GLA_DV:(h + 1) * GLA_DV] = inter + _dot(att, v_h)
            st_ref[h] = st * decay[:, sl] + _dot_tn(v_h, k_end[:, sl])


def _gla(gq, gk, gv, la, lf, batch, seq_len):
    nc = seq_len // BLOCK
    blk = lambda width: pl.BlockSpec((1, BLOCK, width), lambda b, c: (b, c, 0))
    r3 = lambda a: a.reshape(batch, seq_len, a.shape[-1])
    return pl.pallas_call(
        _gla_kernel,
        grid=(batch, nc),
        in_specs=[blk(GLA_KWIDTH), blk(GLA_KWIDTH), blk(GLA_WIDTH), blk(GLA_KWIDTH), blk(LANES)],
        out_specs=[blk(GLA_WIDTH), blk(LANES),
                   pl.BlockSpec((1, FOX_HEADS, BLOCK), lambda b, c: (b, 0, c))],
        out_shape=[jax.ShapeDtypeStruct((batch, seq_len, GLA_WIDTH), F32),
                   jax.ShapeDtypeStruct((batch, seq_len, LANES), F32),
                   jax.ShapeDtypeStruct((batch, FOX_HEADS, seq_len), F32)],
        scratch_shapes=[pltpu.VMEM((GLA_HEADS, GLA_DV, LANES), F32), pltpu.VMEM((1, LANES), F32)],
        compiler_params=pltpu.CompilerParams(
            dimension_semantics=("parallel", "arbitrary"), vmem_limit_bytes=VMEM_LIMIT),
    )(r3(gq), r3(gk), r3(gv), r3(la), r3(lf))


def _fox_kernel(q_ref, k_ref, v_ref, cc_ref, crp_ref, crm_ref, cb_ref, o_ref,
                va_ref, vb_ref, m_ref, l_ref, acc_ref):
    t = FOX_T
    nt = crm_ref.shape[2]
    pair = pl.program_id(1)

    def stack_heads(x, lane_first):
        return jnp.concatenate([jnp.where(lane_first, x, 0.0), jnp.where(lane_first, 0.0, x)], axis=0)

    def head_cols(cc, lane):
        return jnp.concatenate(
            [jnp.sum(jnp.where(lane == 2 * pair + hh, cc, 0.0), axis=1, keepdims=True)
             for hh in range(2)], axis=0)

    sel = lax.broadcasted_iota(jnp.int32, (8, LANES), 0) == lax.broadcasted_iota(
        jnp.int32, (8, LANES), 1) // FOX_HEAD_DIM
    sel = sel.astype(BF16)

    def max_sq_norm(x):
        return jnp.max(_dot_nt(sel, x * x), axis=1, keepdims=True)

    lane_b = lax.broadcasted_iota(jnp.int32, (BLOCK, LANES), 1)
    first_b = lane_b < FOX_HEAD_DIM
    kp = k_ref[0, 0:BLOCK, :]
    vp = v_ref[0, 0:BLOCK, :]
    ckp = crp_ref[0, 0]
    row_b = lax.broadcasted_iota(jnp.int32, (BLOCK, BLOCK), 0)
    ok_b = (lane_b <= row_b) & ((lane_b >= N_PAD) | (lane_b == row_b))
    ok_b = jnp.concatenate([ok_b, ok_b], axis=0)
    s = _dot_nt(stack_heads(q_ref[0, 0:BLOCK, :], first_b), kp)
    z = jnp.concatenate([s[:BLOCK] - ckp[0:1], s[BLOCK:] - ckp[1:2]], axis=0)
    z = jnp.where(ok_b, z, NEG)
    cq2 = head_cols(cc_ref[0, 0:BLOCK, :], lane_b)
    m = jnp.max(z, axis=1, keepdims=True) + cq2
    p = jnp.exp2(z - (m - cq2))
    out = _dot(p.astype(BF16), vp) / jnp.sum(p, axis=1, keepdims=True)
    o_ref[0, 0:BLOCK, :] = jnp.where(first_b, out[:BLOCK], out[BLOCK:]).astype(o_ref.dtype)

    lane_s = lax.broadcasted_iota(jnp.int32, v_ref.shape[1:], 1)
    v_all = v_ref[0]
    va_ref[...] = jnp.where(lane_s < FOX_HEAD_DIM, v_all, jnp.where(lane_s == FOX_HEAD_DIM, 1.0, 0.0).astype(BF16))
    vb_ref[...] = jnp.where(lane_s >= FOX_HEAD_DIM, v_all, jnp.where(lane_s == 0, 1.0, 0.0).astype(BF16))

    k_max = jnp.sqrt(max_sq_norm(k_ref[0]))
    lane = lax.broadcasted_iota(jnp.int32, (t, LANES), 1)
    first = lane < FOX_HEAD_DIM
    lane2 = lax.broadcasted_iota(jnp.int32, (2, LANES), 1)
    lane1 = lane2[0:1]
    cb = cb_ref[0, 0]
    row_l = lax.broadcasted_iota(jnp.int32, (LANES, LANES), 0)
    ind_a = (row_l < FOX_HEAD_DIM).astype(BF16)
    ind_b = (row_l >= FOX_HEAD_DIM).astype(BF16)
    reps = t // LANES

    def main_tile(i, carry):
        r0 = pl.multiple_of(BLOCK + i * t, BLOCK)
        q = q_ref[0, pl.ds(r0, t), :]
        q2 = stack_heads(q, first)
        cq2 = head_cols(cc_ref[0, pl.ds(r0, t), :], lane)

        qsq = q * q
        ub2 = jnp.concatenate([jnp.sqrt(_dot(qsq, ind_a)) * k_max[0:1], jnp.sqrt(_dot(qsq, ind_b)) * k_max[1:2]],
                              axis=0) * NORM_UP
        ub_max = jnp.max(ub2)
        fast = ub_max <= FAST_MAX_BITS
        bound = jnp.where(fast, PRUNE_BITS, PRUNE_BITS + 2.0 * ub_max)
        c_start = jnp.sum(jnp.where(lane2 == nt + i, cb, 0.0), axis=1, keepdims=True)
        dead = (c_start - cb) < -bound
        dead = dead[0:1] & dead[1:2]
        n_skip = jnp.sum(jnp.where(dead & (lane1 < i), 1, 0))
        skip_prefix = jnp.sum(jnp.where(dead & (lane1 == 2 * nt), 1, 0))

        def logits(kt, ck, offs):
            s = _dot_nt(q2, kt)
            return jnp.concatenate([s[:t] - ck[0:1] - offs[:t], s[t:] - ck[1:2] - offs[t:]], axis=0)

        def run(update, offs_p, offs_m):
            acc_ref[...] = jnp.zeros_like(acc_ref)

            @pl.when(skip_prefix == 0)
            def _():
                z = logits(kp, ckp, offs_p)
                lane_k = lax.broadcasted_iota(jnp.int32, (2 * t, BLOCK), 1)
                update(jnp.where(lane_k >= N_PAD, z, NEG), slice(0, BLOCK))

            def step(j, diagonal):
                off = pl.multiple_of(BLOCK + j * t, BLOCK)
                z = logits(k_ref[0, pl.ds(off, t), :], crm_ref[0, 0, j], offs_m)
                if diagonal:
                    ok = lax.broadcasted_iota(jnp.int32, (t, t), 1) <= lax.broadcasted_iota(jnp.int32, (t, t), 0)
                    z = jnp.where(jnp.concatenate([ok, ok], axis=0), z, NEG)
                update(z, pl.ds(off, t))

            def body(j, c):
                step(j, False)
                return c

            lax.fori_loop(n_skip, i, body, 0)
            step(i, True)

        def finish(l_a, l_b):
            acc = acc_ref[...]
            out = jnp.where(first, acc[:t] / l_a, acc[t:] / l_b)
            o_ref[0, pl.ds(r0, t), :] = out.astype(o_ref.dtype)

        @pl.when(fast)
        def _():
            ref = ub2 - cq2

            def update(z, rows):
                p = jnp.exp2(z).astype(BF16)
                acc_ref[0:t] += _dot(p[:t], va_ref[rows, :])
                acc_ref[t:] += _dot(p[t:], vb_ref[rows, :])

            run(update, ref, jnp.concatenate([ref] * reps, axis=1))
            finish(acc_ref[0:t, FOX_HEAD_DIM:FOX_HEAD_DIM + 1], acc_ref[t:, 0:1])

        @pl.when(jnp.logical_not(fast))
        def _():
            m_ref[...] = jnp.full_like(m_ref, NEG)
            l_ref[...] = jnp.zeros_like(l_ref)

            def update(z, rows):
                m_old = m_ref[...]
                m_new = jnp.maximum(m_old, jnp.max(z, axis=1, keepdims=True) + cq2)
                p = jnp.exp2(z - (m_new - cq2))
                alpha = jnp.exp2(m_old - m_new)
                l_ref[...] = alpha * l_ref[...] + jnp.sum(p, axis=1, keepdims=True)
                acc_ref[...] = alpha * acc_ref[...] + _dot(p.astype(BF16), v_ref[0, rows, :])
                m_ref[...] = m_new

            run(update, jnp.zeros((2 * t, 1), F32), jnp.zeros((2 * t, 1), F32))
            finish(l_ref[0:t], l_ref[t:])

        return carry

    lax.fori_loop(0, nt, main_tile, 0)


def _fox(fq, fk, fv, cum_col, cum_row, batch, seq_len):
    t = FOX_T
    nt = (seq_len - BLOCK) // t
    npairs = FOX_HEADS // 2
    assert 2 * nt + 1 <= LANES
    r3 = lambda a: a.reshape(batch, seq_len, a.shape[-1])
    cr = cum_row.reshape(batch, npairs, 2, seq_len)
    cr_prefix = cr[..., :BLOCK]
    cr_main = cr[..., BLOCK:].reshape(batch, npairs, 2, nt, t).transpose(0, 1, 3, 2, 4)
    bounds = jnp.concatenate(
        [cr[..., BLOCK + t - 1::t], cr[..., BLOCK::t], cr[..., BLOCK - 1:BLOCK],
         jnp.zeros((batch, npairs, 2, LANES - 2 * nt - 1), F32)], axis=-1)
    whole = pl.BlockSpec((1, seq_len, LANES), lambda b, p: (b, 0, p))
    out = pl.pallas_call(
        _fox_kernel,
        grid=(batch, npairs),
        in_specs=[whole, whole, whole,
                  pl.BlockSpec((1, seq_len, LANES), lambda b, p: (b, 0, 0)),
                  pl.BlockSpec((1, 1, 2, BLOCK), lambda b, p: (b, p, 0, 0)),
                  pl.BlockSpec((1, 1, nt, 2, t), lambda b, p: (b, p, 0, 0, 0)),
                  pl.BlockSpec((1, 1, 2, LANES), lambda b, p: (b, p, 0, 0))],
        out_specs=whole,
        out_shape=jax.ShapeDtypeStruct((batch, seq_len, FOX_WIDTH), BF16),
        scratch_shapes=[pltpu.VMEM((seq_len, LANES), BF16), pltpu.VMEM((seq_len, LANES), BF16),
                        pltpu.VMEM((2 * t, 1), F32), pltpu.VMEM((2 * t, 1), F32),
                        pltpu.VMEM((2 * t, LANES), F32)],
        compiler_params=pltpu.CompilerParams(
            dimension_semantics=("parallel", "parallel"), vmem_limit_bytes=VMEM_LIMIT),
    )(r3(fq), r3(fk), r3(fv), cum_col, cr_prefix, cr_main, bounds)
    return out.reshape(batch * seq_len, FOX_WIDTH)


def _post_kernel(of_ref, og_ref, gr_ref, gma_ref, gmb_ref, h_ref, ggla_ref, wfo_ref, wgo_ref,
                 wout_ref, gpost_ref, o_ref):
    og = og_ref[...]
    ggla = ggla_ref[...]
    heads = []
    for hh in range(GLA_HEADS):
        sl = slice(hh * GLA_DV, (hh + 1) * GLA_DV)
        heads.append(_rms(og[:, sl], ggla[:, sl]))
    gr = gr_ref[...].astype(F32)
    o_gla = (jnp.concatenate(heads, axis=1) * (gr * jax.nn.sigmoid(gr))).astype(BF16)
    a = _dot(of_ref[...], wfo_ref[...])
    b = _dot(o_gla, wgo_ref[...])
    y = jax.nn.sigmoid(gma_ref[...].astype(F32)) * a + jax.nn.sigmoid(gmb_ref[...].astype(F32)) * b
    mix = _dot(y.astype(BF16), wout_ref[...])
    o_ref[...] = h_ref[...] + _rms(mix, gpost_ref[...])


def _post(o_fox, o_gla, gr, gma, gmb, h, ggla, wfo, wgo, wout, gpost, layer):
    n = h.shape[0]
    tm = ROW_TM
    row = lambda a: pl.BlockSpec((tm, a.shape[1]), lambda i: (i, 0))
    full = lambda a: (_resident(a.shape[1:], layer) if a.ndim == 3
                      else pl.BlockSpec(a.shape, lambda i: (0, 0)))
    args = (o_fox, o_gla, gr, gma, gmb, h, ggla, wfo, wgo, wout, gpost)
    return pl.pallas_call(
        _post_kernel,
        grid=(n // tm,),
        in_specs=[row(a) for a in args[:6]] + [full(a) for a in args[6:]],
        out_specs=pl.BlockSpec((tm, D_MODEL), lambda i: (i, 0)),
        out_shape=jax.ShapeDtypeStruct((n, D_MODEL), F32),
        compiler_params=pltpu.CompilerParams(
            dimension_semantics=("parallel",), vmem_limit_bytes=VMEM_LIMIT),
    )(*args)


def _permute_w_in(w):
    pad = jnp.zeros(w.shape[:-1] + (LANES - FOX_HEADS - GLA_GATE_RANK,), w.dtype)
    return jnp.concatenate(
        [w[..., 0:1536], w[..., 1544:2568], w[..., 2584:5144], w[..., 1536:1544], w[..., 2568:2584], pad],
        axis=-1)


def kernel(x, meta_tokens, w_in, w_alpha_up, b_alpha, b_f, g_gla_out, w_fox_o, w_gla_o, w_out,
           w_ffn1_gu, w_ffn1_down, w_ffn2_gu, w_ffn2_down,
           g_pre_ffn1, g_post_ffn1, g_pre_mix, g_post_mix, g_pre_ffn2, g_post_ffn2):
    batch, seq, d = x.shape
    seq_len = seq + BLOCK
    depth = w_in.shape[0]
    assert d == D_MODEL and (seq_len - BLOCK) % FOX_T == 0 and seq_len % ROW_TM == 0
    assert (batch * seq_len) % FFN_TM == 0

    pad = jnp.zeros((batch, N_PAD, d), x.dtype)
    meta = jnp.broadcast_to(meta_tokens.astype(x.dtype)[None], (batch, N_META, d))
    h = jnp.concatenate([pad, meta, x], axis=1).reshape(batch * seq_len, d)

    row2 = lambda a: a.reshape(1, -1).astype(F32)
    w_perm = _permute_w_in(w_in).astype(BF16)
    w1_gu, w1_down = w_ffn1_gu.astype(BF16), w_ffn1_down.astype(BF16)
    w2_gu, w2_down = w_ffn2_gu.astype(BF16), w_ffn2_down.astype(BF16)
    wfo, wgo, wout = w_fox_o.astype(BF16), w_gla_o.astype(BF16), w_out.astype(BF16)
    for l in range(depth):
        wa_pad = jnp.zeros((LANES, GLA_KWIDTH), F32).at[
            FOX_HEADS:FOX_HEADS + GLA_GATE_RANK].set(w_alpha_up[l]).astype(BF16)
        bf_pad = jnp.zeros((1, LANES), F32).at[0, :FOX_HEADS].set(b_f[l])

        h = _ffn(h, row2(g_pre_ffn1[l]), w1_gu, w1_down, row2(g_post_ffn1[l]), l)
        fq, fk, fv, gq, gk, gv, gr, gma, gmb, la, lf = _proj(
            h, row2(g_pre_mix[l]), w_perm, wa_pad, row2(b_alpha[l]), bf_pad, seq_len, l)
        o_gla, cum_col, cum_row = _gla(gq, gk, gv, la, lf, batch, seq_len)
        o_fox = _fox(fq, fk, fv, cum_col, cum_row, batch, seq_len)
        h = _post(o_fox, o_gla.reshape(batch * seq_len, GLA_WIDTH), gr, gma, gmb, h,
                  row2(g_gla_out[l]), wfo, wgo, wout, row2(g_post_mix[l]), l)
        h = _ffn(h, row2(g_pre_ffn2[l]), w2_gu, w2_down, row2(g_post_ffn2[l]), l)
    return h.reshape(batch, seq_len, d)[:, BLOCK:]
```

```python
import functools

import jax
import jax.numpy as jnp
import numpy as np
from jax import lax
from jax.experimental import pallas as pl
from jax.experimental.pallas import tpu as pltpu

D_MODEL = 1024
N_META = 16
BLOCK = 128
N_PAD = BLOCK - N_META
FOX_HEADS = 8
FOX_HEAD_DIM = 64
FOX_WIDTH = FOX_HEADS * FOX_HEAD_DIM
GLA_HEADS = 4
GLA_WIDTH = 512
GLA_DV = 128
GLA_DK = 64
GLA_KWIDTH = GLA_HEADS * GLA_DK
GLA_GATE_RANK = 16
GLA_GATE_TEMP = 16.0
D_FF = 2816
RMS_EPS = 1e-6

LANES = 128
SUB = 16
N_SUB = BLOCK // SUB
EXP_CAP = 80.0
NEG = -1e30

FFN_TM = 640
FFN_TF = 256
ROW_TM = 640
FOX_T = 512
LOG2E = 1.4426950408889634
PRUNE_BITS = 152.0
NORM_UP = 1.01
FAST_MAX_BITS = 50.0
VMEM_LIMIT = 56 * 1024 * 1024

BF16 = jnp.bfloat16
F32 = jnp.float32

C_FQ, C_FK, C_FV = 0, 512, 1024
C_GQ, C_GK, C_GV = 1536, 1792, 2048
C_GR, C_GMA, C_GMB = 2560, 3072, 4096
C_SMALL = 5120
N_IN_PERM = 5248


def _dot(a, b):
    return jnp.dot(a, b, preferred_element_type=F32)


def _dot_nt(a, b):
    return lax.dot_general(a, b, (((1,), (1,)), ((), ())), preferred_element_type=F32)


def _dot_tn(a, b):
    return lax.dot_general(a, b, (((0,), (0,)), ((), ())), preferred_element_type=F32)


def _rms(x, g):
    return x * lax.rsqrt(jnp.mean(x * x, axis=-1, keepdims=True) + RMS_EPS) * g


def _log_sigmoid(x):
    return jnp.minimum(x, 0.0) - jnp.log1p(jnp.exp(-jnp.abs(x)))


def _split3(x):
    hi = x.astype(BF16)
    r1 = x - hi.astype(F32)
    mid = r1.astype(BF16)
    lo = (r1 - mid.astype(F32)).astype(BF16)
    return hi, mid, lo


def _dot_exact_lhs(m, parts):
    return _dot(m, parts[0]) + _dot(m, parts[1]) + _dot(m, parts[2])


def _ffn_kernel(h_ref, gpre_ref, wgu_ref, wd_ref, gpost_ref, o_ref, xn_ref, a_ref):
    xn_ref[...] = _rms(h_ref[...], gpre_ref[...]).astype(BF16)
    for c in range(D_FF // FFN_TF):
        lo = c * FFN_TF
        g = _dot(xn_ref[...], wgu_ref[:, lo:lo + FFN_TF])
        u = _dot(xn_ref[...], wgu_ref[:, D_FF + lo:D_FF + lo + FFN_TF])
        a_ref[:, lo:lo + FFN_TF] = (g * jax.nn.sigmoid(g) * u).astype(BF16)
    y = _dot(a_ref[...], wd_ref[...])
    o_ref[...] = h_ref[...] + 0.5 * _rms(y, gpost_ref[...])


def _resident(shape, layer):
    return pl.BlockSpec((None,) + shape, lambda i: (layer,) + (0,) * len(shape),
                        pipeline_mode=pl.Buffered(1))


def _ffn(h, gpre, w_gu, w_down, gpost, layer):
    n = h.shape[0]
    tm = FFN_TM
    return pl.pallas_call(
        _ffn_kernel,
        grid=(n // tm,),
        in_specs=[
            pl.BlockSpec((tm, D_MODEL), lambda i: (i, 0)),
            pl.BlockSpec((1, D_MODEL), lambda i: (0, 0)),
            _resident((D_MODEL, 2 * D_FF), layer),
            _resident((D_FF, D_MODEL), layer),
            pl.BlockSpec((1, D_MODEL), lambda i: (0, 0)),
        ],
        out_specs=pl.BlockSpec((tm, D_MODEL), lambda i: (i, 0)),
        out_shape=jax.ShapeDtypeStruct((n, D_MODEL), F32),
        scratch_shapes=[pltpu.VMEM((tm, D_MODEL), BF16), pltpu.VMEM((tm, D_FF), BF16)],
        compiler_params=pltpu.CompilerParams(
            dimension_semantics=("parallel",), vmem_limit_bytes=VMEM_LIMIT),
    )(h, gpre, w_gu, w_down, gpost)


def _proj_kernel(h_ref, g_ref, w_ref, wa_ref, ba_ref, bf_ref,
                 fq_ref, fk_ref, fv_ref, gq_ref, gk_ref, gv_ref, gr_ref, gma_ref, gmb_ref,
                 la_ref, lf_ref, *, tiles_per_seq):
    tm = h_ref.shape[0]
    xn = _rms(h_ref[...], g_ref[...]).astype(BF16)
    pos = (pl.program_id(0) % tiles_per_seq) * tm + lax.broadcasted_iota(jnp.int32, (tm, 1), 0)
    valid = (pos >= N_PAD).astype(F32)

    def proj(c0, width):
        return _dot(xn, w_ref[:, c0:c0 + width])

    fq_ref[...] = (proj(C_FQ, FOX_WIDTH) * (FOX_HEAD_DIM ** -0.5 * LOG2E)).astype(BF16)
    fk_ref[...] = proj(C_FK, FOX_WIDTH).astype(BF16)
    fv_ref[...] = proj(C_FV, FOX_WIDTH).astype(BF16)
    gq_ref[...] = proj(C_GQ, GLA_KWIDTH) * (GLA_DK ** -0.5)
    gk_ref[...] = proj(C_GK, GLA_KWIDTH) * valid
    gv_ref[...] = proj(C_GV, GLA_WIDTH).astype(BF16)
    gr_ref[...] = proj(C_GR, GLA_WIDTH).astype(BF16)
    gma_ref[...] = proj(C_GMA, D_MODEL).astype(BF16)
    gmb_ref[...] = proj(C_GMB, D_MODEL).astype(BF16)
    small = proj(C_SMALL, LANES)
    lane = lax.broadcasted_iota(jnp.int32, (tm, LANES), 1)
    lf = _log_sigmoid(small + bf_ref[...]) * (valid * LOG2E)
    lf_ref[...] = jnp.where(lane < FOX_HEADS, lf, 0.0)
    xa = _dot(small.astype(BF16), wa_ref[...]) + ba_ref[...]
    la_ref[...] = _log_sigmoid(xa) * (valid * (1.0 / GLA_GATE_TEMP))


def _proj(h, g, w_perm, wa_pad, ba, bf_pad, seq_len, layer):
    n = h.shape[0]
    tm = ROW_TM
    row = lambda width: pl.BlockSpec((tm, width), lambda i: (i, 0))
    full = lambda a: (_resident(a.shape[1:], layer) if a.ndim == 3
                      else pl.BlockSpec(a.shape, lambda i: (0, 0)))
    outs = [
        (FOX_WIDTH, BF16), (FOX_WIDTH, BF16), (FOX_WIDTH, BF16),
        (GLA_KWIDTH, F32), (GLA_KWIDTH, F32), (GLA_WIDTH, BF16),
        (GLA_WIDTH, BF16), (D_MODEL, BF16), (D_MODEL, BF16),
        (GLA_KWIDTH, F32), (LANES, F32),
    ]
    return pl.pallas_call(
        functools.partial(_proj_kernel, tiles_per_seq=seq_len // tm),
        grid=(n // tm,),
        in_specs=[row(D_MODEL), full(g), full(w_perm), full(wa_pad), full(ba), full(bf_pad)],
        out_specs=[row(w) for w, _ in outs],
        out_shape=[jax.ShapeDtypeStruct((n, w), dt) for w, dt in outs],
        compiler_params=pltpu.CompilerParams(
            dimension_semantics=("parallel",), vmem_limit_bytes=VMEM_LIMIT),
    )(h, g, w_perm, wa_pad, ba, bf_pad)


def _gla_kernel(q_ref, k_ref, v_ref, la_ref, lf_ref, o_ref, cc_ref, cr_ref, st_ref, carry_ref):
    @pl.when(pl.program_id(0) == 0)
    def _():
        st_ref[...] = jnp.zeros_like(st_ref)
        carry_ref[...] = jnp.zeros_like(carry_ref)

    row = lax.broadcasted_iota(jnp.int32, (BLOCK, BLOCK), 0)
    col = lax.broadcasted_iota(jnp.int32, (BLOCK, BLOCK), 1)
    causal = row >= col
    tril = causal.astype(BF16)
    tril_sub = (causal & ((row // SUB) == (col // SUB))).astype(BF16)
    row_w = lax.broadcasted_iota(jnp.int32, (BLOCK, GLA_KWIDTH), 0)
    row_blk = row // SUB

    batches = range(q_ref.shape[0])
    problems = [(b, h) for b in batches for h in range(GLA_HEADS)]

    cb_all, cw_all = [], []
    for b in batches:
        cl = _dot_exact_lhs(tril, _split3(lf_ref[b])) + carry_ref[b]
        carry_ref[b] = cl[BLOCK - 1:BLOCK, :]
        cc_ref[b] = cl
        cr_ref[b] = cl.T[0:FOX_HEADS, :]
        la_parts = _split3(la_ref[b])
        cb_all.append(_dot_exact_lhs(tril, la_parts))
        cw_all.append(_dot_exact_lhs(tril_sub, la_parts))

    ops = {}
    for b in batches:
        cb, cw = cb_all[b], cw_all[b]
        last = cb[BLOCK - 1:BLOCK, :]
        q = q_ref[b]
        k = k_ref[b]
        q_glob = (q * jnp.exp(cb)).astype(BF16)
        q_loc = (q * jnp.exp(cw)).astype(BF16)
        k_end = (k * jnp.exp(last - cb)).astype(BF16)
        decay = jnp.exp(last)
        k_sub = []
        for i in range(N_SUB):
            ref = cb[i * SUB - 1:i * SUB, :] if i > 0 else jnp.zeros_like(last)
            e = jnp.exp(jnp.minimum(ref - cb, EXP_CAP))
            k_sub.append(jnp.where(row_w < (i + 1) * SUB, k * e, 0.0).astype(BF16))
        for pair in range(GLA_HEADS // 2):
            sl = slice(pair * LANES, (pair + 1) * LANES)
            k_cat = jnp.concatenate([ks[:, sl] for ks in k_sub], axis=1)
            for half in range(2):
                h = 2 * pair + half
                head_lanes = (col < GLA_DK) if half == 0 else (col >= GLA_DK)
                q_h = jnp.where(head_lanes, q_loc[:, sl], 0.0)
                q_cat = jnp.concatenate(
                    [jnp.where(row_blk == i, q_h, 0.0) for i in range(N_SUB)], axis=1)
                ops[b, h] = (q_cat, k_cat, jnp.where(head_lanes, q_glob[:, sl], 0.0), k_end[:, sl],
                             decay[:, sl], v_ref[b, :, h * GLA_DV:(h + 1) * GLA_DV])

    att, inter, upd = {}, {}, {}
    for p in problems:
        q_cat, k_cat, q_g, k_e, _, v_h = ops[p]
        att[p] = _dot_nt(q_cat, k_cat)
        inter[p] = _dot_nt(q_g, st_ref[p].astype(BF16))
        upd[p] = _dot_tn(v_h, k_e)

    for p in problems:
        b, h = p
        a = jnp.where(causal, att[p], 0.0).astype(BF16)
        o_ref[b, :, h * GLA_DV:(h + 1) * GLA_DV] = inter[p] + _dot(a, ops[p][5])
        st_ref[p] = st_ref[p] * ops[p][4] + upd[p]


def _gla(gq, gk, gv, la, lf, batch, seq_len):
    nc = seq_len // BLOCK
    blk = lambda width: pl.BlockSpec((batch, BLOCK, width), lambda c: (0, c, 0))
    r3 = lambda a: a.reshape(batch, seq_len, a.shape[-1])
    return pl.pallas_call(
        _gla_kernel,
        grid=(nc,),
        in_specs=[blk(GLA_KWIDTH), blk(GLA_KWIDTH), blk(GLA_WIDTH), blk(GLA_KWIDTH), blk(LANES)],
        out_specs=[blk(GLA_WIDTH), blk(LANES),
                   pl.BlockSpec((batch, FOX_HEADS, BLOCK), lambda c: (0, 0, c))],
        out_shape=[jax.ShapeDtypeStruct((batch, seq_len, GLA_WIDTH), F32),
                   jax.ShapeDtypeStruct((batch, seq_len, LANES), F32),
                   jax.ShapeDtypeStruct((batch, FOX_HEADS, seq_len), F32)],
        scratch_shapes=[pltpu.VMEM((batch, GLA_HEADS, GLA_DV, LANES), F32),
                        pltpu.VMEM((batch, 1, LANES), F32)],
        compiler_params=pltpu.CompilerParams(
            dimension_semantics=("arbitrary",), vmem_limit_bytes=VMEM_LIMIT),
    )(r3(gq), r3(gk), r3(gv), r3(la), r3(lf))


def _fox_kernel(q_ref, k_ref, v_ref, cc_ref, crp_ref, crm_ref, cb_ref, o_ref,
                va_ref, vb_ref, m_ref, l_ref, acc_ref):
    t = FOX_T
    nt = crm_ref.shape[2]
    pair = pl.program_id(1)

    def stack_heads(x, lane_first):
        return jnp.concatenate([jnp.where(lane_first, x, 0.0), jnp.where(lane_first, 0.0, x)], axis=0)

    def head_cols(cc, lane):
        return jnp.concatenate(
            [jnp.sum(jnp.where(lane == 2 * pair + hh, cc, 0.0), axis=1, keepdims=True)
             for hh in range(2)], axis=0)

    sel = lax.broadcasted_iota(jnp.int32, (8, LANES), 0) == lax.broadcasted_iota(
        jnp.int32, (8, LANES), 1) // FOX_HEAD_DIM
    sel = sel.astype(BF16)

    def max_sq_norm(x):
        return jnp.max(_dot_nt(sel, x * x), axis=1, keepdims=True)

    lane_b = lax.broadcasted_iota(jnp.int32, (BLOCK, LANES), 1)
    first_b = lane_b < FOX_HEAD_DIM
    kp = k_ref[0, 0:BLOCK, :]
    vp = v_ref[0, 0:BLOCK, :]
    ckp = crp_ref[0, 0]
    row_b = lax.broadcasted_iota(jnp.int32, (BLOCK, BLOCK), 0)
    ok_b = (lane_b <= row_b) & ((lane_b >= N_PAD) | (lane_b == row_b))
    ok_b = jnp.concatenate([ok_b, ok_b], axis=0)
    s = _dot_nt(stack_heads(q_ref[0, 0:BLOCK, :], first_b), kp)
    z = jnp.concatenate([s[:BLOCK] - ckp[0:1], s[BLOCK:] - ckp[1:2]], axis=0)
    z = jnp.where(ok_b, z, NEG)
    cq2 = head_cols(cc_ref[0, 0:BLOCK, :], lane_b)
    m = jnp.max(z, axis=1, keepdims=True) + cq2
    p = jnp.exp2(z - (m - cq2))
    out = _dot(p.astype(BF16), vp) / jnp.sum(p, axis=1, keepdims=True)
    o_ref[0, 0:BLOCK, :] = jnp.where(first_b, out[:BLOCK], out[BLOCK:]).astype(o_ref.dtype)

    lane_s = lax.broadcasted_iota(jnp.int32, v_ref.shape[1:], 1)
    v_all = v_ref[0]
    va_ref[...] = jnp.where(lane_s < FOX_HEAD_DIM, v_all, jnp.where(lane_s == FOX_HEAD_DIM, 1.0, 0.0).astype(BF16))
    vb_ref[...] = jnp.where(lane_s >= FOX_HEAD_DIM, v_all, jnp.where(lane_s == 0, 1.0, 0.0).astype(BF16))

    k_max = jnp.sqrt(max_sq_norm(k_ref[0]))
    lane = lax.broadcasted_iota(jnp.int32, (t, LANES), 1)
    first = lane < FOX_HEAD_DIM
    lane2 = lax.broadcasted_iota(jnp.int32, (2, LANES), 1)
    lane1 = lane2[0:1]
    cb = cb_ref[0, 0]
    row_l = lax.broadcasted_iota(jnp.int32, (LANES, LANES), 0)
    ind_a = (row_l < FOX_HEAD_DIM).astype(BF16)
    ind_b = (row_l >= FOX_HEAD_DIM).astype(BF16)
    reps = t // LANES

    def main_tile(i, carry):
        r0 = pl.multiple_of(BLOCK + i * t, BLOCK)
        q = q_ref[0, pl.ds(r0, t), :]
        q2 = stack_heads(q, first)
        cq2 = head_cols(cc_ref[0, pl.ds(r0, t), :], lane)

        qsq = q * q
        ub2 = jnp.concatenate([jnp.sqrt(_dot(qsq, ind_a)) * k_max[0:1], jnp.sqrt(_dot(qsq, ind_b)) * k_max[1:2]],
                              axis=0) * NORM_UP
        ub_max = jnp.max(ub2)
        fast = ub_max <= FAST_MAX_BITS
        bound = jnp.where(fast, PRUNE_BITS, PRUNE_BITS + 2.0 * ub_max)
        c_start = jnp.sum(jnp.where(lane2 == nt + i, cb, 0.0), axis=1, keepdims=True)
        dead = (c_start - cb) < -bound
        dead = dead[0:1] & dead[1:2]
        n_skip = jnp.sum(jnp.where(dead & (lane1 < i), 1, 0))
        skip_prefix = jnp.sum(jnp.where(dead & (lane1 == 2 * nt), 1, 0))

        def logits(kt, ck, offs):
            s = _dot_nt(q2, kt)
            return jnp.concatenate([s[:t] - ck[0:1] - offs[:t], s[t:] - ck[1:2] - offs[t:]], axis=0)

        def run(update, offs_p, offs_m):
            acc_ref[...] = jnp.zeros_like(acc_ref)

            @pl.when(skip_prefix == 0)
            def _():
                z = logits(kp, ckp, offs_p)
                lane_k = lax.broadcasted_iota(jnp.int32, (2 * t, BLOCK), 1)
                update(jnp.where(lane_k >= N_PAD, z, NEG), slice(0, BLOCK))

            def step(j, diagonal):
                off = pl.multiple_of(BLOCK + j * t, BLOCK)
                z = logits(k_ref[0, pl.ds(off, t), :], crm_ref[0, 0, j], offs_m)
                if diagonal:
                    ok = lax.broadcasted_iota(jnp.int32, (t, t), 1) <= lax.broadcasted_iota(jnp.int32, (t, t), 0)
                    z = jnp.where(jnp.concatenate([ok, ok], axis=0), z, NEG)
                update(z, pl.ds(off, t))

            def body(j, c):
                step(j, False)
                return c

            lax.fori_loop(n_skip, i, body, 0)
            step(i, True)

        def finish(l_a, l_b):
            acc = acc_ref[...]
            out = jnp.where(first, acc[:t] / l_a, acc[t:] / l_b)
            o_ref[0, pl.ds(r0, t), :] = out.astype(o_ref.dtype)

        @pl.when(fast)
        def _():
            ref = ub2 - cq2

            def update(z, rows):
                p = jnp.exp2(z).astype(BF16)
                acc_ref[0:t] += _dot(p[:t], va_ref[rows, :])
                acc_ref[t:] += _dot(p[t:], vb_ref[rows, :])

            run(update, ref, jnp.concatenate([ref] * reps, axis=1))
            finish(acc_ref[0:t, FOX_HEAD_DIM:FOX_HEAD_DIM + 1], acc_ref[t:, 0:1])

        @pl.when(jnp.logical_not(fast))
        def _():
            m_ref[...] = jnp.full_like(m_ref, NEG)
            l_ref[...] = jnp.zeros_like(l_ref)

            def update(z, rows):
                m_old = m_ref[...]
                m_new = jnp.maximum(m_old, jnp.max(z, axis=1, keepdims=True) + cq2)
                p = jnp.exp2(z - (m_new - cq2))
                alpha = jnp.exp2(m_old - m_new)
                l_ref[...] = alpha * l_ref[...] + jnp.sum(p, axis=1, keepdims=True)
                acc_ref[...] = alpha * acc_ref[...] + _dot(p.astype(BF16), v_ref[0, rows, :])
                m_ref[...] = m_new

            run(update, jnp.zeros((2 * t, 1), F32), jnp.zeros((2 * t, 1), F32))
            finish(l_ref[0:t], l_ref[t:])

        return carry

    lax.fori_loop(0, nt, main_tile, 0)


def _fox(fq, fk, fv, cum_col, cum_row, batch, seq_len):
    t = FOX_T
    nt = (seq_len - BLOCK) // t
    npairs = FOX_HEADS // 2
    assert 2 * nt + 1 <= LANES
    r3 = lambda a: a.reshape(batch, seq_len, a.shape[-1])
    cr = cum_row.reshape(batch, npairs, 2, seq_len)
    cr_prefix = cr[..., :BLOCK]
    cr_main = cr[..., BLOCK:].reshape(batch, npairs, 2, nt, t).transpose(0, 1, 3, 2, 4)
    bounds = jnp.concatenate(
        [cr[..., BLOCK + t - 1::t], cr[..., BLOCK::t], cr[..., BLOCK - 1:BLOCK],
         jnp.zeros((batch, npairs, 2, LANES - 2 * nt - 1), F32)], axis=-1)
    whole = pl.BlockSpec((1, seq_len, LANES), lambda b, p: (b, 0, p))
    out = pl.pallas_call(
        _fox_kernel,
        grid=(batch, npairs),
        in_specs=[whole, whole, whole,
                  pl.BlockSpec((1, seq_len, LANES), lambda b, p: (b, 0, 0)),
                  pl.BlockSpec((1, 1, 2, BLOCK), lambda b, p: (b, p, 0, 0)),
                  pl.BlockSpec((1, 1, nt, 2, t), lambda b, p: (b, p, 0, 0, 0)),
                  pl.BlockSpec((1, 1, 2, LANES), lambda b, p: (b, p, 0, 0))],
        out_specs=whole,
        out_shape=jax.ShapeDtypeStruct((batch, seq_len, FOX_WIDTH), BF16),
        scratch_shapes=[pltpu.VMEM((seq_len, LANES), BF16), pltpu.VMEM((seq_len, LANES), BF16),
                        pltpu.VMEM((2 * t, 1), F32), pltpu.VMEM((2 * t, 1), F32),
                        pltpu.VMEM((2 * t, LANES), F32)],
        compiler_params=pltpu.CompilerParams(
            dimension_semantics=("parallel", "parallel"), vmem_limit_bytes=VMEM_LIMIT),
    )(r3(fq), r3(fk), r3(fv), cum_col, cr_prefix, cr_main, bounds)
    return out.reshape(batch * seq_len, FOX_WIDTH)


def _post_kernel(of_ref, og_ref, gr_ref, gma_ref, gmb_ref, h_ref, ggla_ref, wfo_ref, wgo_ref,
                 wout_ref, gpost_ref, o_ref):
    og = og_ref[...]
    ggla = ggla_ref[...]
    heads = []
    for hh in range(GLA_HEADS):
        sl = slice(hh * GLA_DV, (hh + 1) * GLA_DV)
        heads.append(_rms(og[:, sl], ggla[:, sl]))
    gr = gr_ref[...].astype(F32)
    o_gla = (jnp.concatenate(heads, axis=1) * (gr * jax.nn.sigmoid(gr))).astype(BF16)
    a = _dot(of_ref[...], wfo_ref[...])
    b = _dot(o_gla, wgo_ref[...])
    y = jax.nn.sigmoid(gma_ref[...].astype(F32)) * a + jax.nn.sigmoid(gmb_ref[...].astype(F32)) * b
    mix = _dot(y.astype(BF16), wout_ref[...])
    o_ref[...] = h_ref[...] + _rms(mix, gpost_ref[...])


def _post(o_fox, o_gla, gr, gma, gmb, h, ggla, wfo, wgo, wout, gpost, layer):
    n = h.shape[0]
    tm = ROW_TM
    row = lambda a: pl.BlockSpec((tm, a.shape[1]), lambda i: (i, 0))
    full = lambda a: (_resident(a.shape[1:], layer) if a.ndim == 3
                      else pl.BlockSpec(a.shape, lambda i: (0, 0)))
    args = (o_fox, o_gla, gr, gma, gmb, h, ggla, wfo, wgo, wout, gpost)
    return pl.pallas_call(
        _post_kernel,
        grid=(n // tm,),
        in_specs=[row(a) for a in args[:6]] + [full(a) for a in args[6:]],
        out_specs=pl.BlockSpec((tm, D_MODEL), lambda i: (i, 0)),
        out_shape=jax.ShapeDtypeStruct((n, D_MODEL), F32),
        compiler_params=pltpu.CompilerParams(
            dimension_semantics=("parallel",), vmem_limit_bytes=VMEM_LIMIT),
    )(*args)


def _permute_w_in(w):
    pad = jnp.zeros(w.shape[:-1] + (LANES - FOX_HEADS - GLA_GATE_RANK,), w.dtype)
    return jnp.concatenate(
        [w[..., 0:1536], w[..., 1544:2568], w[..., 2584:5144], w[..., 1536:1544], w[..., 2568:2584], pad],
        axis=-1)


def kernel(x, meta_tokens, w_in, w_alpha_up, b_alpha, b_f, g_gla_out, w_fox_o, w_gla_o, w_out,
           w_ffn1_gu, w_ffn1_down, w_ffn2_gu, w_ffn2_down,
           g_pre_ffn1, g_post_ffn1, g_pre_mix, g_post_mix, g_pre_ffn2, g_post_ffn2):
    batch, seq, d = x.shape
    seq_len = seq + BLOCK
    depth = w_in.shape[0]
    assert d == D_MODEL and (seq_len - BLOCK) % FOX_T == 0 and seq_len % ROW_TM == 0
    assert (batch * seq_len) % FFN_TM == 0

    pad = jnp.zeros((batch, N_PAD, d), x.dtype)
    meta = jnp.broadcast_to(meta_tokens.astype(x.dtype)[None], (batch, N_META, d))
    h = jnp.concatenate([pad, meta, x], axis=1).reshape(batch * seq_len, d)

    row2 = lambda a: a.reshape(1, -1).astype(F32)
    w_perm = _permute_w_in(w_in.astype(BF16))
    w1_gu, w1_down = w_ffn1_gu.astype(BF16), w_ffn1_down.astype(BF16)
    w2_gu, w2_down = w_ffn2_gu.astype(BF16), w_ffn2_down.astype(BF16)
    wfo, wgo, wout = w_fox_o.astype(BF16), w_gla_o.astype(BF16), w_out.astype(BF16)
    for l in range(depth):
        wa_pad = jnp.zeros((LANES, GLA_KWIDTH), F32).at[
            FOX_HEADS:FOX_HEADS + GLA_GATE_RANK].set(w_alpha_up[l]).astype(BF16)
        bf_pad = jnp.zeros((1, LANES), F32).at[0, :FOX_HEADS].set(b_f[l])

        h = _ffn(h, row2(g_pre_ffn1[l]), w1_gu, w1_down, row2(g_post_ffn1[l]), l)
        fq, fk, fv, gq, gk, gv, gr, gma, gmb, la, lf = _proj(
            h, row2(g_pre_mix[l]), w_perm, wa_pad, row2(b_alpha[l]), bf_pad, seq_len, l)
        o_gla, cum_col, cum_row = _gla(gq, gk, gv, la, lf, batch, seq_len)
        o_fox = _fox(fq, fk, fv, cum_col, cum_row, batch, seq_len)
        h = _post(o_fox, o_gla.reshape(batch * seq_len, GLA_WIDTH), gr, gma, gmb, h,
                  row2(g_gla_out[l]), wfo, wgo, wout, row2(g_post_mix[l]), l)
        h = _ffn(h, row2(g_pre_ffn2[l]), w2_gu, w2_down, row2(g_post_ffn2[l]), l)
    return h.reshape(batch, seq_len, d)[:, BLOCK:]
```

```python
import functools

import jax
import jax.numpy as jnp
import numpy as np
from jax import lax
from jax.experimental import pallas as pl
from jax.experimental.pallas import tpu as pltpu

D_MODEL = 1024
N_META = 16
BLOCK = 128
N_PAD = BLOCK - N_META
FOX_HEADS = 8
FOX_HEAD_DIM = 64
FOX_WIDTH = FOX_HEADS * FOX_HEAD_DIM
GLA_HEADS = 4
GLA_WIDTH = 512
GLA_DV = 128
GLA_DK = 64
GLA_KWIDTH = GLA_HEADS * GLA_DK
GLA_GATE_RANK = 16
GLA_GATE_TEMP = 16.0
D_FF = 2816
RMS_EPS = 1e-6

LANES = 128
SUB = 16
N_SUB = BLOCK // SUB
EXP_CAP = 80.0
NEG = -1e30

FFN_TM = 640
FFN_TM_MAIN = 512
FFN_TF = 256
ROW_TM = 640
FOX_T = 512
LOG2E = 1.4426950408889634
PRUNE_BITS = 160.0
NORM_UP = 1.01
FAST_MAX_BITS = 50.0
VMEM_LIMIT = 56 * 1024 * 1024

BF16 = jnp.bfloat16
F32 = jnp.float32

C_FQ, C_FK, C_FV = 0, 512, 1024
C_GQ, C_GK, C_GV = 1536, 1792, 2048
C_GR, C_GMA, C_GMB = 2560, 3072, 4096
C_SMALL = 5120
N_IN_PERM = 5248


def _dot(a, b):
    return jnp.dot(a, b, preferred_element_type=F32)


def _dot_nt(a, b):
    return lax.dot_general(a, b, (((1,), (1,)), ((), ())), preferred_element_type=F32)


def _dot_tn(a, b):
    return lax.dot_general(a, b, (((0,), (0,)), ((), ())), preferred_element_type=F32)


def _rms(x, g):
    return x * lax.rsqrt(jnp.mean(x * x, axis=-1, keepdims=True) + RMS_EPS) * g


def _log_sigmoid(x):
    return jnp.minimum(x, 0.0) - jnp.log1p(jnp.exp(-jnp.abs(x)))


def _split3(x):
    hi = x.astype(BF16)
    r1 = x - hi.astype(F32)
    mid = r1.astype(BF16)
    lo = (r1 - mid.astype(F32)).astype(BF16)
    return hi, mid, lo


def _dot_exact_lhs(m, parts):
    return _dot(m, parts[0]) + _dot(m, parts[1]) + _dot(m, parts[2])


def _ffn_kernel(h_ref, gpre_ref, wgu_ref, wd_ref, gpost_ref, o_ref, xn_ref, a_ref):
    xn_ref[...] = _rms(h_ref[...], gpre_ref[...]).astype(BF16)
    for c in range(D_FF // FFN_TF):
        lo = c * FFN_TF
        g = _dot(xn_ref[...], wgu_ref[:, lo:lo + FFN_TF])
        u = _dot(xn_ref[...], wgu_ref[:, D_FF + lo:D_FF + lo + FFN_TF])
        a_ref[:, lo:lo + FFN_TF] = (g * jax.nn.sigmoid(g) * u).astype(BF16)
    y = _dot(a_ref[...], wd_ref[...])
    o_ref[...] = h_ref[...] + 0.5 * _rms(y, gpost_ref[...])


def _resident(shape, layer):
    return pl.BlockSpec((None,) + shape, lambda *_: (layer,) + (0,) * len(shape),
                        pipeline_mode=pl.Buffered(1))


def _ffn(h, gpre, w_gu, w_down, gpost, layer, main_rows=None):
    batch, seq_len, _ = h.shape
    if main_rows is None:
        tm, rows = FFN_TM, seq_len
    else:
        tm, rows = FFN_TM_MAIN, main_rows
    tile = pl.BlockSpec((None, tm, D_MODEL), lambda b, i: (b, i, 0))
    vec = pl.BlockSpec((1, D_MODEL), lambda b, i: (0, 0))
    return pl.pallas_call(
        _ffn_kernel,
        grid=(batch, rows // tm),
        in_specs=[tile, vec, _resident((D_MODEL, 2 * D_FF), layer), _resident((D_FF, D_MODEL), layer), vec],
        out_specs=tile,
        out_shape=jax.ShapeDtypeStruct((batch, rows, D_MODEL), F32),
        scratch_shapes=[pltpu.VMEM((tm, D_MODEL), BF16), pltpu.VMEM((tm, D_FF), BF16)],
        compiler_params=pltpu.CompilerParams(
            dimension_semantics=("parallel", "parallel"), vmem_limit_bytes=VMEM_LIMIT),
    )(h, gpre, w_gu, w_down, gpost)


def _proj_kernel(h_ref, g_ref, w_ref, wa_ref, ba_ref, bf_ref,
                 fq_ref, fk_ref, fv_ref, gq_ref, gk_ref, gv_ref, gr_ref, gma_ref, gmb_ref,
                 la_ref, lf_ref, *, tiles_per_seq, main_rows):
    tm = h_ref.shape[0]
    xn = _rms(h_ref[...], g_ref[...]).astype(BF16)
    pos = (pl.program_id(0) % tiles_per_seq) * tm + lax.broadcasted_iota(jnp.int32, (tm, 1), 0)
    valid = jnp.logical_or(pos < main_rows, pos >= main_rows + N_PAD).astype(F32)

    def proj(c0, width):
        return _dot(xn, w_ref[:, c0:c0 + width])

    fq_ref[...] = (proj(C_FQ, FOX_WIDTH) * (FOX_HEAD_DIM ** -0.5 * LOG2E)).astype(BF16)
    fk_ref[...] = proj(C_FK, FOX_WIDTH).astype(BF16)
    fv_ref[...] = proj(C_FV, FOX_WIDTH).astype(BF16)
    gq_ref[...] = proj(C_GQ, GLA_KWIDTH) * (GLA_DK ** -0.5)
    gk_ref[...] = proj(C_GK, GLA_KWIDTH) * valid
    gv_ref[...] = proj(C_GV, GLA_WIDTH).astype(BF16)
    gr_ref[...] = proj(C_GR, GLA_WIDTH).astype(BF16)
    gma_ref[...] = proj(C_GMA, D_MODEL).astype(BF16)
    gmb_ref[...] = proj(C_GMB, D_MODEL).astype(BF16)
    small = proj(C_SMALL, LANES)
    lane = lax.broadcasted_iota(jnp.int32, (tm, LANES), 1)
    lf = _log_sigmoid(small + bf_ref[...]) * (valid * LOG2E)
    lf_ref[...] = jnp.where(lane < FOX_HEADS, lf, 0.0)
    xa = _dot(small.astype(BF16), wa_ref[...]) + ba_ref[...]
    la_ref[...] = _log_sigmoid(xa) * (valid * (1.0 / GLA_GATE_TEMP))


def _proj(h, g, w_perm, wa_pad, ba, bf_pad, seq_len, layer):
    n = h.shape[0]
    tm = ROW_TM
    row = lambda width: pl.BlockSpec((tm, width), lambda i: (i, 0))
    full = lambda a: (_resident(a.shape[1:], layer) if a.ndim == 3
                      else pl.BlockSpec(a.shape, lambda i: (0, 0)))
    outs = [
        (FOX_WIDTH, BF16), (FOX_WIDTH, BF16), (FOX_WIDTH, BF16),
        (GLA_KWIDTH, F32), (GLA_KWIDTH, F32), (GLA_WIDTH, BF16),
        (GLA_WIDTH, BF16), (D_MODEL, BF16), (D_MODEL, BF16),
        (GLA_KWIDTH, F32), (LANES, F32),
    ]
    return pl.pallas_call(
        functools.partial(_proj_kernel, tiles_per_seq=seq_len // tm, main_rows=seq_len - BLOCK),
        grid=(n // tm,),
        in_specs=[row(D_MODEL), full(g), full(w_perm), full(wa_pad), full(ba), full(bf_pad)],
        out_specs=[row(w) for w, _ in outs],
        out_shape=[jax.ShapeDtypeStruct((n, w), dt) for w, dt in outs],
        compiler_params=pltpu.CompilerParams(
            dimension_semantics=("parallel",), vmem_limit_bytes=VMEM_LIMIT),
    )(h, g, w_perm, wa_pad, ba, bf_pad)


def _gla_kernel(q_ref, k_ref, v_ref, la_ref, lf_ref, o_ref, cc_ref, cr_ref, st_ref, carry_ref):
    @pl.when(pl.program_id(0) == 0)
    def _():
        st_ref[...] = jnp.zeros_like(st_ref)
        carry_ref[...] = jnp.zeros_like(carry_ref)

    row = lax.broadcasted_iota(jnp.int32, (BLOCK, BLOCK), 0)
    col = lax.broadcasted_iota(jnp.int32, (BLOCK, BLOCK), 1)
    causal = row >= col
    tril = causal.astype(BF16)
    tril_sub = (causal & ((row // SUB) == (col // SUB))).astype(BF16)
    row_w = lax.broadcasted_iota(jnp.int32, (BLOCK, GLA_KWIDTH), 0)
    row_blk = row // SUB

    batches = range(q_ref.shape[0])
    problems = [(b, h) for b in batches for h in range(GLA_HEADS)]

    cb_all, cw_all = [], []
    for b in batches:
        cl = _dot_exact_lhs(tril, _split3(lf_ref[b])) + carry_ref[b]
        carry_ref[b] = cl[BLOCK - 1:BLOCK, :]
        cc_ref[b] = cl
        cr_ref[b] = cl.T[0:FOX_HEADS, :]
        la_parts = _split3(la_ref[b])
        cb_all.append(_dot_exact_lhs(tril, la_parts))
        cw_all.append(_dot_exact_lhs(tril_sub, la_parts))

    ops = {}
    for b in batches:
        cb, cw = cb_all[b], cw_all[b]
        last = cb[BLOCK - 1:BLOCK, :]
        q = q_ref[b]
        k = k_ref[b]
        q_glob = (q * jnp.exp(cb)).astype(BF16)
        q_loc = (q * jnp.exp(cw)).astype(BF16)
        k_end = (k * jnp.exp(last - cb)).astype(BF16)
        decay = jnp.exp(last)
        k_sub = []
        for i in range(N_SUB):
            ref = cb[i * SUB - 1:i * SUB, :] if i > 0 else jnp.zeros_like(last)
            e = jnp.exp(jnp.minimum(ref - cb, EXP_CAP))
            k_sub.append(jnp.where(row_w < (i + 1) * SUB, k * e, 0.0).astype(BF16))
        for pair in range(GLA_HEADS // 2):
            sl = slice(pair * LANES, (pair + 1) * LANES)
            k_cat = jnp.concatenate([ks[:, sl] for ks in k_sub], axis=1)
            for half in range(2):
                h = 2 * pair + half
                head_lanes = (col < GLA_DK) if half == 0 else (col >= GLA_DK)
                q_h = jnp.where(head_lanes, q_loc[:, sl], 0.0)
                q_cat = jnp.concatenate(
                    [jnp.where(row_blk == i, q_h, 0.0) for i in range(N_SUB)], axis=1)
                ops[b, h] = (q_cat, k_cat, jnp.where(head_lanes, q_glob[:, sl], 0.0), k_end[:, sl],
                             decay[:, sl], v_ref[b, :, h * GLA_DV:(h + 1) * GLA_DV])

    att, inter, upd = {}, {}, {}
    for p in problems:
        q_cat, k_cat, q_g, k_e, _, v_h = ops[p]
        att[p] = _dot_nt(q_cat, k_cat)
        inter[p] = _dot_nt(q_g, st_ref[p].astype(BF16))
        upd[p] = _dot_tn(v_h, k_e)

    for p in problems:
        b, h = p
        a = jnp.where(causal, att[p], 0.0).astype(BF16)
        o_ref[b, :, h * GLA_DV:(h + 1) * GLA_DV] = inter[p] + _dot(a, ops[p][5])
        st_ref[p] = st_ref[p] * ops[p][4] + upd[p]


def _gla(gq, gk, gv, la, lf, batch, seq_len):
    nc = seq_len // BLOCK
    chunk = lambda c: (c + nc - 1) % nc
    blk = lambda width: pl.BlockSpec((batch, BLOCK, width), lambda c: (0, chunk(c), 0))
    r3 = lambda a: a.reshape(batch, seq_len, a.shape[-1])
    return pl.pallas_call(
        _gla_kernel,
        grid=(nc,),
        in_specs=[blk(GLA_KWIDTH), blk(GLA_KWIDTH), blk(GLA_WIDTH), blk(GLA_KWIDTH), blk(LANES)],
        out_specs=[blk(GLA_WIDTH), blk(LANES),
                   pl.BlockSpec((batch, FOX_HEADS, BLOCK), lambda c: (0, 0, chunk(c)))],
        out_shape=[jax.ShapeDtypeStruct((batch, seq_len, GLA_WIDTH), F32),
                   jax.ShapeDtypeStruct((batch, seq_len, LANES), F32),
                   jax.ShapeDtypeStruct((batch, FOX_HEADS, seq_len), F32)],
        scratch_shapes=[pltpu.VMEM((batch, GLA_HEADS, GLA_DV, LANES), F32),
                        pltpu.VMEM((batch, 1, LANES), F32)],
        compiler_params=pltpu.CompilerParams(
            dimension_semantics=("arbitrary",), vmem_limit_bytes=VMEM_LIMIT),
    )(r3(gq), r3(gk), r3(gv), r3(la), r3(lf))


def _fox_kernel(q_ref, k_ref, v_ref, cc_ref, crp_ref, crm_ref, cb_ref, cs_ref, o_ref,
                va_ref, vb_ref, ra_ref, rb_ref, ub_ref, plan_ref, m_ref, l_ref, acc_ref):
    t = FOX_T
    nt = crm_ref.shape[2]
    s_main = nt * t
    pre = slice(s_main, s_main + BLOCK)
    pair = pl.program_id(1)

    def stack_heads(x, lane_first):
        return jnp.concatenate([jnp.where(lane_first, x, 0.0), jnp.where(lane_first, 0.0, x)], axis=0)

    def head_cols(cc, lane):
        return jnp.concatenate(
            [jnp.sum(jnp.where(lane == 2 * pair + hh, cc, 0.0), axis=1, keepdims=True)
             for hh in range(2)], axis=0)

    sel = lax.broadcasted_iota(jnp.int32, (8, LANES), 0) == lax.broadcasted_iota(
        jnp.int32, (8, LANES), 1) // FOX_HEAD_DIM
    sel = sel.astype(BF16)

    def max_sq_norm(x):
        return jnp.max(_dot_nt(sel, x * x), axis=1, keepdims=True)

    lane_b = lax.broadcasted_iota(jnp.int32, (BLOCK, LANES), 1)
    first_b = lane_b < FOX_HEAD_DIM
    kp = k_ref[0, pre, :]
    vp = v_ref[0, pre, :]
    ckp = crp_ref[0, 0]
    row_b = lax.broadcasted_iota(jnp.int32, (BLOCK, BLOCK), 0)
    ok_b = (lane_b <= row_b) & ((lane_b >= N_PAD) | (lane_b == row_b))
    ok_b = jnp.concatenate([ok_b, ok_b], axis=0)
    s = _dot_nt(stack_heads(q_ref[0, pre, :], first_b), kp)
    z = jnp.concatenate([s[:BLOCK] - ckp[0:1], s[BLOCK:] - ckp[1:2]], axis=0)
    z = jnp.where(ok_b, z, NEG)
    cq2 = head_cols(cc_ref[0, pre, :], lane_b)
    m = jnp.max(z, axis=1, keepdims=True) + cq2
    p = jnp.exp2(z - (m - cq2))
    out = _dot(p.astype(BF16), vp) / jnp.sum(p, axis=1, keepdims=True)
    o_ref[0, pre, :] = jnp.where(first_b, out[:BLOCK], out[BLOCK:]).astype(o_ref.dtype)

    lane_s = lax.broadcasted_iota(jnp.int32, v_ref.shape[1:], 1)
    v_all = v_ref[0]
    va_ref[...] = jnp.where(lane_s < FOX_HEAD_DIM, v_all, jnp.where(lane_s == FOX_HEAD_DIM, 1.0, 0.0).astype(BF16))
    vb_ref[...] = jnp.where(lane_s >= FOX_HEAD_DIM, v_all, jnp.where(lane_s == 0, 1.0, 0.0).astype(BF16))

    k_max = jnp.sqrt(max_sq_norm(k_ref[0]))
    lane = lax.broadcasted_iota(jnp.int32, (t, LANES), 1)
    first = lane < FOX_HEAD_DIM
    row_l = lax.broadcasted_iota(jnp.int32, (LANES, LANES), 0)
    row_w = lax.broadcasted_iota(jnp.int32, (LANES, 2 * LANES), 0)
    second = lax.broadcasted_iota(jnp.int32, (LANES, 2 * LANES), 1) >= LANES
    ind_ab = ((row_w >= FOX_HEAD_DIM) == second).astype(BF16)
    hot_ab = (row_w == 2 * pair + second.astype(jnp.int32)).astype(BF16)
    hot_la = (row_l == FOX_HEAD_DIM).astype(BF16)
    hot_lb = (row_l == 0).astype(BF16)
    reps = t // LANES

    def prepare(i, carry):
        rows = pl.ds(pl.multiple_of(i * t, t), t)
        q = q_ref[0, rows, :]
        nq = _dot(q * q, ind_ab)
        cq = _dot(cc_ref[0, rows, :].astype(BF16), hot_ab)
        ub_a = jnp.sqrt(nq[:, :LANES]) * (k_max[0:1] * NORM_UP)
        ub_b = jnp.sqrt(nq[:, LANES:]) * (k_max[1:2] * NORM_UP)
        ra_ref[rows, :] = ub_a - cq[:, :LANES]
        rb_ref[rows, :] = ub_b - cq[:, LANES:]
        ub_ref[pl.ds(i, 1), :] = jnp.max(jnp.maximum(ub_a, ub_b), axis=0, keepdims=True)
        return carry

    lax.fori_loop(0, nt, prepare, 0)

    ub_t = ub_ref[0:nt, :]
    fast_v = ub_t <= FAST_MAX_BITS
    bound = jnp.where(fast_v, PRUNE_BITS, PRUNE_BITS + 2.0 * ub_t)
    cb = cb_ref[0, 0]
    dead = ((cs_ref[0, 0, 0, 0:nt, :] - cb[0:1]) < -bound) & ((cs_ref[0, 0, 1, 0:nt, :] - cb[1:2]) < -bound)
    tile_q = lax.broadcasted_iota(jnp.int32, (nt, LANES), 0)
    tile_k = lax.broadcasted_iota(jnp.int32, (nt, LANES), 1)
    n_skip_v = jnp.sum(jnp.where(dead & (tile_k < tile_q), 1, 0), axis=1, keepdims=True)
    skip_pre_v = jnp.sum(jnp.where(dead & (tile_k == nt), 1, 0), axis=1, keepdims=True)
    fast_i = jnp.where(fast_v[:, 0:1], 1, 0)
    tile_1 = tile_q[:, 0:1]
    for i in range(nt):
        for row, vec in enumerate((fast_i, n_skip_v, skip_pre_v)):
            plan_ref[row, i] = jnp.sum(jnp.where(tile_1 == i, vec, 0))

    def main_tile(i, carry):
        r0 = pl.multiple_of(i * t, t)
        q2 = stack_heads(q_ref[0, pl.ds(r0, t), :], first)
        fast = plan_ref[0, i] == 1
        n_skip = plan_ref[1, i]
        skip_prefix = plan_ref[2, i]

        def logits(kt, ck, offs):
            s = _dot_nt(q2, kt)
            return jnp.concatenate([s[:t] - ck[0:1] - offs[:t], s[t:] - ck[1:2] - offs[t:]], axis=0)

        def run(update, offs_p, offs_m):
            acc_ref[...] = jnp.zeros_like(acc_ref)

            @pl.when(skip_prefix == 0)
            def _():
                z = logits(kp, ckp, offs_p)
                lane_k = lax.broadcasted_iota(jnp.int32, (2 * t, BLOCK), 1)
                update(jnp.where(lane_k >= N_PAD, z, NEG), pre)

            def step(j, diagonal):
                off = pl.multiple_of(j * t, t)
                z = logits(k_ref[0, pl.ds(off, t), :], crm_ref[0, 0, j], offs_m)
                if diagonal:
                    ok = lax.broadcasted_iota(jnp.int32, (t, t), 1) <= lax.broadcasted_iota(jnp.int32, (t, t), 0)
                    z = jnp.where(jnp.concatenate([ok, ok], axis=0), z, NEG)
                update(z, pl.ds(off, t))

            def body(j, c):
                step(j, False)
                return c

            lax.fori_loop(n_skip, i, body, 0)
            step(i, True)

        def finish(l_a, l_b):
            acc = acc_ref[...]
            out = jnp.where(first, acc[:t] / l_a, acc[t:] / l_b)
            o_ref[0, pl.ds(r0, t), :] = out.astype(o_ref.dtype)

        @pl.when(fast)
        def _():
            ref = jnp.concatenate([ra_ref[pl.ds(r0, t), :], rb_ref[pl.ds(r0, t), :]], axis=0)

            def update(z, rows):
                p = jnp.exp2(z).astype(BF16)
                acc_ref[0:t] += _dot(p[:t], va_ref[rows, :])
                acc_ref[t:] += _dot(p[t:], vb_ref[rows, :])

            run(update, ref, jnp.concatenate([ref] * reps, axis=1))

            def denominators(acc, hot):
                hi = acc.astype(BF16)
                lo = (acc - hi.astype(F32)).astype(BF16)
                return _dot(hi, hot) + _dot(lo, hot)

            finish(denominators(acc_ref[0:t], hot_la), denominators(acc_ref[t:], hot_lb))

        @pl.when(jnp.logical_not(fast))
        def _():
            cq2 = head_cols(cc_ref[0, pl.ds(r0, t), :], lane)
            m_ref[...] = jnp.full_like(m_ref, NEG)
            l_ref[...] = jnp.zeros_like(l_ref)

            def update(z, rows):
                m_old = m_ref[...]
                m_new = jnp.maximum(m_old, jnp.max(z, axis=1, keepdims=True) + cq2)
                p = jnp.exp2(z - (m_new - cq2))
                alpha = jnp.exp2(m_old - m_new)
                l_ref[...] = alpha * l_ref[...] + jnp.sum(p, axis=1, keepdims=True)
                acc_ref[...] = alpha * acc_ref[...] + _dot(p.astype(BF16), v_ref[0, rows, :])
                m_ref[...] = m_new

            run(update, jnp.zeros((2 * t, 1), F32), jnp.zeros((2 * t, 1), F32))
            finish(l_ref[0:t], l_ref[t:])

        return carry

    lax.fori_loop(0, nt, main_tile, 0)


def _fox(fq, fk, fv, cum_col, cum_row, batch, seq_len):
    t = FOX_T
    nt = (seq_len - BLOCK) // t
    npairs = FOX_HEADS // 2
    assert nt + 1 <= LANES
    s_main = nt * t
    r3 = lambda a: a.reshape(batch, seq_len, a.shape[-1])
    cr = cum_row.reshape(batch, npairs, 2, seq_len)
    cr_prefix = cr[..., s_main:]
    cr_main = cr[..., :s_main].reshape(batch, npairs, 2, nt, t).transpose(0, 1, 3, 2, 4)
    bounds = jnp.concatenate(
        [cr[..., t - 1:s_main:t], cr[..., seq_len - 1:],
         jnp.zeros((batch, npairs, 2, LANES - nt - 1), F32)], axis=-1)
    nt8 = -(-nt // 8) * 8
    starts = jnp.pad(cr[..., 0:s_main:t], ((0, 0), (0, 0), (0, 0), (0, nt8 - nt)))
    starts = jnp.broadcast_to(starts[..., None], (batch, npairs, 2, nt8, LANES))
    whole = pl.BlockSpec((1, seq_len, LANES), lambda b, p: (b, 0, p))
    out = pl.pallas_call(
        _fox_kernel,
        grid=(batch, npairs),
        in_specs=[whole, whole, whole,
                  pl.BlockSpec((1, seq_len, LANES), lambda b, p: (b, 0, 0)),
                  pl.BlockSpec((1, 1, 2, BLOCK), lambda b, p: (b, p, 0, 0)),
                  pl.BlockSpec((1, 1, nt, 2, t), lambda b, p: (b, p, 0, 0, 0)),
                  pl.BlockSpec((1, 1, 2, LANES), lambda b, p: (b, p, 0, 0)),
                  pl.BlockSpec((1, 1, 2, nt8, LANES), lambda b, p: (b, p, 0, 0, 0))],
        out_specs=whole,
        out_shape=jax.ShapeDtypeStruct((batch, seq_len, FOX_WIDTH), BF16),
        scratch_shapes=[pltpu.VMEM((seq_len, LANES), BF16), pltpu.VMEM((seq_len, LANES), BF16),
                        pltpu.VMEM((s_main, LANES), F32), pltpu.VMEM((s_main, LANES), F32),
                        pltpu.VMEM((nt8, LANES), F32),
                        pltpu.SMEM((3, nt), jnp.int32),
                        pltpu.VMEM((2 * t, 1), F32), pltpu.VMEM((2 * t, 1), F32),
                        pltpu.VMEM((2 * t, LANES), F32)],
        compiler_params=pltpu.CompilerParams(
            dimension_semantics=("parallel", "parallel"), vmem_limit_bytes=VMEM_LIMIT),
    )(r3(fq), r3(fk), r3(fv), cum_col, cr_prefix, cr_main, bounds, starts)
    return out.reshape(batch * seq_len, FOX_WIDTH)


def _post_kernel(of_ref, og_ref, gr_ref, gma_ref, gmb_ref, h_ref, ggla_ref, wfo_ref, wgo_ref,
                 wout_ref, gpost_ref, o_ref):
    og = og_ref[...]
    ggla = ggla_ref[...]
    heads = []
    for hh in range(GLA_HEADS):
        sl = slice(hh * GLA_DV, (hh + 1) * GLA_DV)
        heads.append(_rms(og[:, sl], ggla[:, sl]))
    gr = gr_ref[...].astype(F32)
    o_gla = (jnp.concatenate(heads, axis=1) * (gr * jax.nn.sigmoid(gr))).astype(BF16)
    a = _dot(of_ref[...], wfo_ref[...])
    b = _dot(o_gla, wgo_ref[...])
    y = jax.nn.sigmoid(gma_ref[...].astype(F32)) * a + jax.nn.sigmoid(gmb_ref[...].astype(F32)) * b
    mix = _dot(y.astype(BF16), wout_ref[...])
    o_ref[...] = h_ref[...] + _rms(mix, gpost_ref[...])


def _post(o_fox, o_gla, gr, gma, gmb, h, ggla, wfo, wgo, wout, gpost, layer):
    n = h.shape[0]
    tm = ROW_TM
    row = lambda a: pl.BlockSpec((tm, a.shape[1]), lambda i: (i, 0))
    full = lambda a: (_resident(a.shape[1:], layer) if a.ndim == 3
                      else pl.BlockSpec(a.shape, lambda i: (0, 0)))
    args = (o_fox, o_gla, gr, gma, gmb, h, ggla, wfo, wgo, wout, gpost)
    return pl.pallas_call(
        _post_kernel,
        grid=(n // tm,),
        in_specs=[row(a) for a in args[:6]] + [full(a) for a in args[6:]],
        out_specs=pl.BlockSpec((tm, D_MODEL), lambda i: (i, 0)),
        out_shape=jax.ShapeDtypeStruct((n, D_MODEL), F32),
        compiler_params=pltpu.CompilerParams(
            dimension_semantics=("parallel",), vmem_limit_bytes=VMEM_LIMIT),
    )(*args)


def _permute_w_in(w):
    pad = jnp.zeros(w.shape[:-1] + (LANES - FOX_HEADS - GLA_GATE_RANK,), w.dtype)
    return jnp.concatenate(
        [w[..., 0:1536], w[..., 1544:2568], w[..., 2584:5144], w[..., 1536:1544], w[..., 2568:2584], pad],
        axis=-1)


def kernel(x, meta_tokens, w_in, w_alpha_up, b_alpha, b_f, g_gla_out, w_fox_o, w_gla_o, w_out,
           w_ffn1_gu, w_ffn1_down, w_ffn2_gu, w_ffn2_down,
           g_pre_ffn1, g_post_ffn1, g_pre_mix, g_post_mix, g_pre_ffn2, g_post_ffn2):
    batch, seq, d = x.shape
    seq_len = seq + BLOCK
    depth = w_in.shape[0]
    assert d == D_MODEL and (seq_len - BLOCK) % FOX_T == 0 and seq_len % ROW_TM == 0
    assert seq_len % FFN_TM == 0 and seq % FFN_TM_MAIN == 0

    pad = jnp.zeros((batch, N_PAD, d), x.dtype)
    meta = jnp.broadcast_to(meta_tokens.astype(x.dtype)[None], (batch, N_META, d))
    h = jnp.concatenate([x, pad, meta], axis=1)
    flat = lambda a: a.reshape(batch * seq_len, a.shape[-1])

    row2 = lambda a: a.reshape(1, -1).astype(F32)
    w_perm = _permute_w_in(w_in.astype(BF16))
    w1_gu, w1_down = w_ffn1_gu.astype(BF16), w_ffn1_down.astype(BF16)
    w2_gu, w2_down = w_ffn2_gu.astype(BF16), w_ffn2_down.astype(BF16)
    wfo, wgo, wout = w_fox_o.astype(BF16), w_gla_o.astype(BF16), w_out.astype(BF16)
    for l in range(depth):
        wa_pad = jnp.zeros((LANES, GLA_KWIDTH), F32).at[
            FOX_HEADS:FOX_HEADS + GLA_GATE_RANK].set(w_alpha_up[l]).astype(BF16)
        bf_pad = jnp.zeros((1, LANES), F32).at[0, :FOX_HEADS].set(b_f[l])

        h = _ffn(h, row2(g_pre_ffn1[l]), w1_gu, w1_down, row2(g_post_ffn1[l]), l)
        fq, fk, fv, gq, gk, gv, gr, gma, gmb, la, lf = _proj(
            flat(h), row2(g_pre_mix[l]), w_perm, wa_pad, row2(b_alpha[l]), bf_pad, seq_len, l)
        o_gla, cum_col, cum_row = _gla(gq, gk, gv, la, lf, batch, seq_len)
        o_fox = _fox(fq, fk, fv, cum_col, cum_row, batch, seq_len)
        h = _post(o_fox, flat(o_gla), gr, gma, gmb, flat(h),
                  row2(g_gla_out[l]), wfo, wgo, wout, row2(g_post_mix[l]), l).reshape(batch, seq_len, d)
        h = _ffn(h, row2(g_pre_ffn2[l]), w2_gu, w2_down, row2(g_post_ffn2[l]), l,
                 main_rows=seq if l == depth - 1 else None)
    return h
```

```python
import functools

import jax
import jax.numpy as jnp
import numpy as np
from jax import lax
from jax.experimental import pallas as pl
from jax.experimental.pallas import tpu as pltpu

D_MODEL = 1024
N_META = 16
BLOCK = 128
N_PAD = BLOCK - N_META
FOX_HEADS = 8
FOX_HEAD_DIM = 64
FOX_WIDTH = FOX_HEADS * FOX_HEAD_DIM
GLA_HEADS = 4
GLA_WIDTH = 512
GLA_DV = 128
GLA_DK = 64
GLA_KWIDTH = GLA_HEADS * GLA_DK
GLA_GATE_RANK = 16
GLA_GATE_TEMP = 16.0
D_FF = 2816
RMS_EPS = 1e-6

LANES = 128
SUB = 16
N_SUB = BLOCK // SUB
EXP_CAP = 80.0
NEG = -1e30

FFN_TM = 640
FFN_TM_MAIN = 512
FFN_TF = 256
ROW_TM = 640
FOX_T = 512
LOG2E = 1.4426950408889634
PRUNE_BITS = 160.0
NORM_UP = 1.01
FAST_MAX_BITS = 50.0
VMEM_LIMIT = 56 * 1024 * 1024

BF16 = jnp.bfloat16
F32 = jnp.float32

C_FQ, C_FK, C_FV = 0, 512, 1024
C_GQ, C_GK, C_GV = 1536, 1792, 2048
C_GR, C_GMA, C_GMB = 2560, 3072, 4096
C_SMALL = 5120
N_IN = 5144
N_IN_PERM = 5248
W_IN_MOVES = ((0, 0, 1536), (1536, 1544, 1024), (2560, 2584, 2560), (5120, 1536, 8), (5128, 2568, 16))


def _dot(a, b):
    return jnp.dot(a, b, preferred_element_type=F32)


def _dot_nt(a, b):
    return lax.dot_general(a, b, (((1,), (1,)), ((), ())), preferred_element_type=F32)


def _dot_tn(a, b):
    return lax.dot_general(a, b, (((0,), (0,)), ((), ())), preferred_element_type=F32)


def _rms(x, g):
    return x * lax.rsqrt(jnp.mean(x * x, axis=-1, keepdims=True) + RMS_EPS) * g


def _log_sigmoid(x):
    return jnp.minimum(x, 0.0) - jnp.log1p(jnp.exp(-jnp.abs(x)))


def _split3(x):
    hi = x.astype(BF16)
    r1 = x - hi.astype(F32)
    mid = r1.astype(BF16)
    lo = (r1 - mid.astype(F32)).astype(BF16)
    return hi, mid, lo


def _dot_exact_lhs(m, parts):
    return _dot(m, parts[0]) + _dot(m, parts[1]) + _dot(m, parts[2])


def _ffn_rows(h, gpre_ref, wgu_ref, wd_ref, gpost_ref, xn_ref, a_ref):
    m = h.shape[0]
    xn_ref[0:m] = _rms(h, gpre_ref[...]).astype(BF16)
    for c in range(D_FF // FFN_TF):
        lo = c * FFN_TF
        g = _dot(xn_ref[0:m], wgu_ref[:, lo:lo + FFN_TF])
        u = _dot(xn_ref[0:m], wgu_ref[:, D_FF + lo:D_FF + lo + FFN_TF])
        a_ref[0:m, lo:lo + FFN_TF] = (g * jax.nn.sigmoid(g) * u).astype(BF16)
    y = _dot(a_ref[0:m], wd_ref[...])
    return h + 0.5 * _rms(y, gpost_ref[...])


def _ffn_kernel(h_ref, gpre_ref, wgu_ref, wd_ref, gpost_ref, o_ref, xn_ref, a_ref):
    o_ref[...] = _ffn_rows(h_ref[...], gpre_ref, wgu_ref, wd_ref, gpost_ref, xn_ref, a_ref)


def _ffn_first_kernel(x_ref, p_ref, gpre_ref, wgu_ref, wd_ref, gpost_ref, o_ref, xn_ref, a_ref):
    last = pl.program_id(1) == pl.num_programs(1) - 1

    @pl.when(jnp.logical_not(last))
    def _():
        o_ref[...] = _ffn_rows(x_ref[...], gpre_ref, wgu_ref, wd_ref, gpost_ref, xn_ref, a_ref)

    @pl.when(last)
    def _():
        o_ref[0:BLOCK] = _ffn_rows(p_ref[...], gpre_ref, wgu_ref, wd_ref, gpost_ref, xn_ref, a_ref)


def _resident(shape, layer):
    return pl.BlockSpec((None,) + shape, lambda *_: (layer,) + (0,) * len(shape),
                        pipeline_mode=pl.Buffered(1))


def _ffn(h, gpre, w_gu, w_down, gpost, layer, main_rows=None):
    batch, seq_len, _ = h.shape
    if main_rows is None:
        tm, rows = FFN_TM, seq_len
    else:
        tm, rows = FFN_TM_MAIN, main_rows
    tile = pl.BlockSpec((None, tm, D_MODEL), lambda b, i: (b, i, 0))
    vec = pl.BlockSpec((1, D_MODEL), lambda b, i: (0, 0))
    return pl.pallas_call(
        _ffn_kernel,
        grid=(batch, rows // tm),
        in_specs=[tile, vec, _resident((D_MODEL, 2 * D_FF), layer), _resident((D_FF, D_MODEL), layer), vec],
        out_specs=tile,
        out_shape=jax.ShapeDtypeStruct((batch, rows, D_MODEL), F32),
        scratch_shapes=[pltpu.VMEM((tm, D_MODEL), BF16), pltpu.VMEM((tm, D_FF), BF16)],
        compiler_params=pltpu.CompilerParams(
            dimension_semantics=("parallel", "parallel"), vmem_limit_bytes=VMEM_LIMIT),
    )(h, gpre, w_gu, w_down, gpost)


def _ffn_first(x, prefix, gpre, w_gu, w_down, gpost, layer):
    batch, seq, _ = x.shape
    tm = FFN_TM_MAIN
    n_main = seq // tm
    vec = pl.BlockSpec((1, D_MODEL), lambda b, i: (0, 0))
    return pl.pallas_call(
        _ffn_first_kernel,
        grid=(batch, n_main + 1),
        in_specs=[pl.BlockSpec((None, tm, D_MODEL), lambda b, i: (b, jnp.minimum(i, n_main - 1), 0)),
                  pl.BlockSpec((None, BLOCK, D_MODEL), lambda b, i: (b, 0, 0)),
                  vec, _resident((D_MODEL, 2 * D_FF), layer), _resident((D_FF, D_MODEL), layer), vec],
        out_specs=pl.BlockSpec((None, tm, D_MODEL), lambda b, i: (b, i, 0)),
        out_shape=jax.ShapeDtypeStruct((batch, seq + BLOCK, D_MODEL), F32),
        scratch_shapes=[pltpu.VMEM((tm, D_MODEL), BF16), pltpu.VMEM((tm, D_FF), BF16)],
        compiler_params=pltpu.CompilerParams(
            dimension_semantics=("parallel", "arbitrary"), vmem_limit_bytes=VMEM_LIMIT),
    )(x, prefix, gpre, w_gu, w_down, gpost)


def _proj_kernel(h_ref, g_ref, w_ref, wa_ref, ba_ref, bf_ref,
                 fq_ref, fk_ref, fv_ref, gq_ref, gk_ref, gv_ref, gr_ref, gma_ref, gmb_ref,
                 la_ref, lf_ref, ws_ref, *, tiles_per_seq, main_rows):
    @pl.when(pl.program_id(0) == 0)
    def _():
        for dst, src, width in W_IN_MOVES:
            ws_ref[:, dst:dst + width] = w_ref[:, src:src + width]
        ws_ref[:, N_IN:] = jnp.zeros((D_MODEL, N_IN_PERM - N_IN), BF16)

    w_ref = ws_ref
    tm = h_ref.shape[0]
    xn = _rms(h_ref[...], g_ref[...]).astype(BF16)
    pos = (pl.program_id(0) % tiles_per_seq) * tm + lax.broadcasted_iota(jnp.int32, (tm, 1), 0)
    valid = jnp.logical_or(pos < main_rows, pos >= main_rows + N_PAD).astype(F32)

    def proj(c0, width):
        return _dot(xn, w_ref[:, c0:c0 + width])

    fq_ref[...] = (proj(C_FQ, FOX_WIDTH) * (FOX_HEAD_DIM ** -0.5 * LOG2E)).astype(BF16)
    fk_ref[...] = proj(C_FK, FOX_WIDTH).astype(BF16)
    fv_ref[...] = proj(C_FV, FOX_WIDTH).astype(BF16)
    gq_ref[...] = proj(C_GQ, GLA_KWIDTH) * (GLA_DK ** -0.5)
    gk_ref[...] = proj(C_GK, GLA_KWIDTH) * valid
    gv_ref[...] = proj(C_GV, GLA_WIDTH).astype(BF16)
    gr_ref[...] = proj(C_GR, GLA_WIDTH).astype(BF16)
    gma_ref[...] = proj(C_GMA, D_MODEL).astype(BF16)
    gmb_ref[...] = proj(C_GMB, D_MODEL).astype(BF16)
    small = proj(C_SMALL, LANES)
    lane = lax.broadcasted_iota(jnp.int32, (tm, LANES), 1)
    lf = _log_sigmoid(small + bf_ref[...]) * (valid * LOG2E)
    lf_ref[...] = jnp.where(lane < FOX_HEADS, lf, 0.0)
    xa = _dot(small.astype(BF16), wa_ref[...]) + ba_ref[...]
    la_ref[...] = _log_sigmoid(xa) * (valid * (1.0 / GLA_GATE_TEMP))


def _proj(h, g, w_perm, wa_pad, ba, bf_pad, seq_len, layer):
    n = h.shape[0]
    tm = ROW_TM
    row = lambda width: pl.BlockSpec((tm, width), lambda i: (i, 0))
    full = lambda a: (_resident(a.shape[1:], layer) if a.ndim == 3
                      else pl.BlockSpec(a.shape, lambda i: (0, 0)))
    outs = [
        (FOX_WIDTH, BF16), (FOX_WIDTH, BF16), (FOX_WIDTH, BF16),
        (GLA_KWIDTH, F32), (GLA_KWIDTH, F32), (GLA_WIDTH, BF16),
        (GLA_WIDTH, BF16), (D_MODEL, BF16), (D_MODEL, BF16),
        (GLA_KWIDTH, F32), (LANES, F32),
    ]
    return pl.pallas_call(
        functools.partial(_proj_kernel, tiles_per_seq=seq_len // tm, main_rows=seq_len - BLOCK),
        grid=(n // tm,),
        in_specs=[row(D_MODEL), full(g), full(w_perm), full(wa_pad), full(ba), full(bf_pad)],
        out_specs=[row(w) for w, _ in outs],
        out_shape=[jax.ShapeDtypeStruct((n, w), dt) for w, dt in outs],
        scratch_shapes=[pltpu.VMEM((D_MODEL, N_IN_PERM), BF16)],
        compiler_params=pltpu.CompilerParams(
            dimension_semantics=("arbitrary",), vmem_limit_bytes=VMEM_LIMIT),
    )(h, g, w_perm, wa_pad, ba, bf_pad)


def _gla_kernel(q_ref, k_ref, v_ref, la_ref, lf_ref, o_ref, cc_ref, cr_ref, st_ref, carry_ref):
    @pl.when(pl.program_id(0) == 0)
    def _():
        st_ref[...] = jnp.zeros_like(st_ref)
        carry_ref[...] = jnp.zeros_like(carry_ref)

    row = lax.broadcasted_iota(jnp.int32, (BLOCK, BLOCK), 0)
    col = lax.broadcasted_iota(jnp.int32, (BLOCK, BLOCK), 1)
    causal = row >= col
    tril = causal.astype(BF16)
    tril_sub = (causal & ((row // SUB) == (col // SUB))).astype(BF16)
    row_w = lax.broadcasted_iota(jnp.int32, (BLOCK, GLA_KWIDTH), 0)
    row_blk = row // SUB

    batches = range(q_ref.shape[0])
    problems = [(b, h) for b in batches for h in range(GLA_HEADS)]

    cb_all, cw_all = [], []
    for b in batches:
        cl = _dot_exact_lhs(tril, _split3(lf_ref[b])) + carry_ref[b]
        carry_ref[b] = cl[BLOCK - 1:BLOCK, :]
        cc_ref[b] = cl
        cr_ref[b] = cl.T[0:FOX_HEADS, :]
        la_parts = _split3(la_ref[b])
        cb_all.append(_dot_exact_lhs(tril, la_parts))
        cw_all.append(_dot_exact_lhs(tril_sub, la_parts))

    ops = {}
    for b in batches:
        cb, cw = cb_all[b], cw_all[b]
        last = cb[BLOCK - 1:BLOCK, :]
        q = q_ref[b]
        k = k_ref[b]
        q_glob = (q * jnp.exp(cb)).astype(BF16)
        q_loc = (q * jnp.exp(cw)).astype(BF16)
        k_end = (k * jnp.exp(last - cb)).astype(BF16)
        decay = jnp.exp(last)
        k_sub = []
        for i in range(N_SUB):
            ref = cb[i * SUB - 1:i * SUB, :] if i > 0 else jnp.zeros_like(last)
            e = jnp.exp(jnp.minimum(ref - cb, EXP_CAP))
            k_sub.append(jnp.where(row_w < (i + 1) * SUB, k * e, 0.0).astype(BF16))
        for pair in range(GLA_HEADS // 2):
            sl = slice(pair * LANES, (pair + 1) * LANES)
            k_cat = jnp.concatenate([ks[:, sl] for ks in k_sub], axis=1)
            for half in range(2):
                h = 2 * pair + half
                head_lanes = (col < GLA_DK) if half == 0 else (col >= GLA_DK)
                q_h = jnp.where(head_lanes, q_loc[:, sl], 0.0)
                q_cat = jnp.concatenate(
                    [jnp.where(row_blk == i, q_h, 0.0) for i in range(N_SUB)], axis=1)
                ops[b, h] = (q_cat, k_cat, jnp.where(head_lanes, q_glob[:, sl], 0.0), k_end[:, sl],
                             decay[:, sl], v_ref[b, :, h * GLA_DV:(h + 1) * GLA_DV])

    att, inter, upd = {}, {}, {}
    for p in problems:
        q_cat, k_cat, q_g, k_e, _, v_h = ops[p]
        att[p] = _dot_nt(q_cat, k_cat)
        inter[p] = _dot_nt(q_g, st_ref[p].astype(BF16))
        upd[p] = _dot_tn(v_h, k_e)

    for p in problems:
        b, h = p
        a = jnp.where(causal, att[p], 0.0).astype(BF16)
        o_ref[b, :, h * GLA_DV:(h + 1) * GLA_DV] = inter[p] + _dot(a, ops[p][5])
        st_ref[p] = st_ref[p] * ops[p][4] + upd[p]


def _gla(gq, gk, gv, la, lf, batch, seq_len):
    nc = seq_len // BLOCK
    chunk = lambda c: (c + nc - 1) % nc
    blk = lambda width: pl.BlockSpec((batch, BLOCK, width), lambda c: (0, chunk(c), 0))
    r3 = lambda a: a.reshape(batch, seq_len, a.shape[-1])
    return pl.pallas_call(
        _gla_kernel,
        grid=(nc,),
        in_specs=[blk(GLA_KWIDTH), blk(GLA_KWIDTH), blk(GLA_WIDTH), blk(GLA_KWIDTH), blk(LANES)],
        out_specs=[blk(GLA_WIDTH), blk(LANES),
                   pl.BlockSpec((batch, FOX_HEADS, BLOCK), lambda c: (0, 0, chunk(c)))],
        out_shape=[jax.ShapeDtypeStruct((batch, seq_len, GLA_WIDTH), F32),
                   jax.ShapeDtypeStruct((batch, seq_len, LANES), F32),
                   jax.ShapeDtypeStruct((batch, FOX_HEADS, seq_len), F32)],
        scratch_shapes=[pltpu.VMEM((batch, GLA_HEADS, GLA_DV, LANES), F32),
                        pltpu.VMEM((batch, 1, LANES), F32)],
        compiler_params=pltpu.CompilerParams(
            dimension_semantics=("arbitrary",), vmem_limit_bytes=VMEM_LIMIT),
    )(r3(gq), r3(gk), r3(gv), r3(la), r3(lf))


def _fox_kernel(q_ref, k_ref, v_ref, cc_ref, crp_ref, crm_ref, cb_ref, cs_ref, o_ref,
                va_ref, vb_ref, ra_ref, rb_ref, ub_ref, plan_ref, m_ref, l_ref, acc_ref):
    t = FOX_T
    nt = crm_ref.shape[2]
    s_main = nt * t
    pre = slice(s_main, s_main + BLOCK)
    pair = pl.program_id(1)

    def stack_heads(x, lane_first):
        return jnp.concatenate([jnp.where(lane_first, x, 0.0), jnp.where(lane_first, 0.0, x)], axis=0)

    def head_cols(cc, lane):
        return jnp.concatenate(
            [jnp.sum(jnp.where(lane == 2 * pair + hh, cc, 0.0), axis=1, keepdims=True)
             for hh in range(2)], axis=0)

    sel = lax.broadcasted_iota(jnp.int32, (8, LANES), 0) == lax.broadcasted_iota(
        jnp.int32, (8, LANES), 1) // FOX_HEAD_DIM
    sel = sel.astype(BF16)

    def max_sq_norm(x):
        return jnp.max(_dot_nt(sel, x * x), axis=1, keepdims=True)

    lane_b = lax.broadcasted_iota(jnp.int32, (BLOCK, LANES), 1)
    first_b = lane_b < FOX_HEAD_DIM
    kp = k_ref[0, pre, :]
    vp = v_ref[0, pre, :]
    ckp = crp_ref[0, 0]
    row_b = lax.broadcasted_iota(jnp.int32, (BLOCK, BLOCK), 0)
    ok_b = (lane_b <= row_b) & ((lane_b >= N_PAD) | (lane_b == row_b))
    ok_b = jnp.concatenate([ok_b, ok_b], axis=0)
    s = _dot_nt(stack_heads(q_ref[0, pre, :], first_b), kp)
    z = jnp.concatenate([s[:BLOCK] - ckp[0:1], s[BLOCK:] - ckp[1:2]], axis=0)
    z = jnp.where(ok_b, z, NEG)
    cq2 = head_cols(cc_ref[0, pre, :], lane_b)
    m = jnp.max(z, axis=1, keepdims=True) + cq2
    p = jnp.exp2(z - (m - cq2))
    out = _dot(p.astype(BF16), vp) / jnp.sum(p, axis=1, keepdims=True)
    o_ref[0, pre, :] = jnp.where(first_b, out[:BLOCK], out[BLOCK:]).astype(o_ref.dtype)

    lane_s = lax.broadcasted_iota(jnp.int32, v_ref.shape[1:], 1)
    v_all = v_ref[0]
    va_ref[...] = jnp.where(lane_s < FOX_HEAD_DIM, v_all, jnp.where(lane_s == FOX_HEAD_DIM, 1.0, 0.0).astype(BF16))
    vb_ref[...] = jnp.where(lane_s >= FOX_HEAD_DIM, v_all, jnp.where(lane_s == 0, 1.0, 0.0).astype(BF16))

    k_max = jnp.sqrt(max_sq_norm(k_ref[0]))
    lane = lax.broadcasted_iota(jnp.int32, (t, LANES), 1)
    first = lane < FOX_HEAD_DIM
    row_l = lax.broadcasted_iota(jnp.int32, (LANES, LANES), 0)
    row_w = lax.broadcasted_iota(jnp.int32, (LANES, 2 * LANES), 0)
    second = lax.broadcasted_iota(jnp.int32, (LANES, 2 * LANES), 1) >= LANES
    ind_ab = ((row_w >= FOX_HEAD_DIM) == second).astype(BF16)
    hot_ab = (row_w == 2 * pair + second.astype(jnp.int32)).astype(BF16)
    hot_la = (row_l == FOX_HEAD_DIM).astype(BF16)
    hot_lb = (row_l == 0).astype(BF16)
    reps = t // LANES

    def prepare(i, carry):
        rows = pl.ds(pl.multiple_of(i * t, t), t)
        q = q_ref[0, rows, :]
        nq = _dot(q * q, ind_ab)
        cq = _dot(cc_ref[0, rows, :].astype(BF16), hot_ab)
        ub_a = jnp.sqrt(nq[:, :LANES]) * (k_max[0:1] * NORM_UP)
        ub_b = jnp.sqrt(nq[:, LANES:]) * (k_max[1:2] * NORM_UP)
        ra_ref[rows, :] = ub_a - cq[:, :LANES]
        rb_ref[rows, :] = ub_b - cq[:, LANES:]
        ub_ref[pl.ds(i, 1), :] = jnp.max(jnp.maximum(ub_a, ub_b), axis=0, keepdims=True)
        return carry

    lax.fori_loop(0, nt, prepare, 0)

    ub_t = ub_ref[0:nt, :]
    fast_v = ub_t <= FAST_MAX_BITS
    bound = jnp.where(fast_v, PRUNE_BITS, PRUNE_BITS + 2.0 * ub_t)
    cb = cb_ref[0, 0]
    dead = ((cs_ref[0, 0, 0, 0:nt, :] - cb[0:1]) < -bound) & ((cs_ref[0, 0, 1, 0:nt, :] - cb[1:2]) < -bound)
    tile_q = lax.broadcasted_iota(jnp.int32, (nt, LANES), 0)
    tile_k = lax.broadcasted_iota(jnp.int32, (nt, LANES), 1)
    n_skip_v = jnp.sum(jnp.where(dead & (tile_k < tile_q), 1, 0), axis=1, keepdims=True)
    skip_pre_v = jnp.sum(jnp.where(dead & (tile_k == nt), 1, 0), axis=1, keepdims=True)
    fast_i = jnp.where(fast_v[:, 0:1], 1, 0)
    tile_1 = tile_q[:, 0:1]
    for i in range(nt):
        for row, vec in enumerate((fast_i, n_skip_v, skip_pre_v)):
            plan_ref[row, i] = jnp.sum(jnp.where(tile_1 == i, vec, 0))

    def main_tile(i, carry):
        r0 = pl.multiple_of(i * t, t)
        q2 = stack_heads(q_ref[0, pl.ds(r0, t), :], first)
        fast = plan_ref[0, i] == 1
        n_skip = plan_ref[1, i]
        skip_prefix = plan_ref[2, i]

        def logits(kt, ck, offs):
            s = _dot_nt(q2, kt)
            return jnp.concatenate([s[:t] - ck[0:1] - offs[:t], s[t:] - ck[1:2] - offs[t:]], axis=0)

        def run(update, offs_p, offs_m):
            acc_ref[...] = jnp.zeros_like(acc_ref)

            @pl.when(skip_prefix == 0)
            def _():
                z = logits(kp, ckp, offs_p)
                lane_k = lax.broadcasted_iota(jnp.int32, (2 * t, BLOCK), 1)
                update(jnp.where(lane_k >= N_PAD, z, NEG), pre)

            def step(j, diagonal):
                off = pl.multiple_of(j * t, t)
                z = logits(k_ref[0, pl.ds(off, t), :], crm_ref[0, 0, j], offs_m)
                if diagonal:
                    ok = lax.broadcasted_iota(jnp.int32, (t, t), 1) <= lax.broadcasted_iota(jnp.int32, (t, t), 0)
                    z = jnp.where(jnp.concatenate([ok, ok], axis=0), z, NEG)
                update(z, pl.ds(off, t))

            def body(j, c):
                step(j, False)
                return c

            lax.fori_loop(n_skip, i, body, 0)
            step(i, True)

        def finish(l_a, l_b):
            acc = acc_ref[...]
            out = jnp.where(first, acc[:t] / l_a, acc[t:] / l_b)
            o_ref[0, pl.ds(r0, t), :] = out.astype(o_ref.dtype)

        @pl.when(fast)
        def _():
            ref = jnp.concatenate([ra_ref[pl.ds(r0, t), :], rb_ref[pl.ds(r0, t), :]], axis=0)

            def update(z, rows):
                p = jnp.exp2(z).astype(BF16)
                acc_ref[0:t] += _dot(p[:t], va_ref[rows, :])
                acc_ref[t:] += _dot(p[t:], vb_ref[rows, :])

            run(update, ref, jnp.concatenate([ref] * reps, axis=1))

            def denominators(acc, hot):
                hi = acc.astype(BF16)
                lo = (acc - hi.astype(F32)).astype(BF16)
                return _dot(hi, hot) + _dot(lo, hot)

            finish(denominators(acc_ref[0:t], hot_la), denominators(acc_ref[t:], hot_lb))

        @pl.when(jnp.logical_not(fast))
        def _():
            cq2 = head_cols(cc_ref[0, pl.ds(r0, t), :], lane)
            m_ref[...] = jnp.full_like(m_ref, NEG)
            l_ref[...] = jnp.zeros_like(l_ref)

            def update(z, rows):
                m_old = m_ref[...]
                m_new = jnp.maximum(m_old, jnp.max(z, axis=1, keepdims=True) + cq2)
                p = jnp.exp2(z - (m_new - cq2))
                alpha = jnp.exp2(m_old - m_new)
                l_ref[...] = alpha * l_ref[...] + jnp.sum(p, axis=1, keepdims=True)
                acc_ref[...] = alpha * acc_ref[...] + _dot(p.astype(BF16), v_ref[0, rows, :])
                m_ref[...] = m_new

            run(update, jnp.zeros((2 * t, 1), F32), jnp.zeros((2 * t, 1), F32))
            finish(l_ref[0:t], l_ref[t:])

        return carry

    lax.fori_loop(0, nt, main_tile, 0)


def _fox(fq, fk, fv, cum_col, cum_row, batch, seq_len):
    t = FOX_T
    nt = (seq_len - BLOCK) // t
    npairs = FOX_HEADS // 2
    assert nt + 1 <= LANES
    s_main = nt * t
    r3 = lambda a: a.reshape(batch, seq_len, a.shape[-1])
    cr = cum_row.reshape(batch, npairs, 2, seq_len)
    cr_prefix = cr[..., s_main:]
    cr_main = cr[..., :s_main].reshape(batch, npairs, 2, nt, t).transpose(0, 1, 3, 2, 4)
    bounds = jnp.concatenate(
        [cr[..., t - 1:s_main:t], cr[..., seq_len - 1:],
         jnp.zeros((batch, npairs, 2, LANES - nt - 1), F32)], axis=-1)
    nt8 = -(-nt // 8) * 8
    starts = jnp.pad(cr[..., 0:s_main:t], ((0, 0), (0, 0), (0, 0), (0, nt8 - nt)))
    starts = jnp.broadcast_to(starts[..., None], (batch, npairs, 2, nt8, LANES))
    whole = pl.BlockSpec((1, seq_len, LANES), lambda b, p: (b, 0, p))
    out = pl.pallas_call(
        _fox_kernel,
        grid=(batch, npairs),
        in_specs=[whole, whole, whole,
                  pl.BlockSpec((1, seq_len, LANES), lambda b, p: (b, 0, 0)),
                  pl.BlockSpec((1, 1, 2, BLOCK), lambda b, p: (b, p, 0, 0)),
                  pl.BlockSpec((1, 1, nt, 2, t), lambda b, p: (b, p, 0, 0, 0)),
                  pl.BlockSpec((1, 1, 2, LANES), lambda b, p: (b, p, 0, 0)),
                  pl.BlockSpec((1, 1, 2, nt8, LANES), lambda b, p: (b, p, 0, 0, 0))],
        out_specs=whole,
        out_shape=jax.ShapeDtypeStruct((batch, seq_len, FOX_WIDTH), BF16),
        scratch_shapes=[pltpu.VMEM((seq_len, LANES), BF16), pltpu.VMEM((seq_len, LANES), BF16),
                        pltpu.VMEM((s_main, LANES), F32), pltpu.VMEM((s_main, LANES), F32),
                        pltpu.VMEM((nt8, LANES), F32),
                        pltpu.SMEM((3, nt), jnp.int32),
                        pltpu.VMEM((2 * t, 1), F32), pltpu.VMEM((2 * t, 1), F32),
                        pltpu.VMEM((2 * t, LANES), F32)],
        compiler_params=pltpu.CompilerParams(
            dimension_semantics=("parallel", "parallel"), vmem_limit_bytes=VMEM_LIMIT),
    )(r3(fq), r3(fk), r3(fv), cum_col, cr_prefix, cr_main, bounds, starts)
    return out.reshape(batch * seq_len, FOX_WIDTH)


def _post_kernel(of_ref, og_ref, gr_ref, gma_ref, gmb_ref, h_ref, ggla_ref, wfo_ref, wgo_ref,
                 wout_ref, gpost_ref, o_ref):
    og = og_ref[...]
    ggla = ggla_ref[...]
    heads = []
    for hh in range(GLA_HEADS):
        sl = slice(hh * GLA_DV, (hh + 1) * GLA_DV)
        heads.append(_rms(og[:, sl], ggla[:, sl]))
    gr = gr_ref[...].astype(F32)
    o_gla = (jnp.concatenate(heads, axis=1) * (gr * jax.nn.sigmoid(gr))).astype(BF16)
    a = _dot(of_ref[...], wfo_ref[...])
    b = _dot(o_gla, wgo_ref[...])
    y = jax.nn.sigmoid(gma_ref[...].astype(F32)) * a + jax.nn.sigmoid(gmb_ref[...].astype(F32)) * b
    mix = _dot(y.astype(BF16), wout_ref[...])
    o_ref[...] = h_ref[...] + _rms(mix, gpost_ref[...])


def _post(o_fox, o_gla, gr, gma, gmb, h, ggla, wfo, wgo, wout, gpost, layer):
    n = h.shape[0]
    tm = ROW_TM
    row = lambda a: pl.BlockSpec((tm, a.shape[1]), lambda i: (i, 0))
    full = lambda a: (_resident(a.shape[1:], layer) if a.ndim == 3
                      else pl.BlockSpec(a.shape, lambda i: (0, 0)))
    args = (o_fox, o_gla, gr, gma, gmb, h, ggla, wfo, wgo, wout, gpost)
    return pl.pallas_call(
        _post_kernel,
        grid=(n // tm,),
        in_specs=[row(a) for a in args[:6]] + [full(a) for a in args[6:]],
        out_specs=pl.BlockSpec((tm, D_MODEL), lambda i: (i, 0)),
        out_shape=jax.ShapeDtypeStruct((n, D_MODEL), F32),
        compiler_params=pltpu.CompilerParams(
            dimension_semantics=("parallel",), vmem_limit_bytes=VMEM_LIMIT),
    )(*args)


def kernel(x, meta_tokens, w_in, w_alpha_up, b_alpha, b_f, g_gla_out, w_fox_o, w_gla_o, w_out,
           w_ffn1_gu, w_ffn1_down, w_ffn2_gu, w_ffn2_down,
           g_pre_ffn1, g_post_ffn1, g_pre_mix, g_post_mix, g_pre_ffn2, g_post_ffn2):
    batch, seq, d = x.shape
    seq_len = seq + BLOCK
    depth = w_in.shape[0]
    assert d == D_MODEL and (seq_len - BLOCK) % FOX_T == 0 and seq_len % ROW_TM == 0
    assert seq_len % FFN_TM == 0 and seq % FFN_TM_MAIN == 0

    pad = jnp.zeros((batch, N_PAD, d), x.dtype)
    meta = jnp.broadcast_to(meta_tokens.astype(x.dtype)[None], (batch, N_META, d))
    prefix = jnp.concatenate([pad, meta], axis=1)
    flat = lambda a: a.reshape(batch * seq_len, a.shape[-1])

    row2 = lambda a: a.reshape(1, -1).astype(F32)
    w_perm = w_in.astype(BF16)
    w1_gu, w1_down = w_ffn1_gu.astype(BF16), w_ffn1_down.astype(BF16)
    w2_gu, w2_down = w_ffn2_gu.astype(BF16), w_ffn2_down.astype(BF16)
    wfo, wgo, wout = w_fox_o.astype(BF16), w_gla_o.astype(BF16), w_out.astype(BF16)
    for l in range(depth):
        wa_pad = jnp.zeros((LANES, GLA_KWIDTH), F32).at[
            FOX_HEADS:FOX_HEADS + GLA_GATE_RANK].set(w_alpha_up[l]).astype(BF16)
        bf_pad = jnp.zeros((1, LANES), F32).at[0, :FOX_HEADS].set(b_f[l])

        if l == 0:
            h = _ffn_first(x, prefix, row2(g_pre_ffn1[l]), w1_gu, w1_down, row2(g_post_ffn1[l]), l)
        else:
            h = _ffn(h, row2(g_pre_ffn1[l]), w1_gu, w1_down, row2(g_post_ffn1[l]), l)
        fq, fk, fv, gq, gk, gv, gr, gma, gmb, la, lf = _proj(
            flat(h), row2(g_pre_mix[l]), w_perm, wa_pad, row2(b_alpha[l]), bf_pad, seq_len, l)
        o_gla, cum_col, cum_row = _gla(gq, gk, gv, la, lf, batch, seq_len)
        o_fox = _fox(fq, fk, fv, cum_col, cum_row, batch, seq_len)
        h = _post(o_fox, flat(o_gla), gr, gma, gmb, flat(h),
                  row2(g_gla_out[l]), wfo, wgo, wout, row2(g_post_mix[l]), l).reshape(batch, seq_len, d)
        h = _ffn(h, row2(g_pre_ffn2[l]), w2_gu, w2_down, row2(g_post_ffn2[l]), l,
                 main_rows=seq if l == depth - 1 else None)
    return h
```

```python
import functools

import jax
import jax.numpy as jnp
import numpy as np
from jax import lax
from jax.experimental import pallas as pl
from jax.experimental.pallas import tpu as pltpu

D_MODEL = 1024
N_META = 16
BLOCK = 128
N_PAD = BLOCK - N_META
FOX_HEADS = 8
FOX_HEAD_DIM = 64
FOX_WIDTH = FOX_HEADS * FOX_HEAD_DIM
GLA_HEADS = 4
GLA_WIDTH = 512
GLA_DV = 128
GLA_DK = 64
GLA_KWIDTH = GLA_HEADS * GLA_DK
GLA_GATE_RANK = 16
GLA_GATE_TEMP = 16.0
D_FF = 2816
RMS_EPS = 1e-6

LANES = 128
SUB = 16
N_SUB = BLOCK // SUB
EXP_CAP = 80.0
NEG = -1e30

FFN_TM = 640
FFN_TM_MAIN = 512
FFN_TF = 256
ROW_TM = 640
FOX_T = 512
LOG2E = 1.4426950408889634
PRUNE_BITS = 160.0
NORM_UP = 1.01
FAST_MAX_BITS = 50.0
VMEM_LIMIT = 56 * 1024 * 1024

BF16 = jnp.bfloat16
F32 = jnp.float32

C_FQ, C_FK, C_FV = 0, 512, 1024
C_GQ, C_GK, C_GV = 1536, 1792, 2048
C_GR, C_GMA, C_GMB = 2560, 3072, 4096
C_SMALL = 5120
N_IN = 5144
N_IN_PERM = 5248
W_IN_MOVES = ((0, 0, 1536), (1536, 1544, 1024), (2560, 2584, 2560), (5120, 1536, 8), (5128, 2568, 16))


def _dot(a, b):
    return jnp.dot(a, b, preferred_element_type=F32)


def _dot_nt(a, b):
    return lax.dot_general(a, b, (((1,), (1,)), ((), ())), preferred_element_type=F32)


def _dot_tn(a, b):
    return lax.dot_general(a, b, (((0,), (0,)), ((), ())), preferred_element_type=F32)


def _rms(x, g):
    return x * lax.rsqrt(jnp.mean(x * x, axis=-1, keepdims=True) + RMS_EPS) * g


def _log_sigmoid(x):
    return jnp.minimum(x, 0.0) - jnp.log1p(jnp.exp(-jnp.abs(x)))


def _split3(x):
    hi = x.astype(BF16)
    r1 = x - hi.astype(F32)
    mid = r1.astype(BF16)
    lo = (r1 - mid.astype(F32)).astype(BF16)
    return hi, mid, lo


def _dot_exact_lhs(m, parts):
    return _dot(m, parts[0]) + _dot(m, parts[1]) + _dot(m, parts[2])


def _ffn_rows(h, gpre_ref, wgu_ref, wd_ref, gpost_ref, xn_ref, a_ref):
    m = h.shape[0]
    xn_ref[0:m] = _rms(h, gpre_ref[...]).astype(BF16)
    for c in range(D_FF // FFN_TF):
        lo = c * FFN_TF
        g = _dot(xn_ref[0:m], wgu_ref[:, lo:lo + FFN_TF])
        u = _dot(xn_ref[0:m], wgu_ref[:, D_FF + lo:D_FF + lo + FFN_TF])
        a_ref[0:m, lo:lo + FFN_TF] = (g * jax.nn.sigmoid(g) * u).astype(BF16)
    y = _dot(a_ref[0:m], wd_ref[...])
    return h + 0.5 * _rms(y, gpost_ref[...])


def _ffn_kernel(h_ref, gpre_ref, wgu_ref, wd_ref, gpost_ref, o_ref, xn_ref, a_ref):
    o_ref[...] = _ffn_rows(h_ref[...], gpre_ref, wgu_ref, wd_ref, gpost_ref, xn_ref, a_ref)


def _ffn_first_kernel(x_ref, p_ref, gpre_ref, wgu_ref, wd_ref, gpost_ref, o_ref, xn_ref, a_ref):
    last = pl.program_id(1) == pl.num_programs(1) - 1

    @pl.when(jnp.logical_not(last))
    def _():
        o_ref[...] = _ffn_rows(x_ref[...], gpre_ref, wgu_ref, wd_ref, gpost_ref, xn_ref, a_ref)

    @pl.when(last)
    def _():
        o_ref[0:BLOCK] = _ffn_rows(p_ref[...], gpre_ref, wgu_ref, wd_ref, gpost_ref, xn_ref, a_ref)


def _resident(shape, layer):
    return pl.BlockSpec((None,) + shape, lambda *_: (layer,) + (0,) * len(shape),
                        pipeline_mode=pl.Buffered(1))


def _ffn(h, gpre, w_gu, w_down, gpost, layer, main_rows=None):
    batch, seq_len, _ = h.shape
    if main_rows is None:
        tm, rows = FFN_TM, seq_len
    else:
        tm, rows = FFN_TM_MAIN, main_rows
    tile = pl.BlockSpec((None, tm, D_MODEL), lambda b, i: (b, i, 0))
    vec = pl.BlockSpec((1, D_MODEL), lambda b, i: (0, 0))
    return pl.pallas_call(
        _ffn_kernel,
        grid=(batch, rows // tm),
        in_specs=[tile, vec, _resident((D_MODEL, 2 * D_FF), layer), _resident((D_FF, D_MODEL), layer), vec],
        out_specs=tile,
        out_shape=jax.ShapeDtypeStruct((batch, rows, D_MODEL), F32),
        scratch_shapes=[pltpu.VMEM((tm, D_MODEL), BF16), pltpu.VMEM((tm, D_FF), BF16)],
        compiler_params=pltpu.CompilerParams(
            dimension_semantics=("parallel", "parallel"), vmem_limit_bytes=VMEM_LIMIT),
    )(h, gpre, w_gu, w_down, gpost)


def _ffn_first(x, prefix, gpre, w_gu, w_down, gpost, layer):
    batch, seq, _ = x.shape
    tm = FFN_TM_MAIN
    n_main = seq // tm
    vec = pl.BlockSpec((1, D_MODEL), lambda b, i: (0, 0))
    return pl.pallas_call(
        _ffn_first_kernel,
        grid=(batch, n_main + 1),
        in_specs=[pl.BlockSpec((None, tm, D_MODEL), lambda b, i: (b, jnp.minimum(i, n_main - 1), 0)),
                  pl.BlockSpec((None, BLOCK, D_MODEL), lambda b, i: (b, 0, 0)),
                  vec, _resident((D_MODEL, 2 * D_FF), layer), _resident((D_FF, D_MODEL), layer), vec],
        out_specs=pl.BlockSpec((None, tm, D_MODEL), lambda b, i: (b, i, 0)),
        out_shape=jax.ShapeDtypeStruct((batch, seq + BLOCK, D_MODEL), F32),
        scratch_shapes=[pltpu.VMEM((tm, D_MODEL), BF16), pltpu.VMEM((tm, D_FF), BF16)],
        compiler_params=pltpu.CompilerParams(
            dimension_semantics=("parallel", "arbitrary"), vmem_limit_bytes=VMEM_LIMIT),
    )(x, prefix, gpre, w_gu, w_down, gpost)


def _proj_kernel(h_ref, g_ref, w_ref, wa_ref, ba_ref, bf_ref,
                 fq_ref, fk_ref, fv_ref, gq_ref, gk_ref, gv_ref, gr_ref, gma_ref, gmb_ref,
                 la_ref, lf_ref, ws_ref, *, tiles_per_seq, main_rows):
    @pl.when(pl.program_id(0) == 0)
    def _():
        for dst, src, width in W_IN_MOVES:
            ws_ref[:, dst:dst + width] = w_ref[:, src:src + width]
        ws_ref[:, N_IN:] = jnp.zeros((D_MODEL, N_IN_PERM - N_IN), BF16)

    w_ref = ws_ref
    tm = h_ref.shape[0]
    xn = _rms(h_ref[...], g_ref[...]).astype(BF16)
    pos = (pl.program_id(0) % tiles_per_seq) * tm + lax.broadcasted_iota(jnp.int32, (tm, 1), 0)
    valid = jnp.logical_or(pos < main_rows, pos >= main_rows + N_PAD).astype(F32)

    def proj(c0, width):
        return _dot(xn, w_ref[:, c0:c0 + width])

    fq_ref[...] = (proj(C_FQ, FOX_WIDTH) * (FOX_HEAD_DIM ** -0.5 * LOG2E)).astype(BF16)
    fk_ref[...] = proj(C_FK, FOX_WIDTH).astype(BF16)
    fv_ref[...] = proj(C_FV, FOX_WIDTH).astype(BF16)
    gq_ref[...] = proj(C_GQ, GLA_KWIDTH) * (GLA_DK ** -0.5)
    gk_ref[...] = proj(C_GK, GLA_KWIDTH) * valid
    gv_ref[...] = proj(C_GV, GLA_WIDTH).astype(BF16)
    gr_ref[...] = proj(C_GR, GLA_WIDTH).astype(BF16)
    gma_ref[...] = proj(C_GMA, D_MODEL).astype(BF16)
    gmb_ref[...] = proj(C_GMB, D_MODEL).astype(BF16)
    small = proj(C_SMALL, LANES)
    lane = lax.broadcasted_iota(jnp.int32, (tm, LANES), 1)
    lf = _log_sigmoid(small + bf_ref[...]) * (valid * LOG2E)
    lf_ref[...] = jnp.where(lane < FOX_HEADS, lf, 0.0)
    xa = _dot(small.astype(BF16), wa_ref[...]) + ba_ref[...]
    la_ref[...] = _log_sigmoid(xa) * (valid * (1.0 / GLA_GATE_TEMP))


def _proj(h, g, w_perm, wa_pad, ba, bf_pad, seq_len, layer):
    n = h.shape[0]
    tm = ROW_TM
    row = lambda width: pl.BlockSpec((tm, width), lambda i: (i, 0))
    full = lambda a: (_resident(a.shape[1:], layer) if a.ndim == 3
                      else pl.BlockSpec(a.shape, lambda i: (0, 0)))
    outs = [
        (FOX_WIDTH, BF16), (FOX_WIDTH, BF16), (FOX_WIDTH, BF16),
        (GLA_KWIDTH, F32), (GLA_KWIDTH, F32), (GLA_WIDTH, BF16),
        (GLA_WIDTH, BF16), (D_MODEL, BF16), (D_MODEL, BF16),
        (GLA_KWIDTH, F32), (LANES, F32),
    ]
    return pl.pallas_call(
        functools.partial(_proj_kernel, tiles_per_seq=seq_len // tm, main_rows=seq_len - BLOCK),
        grid=(n // tm,),
        in_specs=[row(D_MODEL), full(g), full(w_perm), full(wa_pad), full(ba), full(bf_pad)],
        out_specs=[row(w) for w, _ in outs],
        out_shape=[jax.ShapeDtypeStruct((n, w), dt) for w, dt in outs],
        scratch_shapes=[pltpu.VMEM((D_MODEL, N_IN_PERM), BF16)],
        compiler_params=pltpu.CompilerParams(
            dimension_semantics=("arbitrary",), vmem_limit_bytes=VMEM_LIMIT),
    )(h, g, w_perm, wa_pad, ba, bf_pad)


def _gla_kernel(q_ref, k_ref, v_ref, la_ref, lf_ref, o_ref, cc_ref, cr_ref, st_ref, carry_ref):
    @pl.when(pl.program_id(0) == 0)
    def _():
        st_ref[...] = jnp.zeros_like(st_ref)
        carry_ref[...] = jnp.zeros_like(carry_ref)

    row = lax.broadcasted_iota(jnp.int32, (BLOCK, BLOCK), 0)
    col = lax.broadcasted_iota(jnp.int32, (BLOCK, BLOCK), 1)
    causal = row >= col
    tril = causal.astype(BF16)
    tril_sub = (causal & ((row // SUB) == (col // SUB))).astype(BF16)
    row_w = lax.broadcasted_iota(jnp.int32, (BLOCK, GLA_KWIDTH), 0)
    row_blk = row // SUB

    batches = range(q_ref.shape[0])
    problems = [(b, h) for b in batches for h in range(GLA_HEADS)]

    cb_all, cw_all = [], []
    for b in batches:
        cl = _dot_exact_lhs(tril, _split3(lf_ref[b])) + carry_ref[b]
        carry_ref[b] = cl[BLOCK - 1:BLOCK, :]
        cc_ref[b] = cl
        cr_ref[b] = cl.T[0:FOX_HEADS, :]
        la_parts = _split3(la_ref[b])
        cb_all.append(_dot_exact_lhs(tril, la_parts))
        cw_all.append(_dot_exact_lhs(tril_sub, la_parts))

    ops = {}
    for b in batches:
        cb, cw = cb_all[b], cw_all[b]
        last = cb[BLOCK - 1:BLOCK, :]
        q = q_ref[b]
        k = k_ref[b]
        q_glob = (q * jnp.exp(cb)).astype(BF16)
        q_loc = (q * jnp.exp(cw)).astype(BF16)
        k_end = (k * jnp.exp(last - cb)).astype(BF16)
        decay = jnp.exp(last)
        k_sub = []
        for i in range(N_SUB):
            ref = cb[i * SUB - 1:i * SUB, :] if i > 0 else jnp.zeros_like(last)
            e = jnp.exp(jnp.minimum(ref - cb, EXP_CAP))
            k_sub.append(jnp.where(row_w < (i + 1) * SUB, k * e, 0.0).astype(BF16))
        for pair in range(GLA_HEADS // 2):
            sl = slice(pair * LANES, (pair + 1) * LANES)
            k_cat = jnp.concatenate([ks[:, sl] for ks in k_sub], axis=1)
            for half in range(2):
                h = 2 * pair + half
                head_lanes = (col < GLA_DK) if half == 0 else (col >= GLA_DK)
                q_h = jnp.where(head_lanes, q_loc[:, sl], 0.0)
                q_cat = jnp.concatenate(
                    [jnp.where(row_blk == i, q_h, 0.0) for i in range(N_SUB)], axis=1)
                ops[b, h] = (q_cat, k_cat, jnp.where(head_lanes, q_glob[:, sl], 0.0), k_end[:, sl],
                             decay[:, sl], v_ref[b, :, h * GLA_DV:(h + 1) * GLA_DV])

    att, inter, upd = {}, {}, {}
    for p in problems:
        q_cat, k_cat, q_g, k_e, _, v_h = ops[p]
        att[p] = _dot_nt(q_cat, k_cat)
        inter[p] = _dot_nt(q_g, st_ref[p].astype(BF16))
        upd[p] = _dot_tn(v_h, k_e)

    for p in problems:
        b, h = p
        a = jnp.where(causal, att[p], 0.0).astype(BF16)
        o_ref[b, :, h * GLA_DV:(h + 1) * GLA_DV] = inter[p] + _dot(a, ops[p][5])
        st_ref[p] = st_ref[p] * ops[p][4] + upd[p]


def _gla(gq, gk, gv, la, lf, batch, seq_len):
    nc = seq_len // BLOCK
    chunk = lambda c: (c + nc - 1) % nc
    blk = lambda width: pl.BlockSpec((batch, BLOCK, width), lambda c: (0, chunk(c), 0))
    r3 = lambda a: a.reshape(batch, seq_len, a.shape[-1])
    return pl.pallas_call(
        _gla_kernel,
        grid=(nc,),
        in_specs=[blk(GLA_KWIDTH), blk(GLA_KWIDTH), blk(GLA_WIDTH), blk(GLA_KWIDTH), blk(LANES)],
        out_specs=[blk(GLA_WIDTH), blk(LANES),
                   pl.BlockSpec((batch, FOX_HEADS, BLOCK), lambda c: (0, 0, chunk(c)))],
        out_shape=[jax.ShapeDtypeStruct((batch, seq_len, GLA_WIDTH), F32),
                   jax.ShapeDtypeStruct((batch, seq_len, LANES), F32),
                   jax.ShapeDtypeStruct((batch, FOX_HEADS, seq_len), F32)],
        scratch_shapes=[pltpu.VMEM((batch, GLA_HEADS, GLA_DV, LANES), F32),
                        pltpu.VMEM((batch, 1, LANES), F32)],
        compiler_params=pltpu.CompilerParams(
            dimension_semantics=("arbitrary",), vmem_limit_bytes=VMEM_LIMIT),
    )(r3(gq), r3(gk), r3(gv), r3(la), r3(lf))


def _fox_kernel(q_ref, k_ref, v_ref, cc_ref, crp_ref, crm_ref, cb_ref, cs_ref, o_ref,
                va_ref, vb_ref, ra_ref, rb_ref, ub_ref, plan_ref, m_ref, l_ref, acc_ref):
    t = FOX_T
    nt = crm_ref.shape[2]
    s_main = nt * t
    pre = slice(s_main, s_main + BLOCK)
    pair = pl.program_id(1)

    def stack_heads(x, lane_first):
        return jnp.concatenate([jnp.where(lane_first, x, 0.0), jnp.where(lane_first, 0.0, x)], axis=0)

    def head_cols(cc, lane):
        return jnp.concatenate(
            [jnp.sum(jnp.where(lane == 2 * pair + hh, cc, 0.0), axis=1, keepdims=True)
             for hh in range(2)], axis=0)

    sel = lax.broadcasted_iota(jnp.int32, (8, LANES), 0) == lax.broadcasted_iota(
        jnp.int32, (8, LANES), 1) // FOX_HEAD_DIM
    sel = sel.astype(BF16)

    def max_sq_norm(x):
        return jnp.max(_dot_nt(sel, x * x), axis=1, keepdims=True)

    lane_b = lax.broadcasted_iota(jnp.int32, (BLOCK, LANES), 1)
    first_b = lane_b < FOX_HEAD_DIM
    kp = k_ref[0, pre, :]
    vp = v_ref[0, pre, :]
    ckp = crp_ref[0, 0]
    row_b = lax.broadcasted_iota(jnp.int32, (BLOCK, BLOCK), 0)
    ok_b = (lane_b <= row_b) & ((lane_b >= N_PAD) | (lane_b == row_b))
    ok_b = jnp.concatenate([ok_b, ok_b], axis=0)
    s = _dot_nt(stack_heads(q_ref[0, pre, :], first_b), kp)
    z = jnp.concatenate([s[:BLOCK] - ckp[0:1], s[BLOCK:] - ckp[1:2]], axis=0)
    z = jnp.where(ok_b, z, NEG)
    cq2 = head_cols(cc_ref[0, pre, :], lane_b)
    m = jnp.max(z, axis=1, keepdims=True) + cq2
    p = jnp.exp2(z - (m - cq2))
    out = _dot(p.astype(BF16), vp) / jnp.sum(p, axis=1, keepdims=True)
    o_ref[0, pre, :] = jnp.where(first_b, out[:BLOCK], out[BLOCK:]).astype(o_ref.dtype)

    lane_s = lax.broadcasted_iota(jnp.int32, v_ref.shape[1:], 1)
    v_all = v_ref[0]
    va_ref[...] = jnp.where(lane_s < FOX_HEAD_DIM, v_all, jnp.where(lane_s == FOX_HEAD_DIM, 1.0, 0.0).astype(BF16))
    vb_ref[...] = jnp.where(lane_s >= FOX_HEAD_DIM, v_all, jnp.where(lane_s == 0, 1.0, 0.0).astype(BF16))

    k_max = jnp.sqrt(max_sq_norm(k_ref[0]))
    lane = lax.broadcasted_iota(jnp.int32, (t, LANES), 1)
    first = lane < FOX_HEAD_DIM
    row_l = lax.broadcasted_iota(jnp.int32, (LANES, LANES), 0)
    row_w = lax.broadcasted_iota(jnp.int32, (LANES, 2 * LANES), 0)
    second = lax.broadcasted_iota(jnp.int32, (LANES, 2 * LANES), 1) >= LANES
    ind_ab = ((row_w >= FOX_HEAD_DIM) == second).astype(BF16)
    hot_ab = (row_w == 2 * pair + second.astype(jnp.int32)).astype(BF16)
    hot_la = (row_l == FOX_HEAD_DIM).astype(BF16)
    hot_lb = (row_l == 0).astype(BF16)
    reps = t // LANES

    def prepare(i, carry):
        rows = pl.ds(pl.multiple_of(i * t, t), t)
        q = q_ref[0, rows, :]
        nq = _dot(q * q, ind_ab)
        cq = _dot(cc_ref[0, rows, :].astype(BF16), hot_ab)
        ub_a = jnp.sqrt(nq[:, :LANES]) * (k_max[0:1] * NORM_UP)
        ub_b = jnp.sqrt(nq[:, LANES:]) * (k_max[1:2] * NORM_UP)
        ra_ref[rows, :] = ub_a - cq[:, :LANES]
        rb_ref[rows, :] = ub_b - cq[:, LANES:]
        ub_ref[pl.ds(i, 1), :] = jnp.max(jnp.maximum(ub_a, ub_b), axis=0, keepdims=True)
        return carry

    lax.fori_loop(0, nt, prepare, 0)

    ub_t = ub_ref[0:nt, :]
    fast_v = ub_t <= FAST_MAX_BITS
    bound = jnp.where(fast_v, PRUNE_BITS, PRUNE_BITS + 2.0 * ub_t)
    cb = cb_ref[0, 0]
    dead = ((cs_ref[0, 0, 0, 0:nt, :] - cb[0:1]) < -bound) & ((cs_ref[0, 0, 1, 0:nt, :] - cb[1:2]) < -bound)
    tile_q = lax.broadcasted_iota(jnp.int32, (nt, LANES), 0)
    tile_k = lax.broadcasted_iota(jnp.int32, (nt, LANES), 1)
    n_skip_v = jnp.sum(jnp.where(dead & (tile_k < tile_q), 1, 0), axis=1, keepdims=True)
    skip_pre_v = jnp.sum(jnp.where(dead & (tile_k == nt), 1, 0), axis=1, keepdims=True)
    fast_i = jnp.where(fast_v[:, 0:1], 1, 0)
    tile_1 = tile_q[:, 0:1]
    for i in range(nt):
        for row, vec in enumerate((fast_i, n_skip_v, skip_pre_v)):
            plan_ref[row, i] = jnp.sum(jnp.where(tile_1 == i, vec, 0))

    def main_tile(i, carry):
        r0 = pl.multiple_of(i * t, t)
        q2 = stack_heads(q_ref[0, pl.ds(r0, t), :], first)
        fast = plan_ref[0, i] == 1
        n_skip = plan_ref[1, i]
        skip_prefix = plan_ref[2, i]

        def logits(kt, ck, offs):
            s = _dot_nt(q2, kt)
            return jnp.concatenate([s[:t] - ck[0:1] - offs[:t], s[t:] - ck[1:2] - offs[t:]], axis=0)

        def run(update, offs_p, offs_m, fuse_tail):
            acc_ref[...] = jnp.zeros_like(acc_ref)

            @pl.when(skip_prefix == 0)
            def _():
                z = logits(kp, ckp, offs_p)
                lane_k = lax.broadcasted_iota(jnp.int32, (2 * t, BLOCK), 1)
                update([jnp.where(lane_k >= N_PAD, z, NEG)], [pre])

            def tiles(js, diagonal):
                rows = [pl.ds(pl.multiple_of(j * t, t), t) for j in js]
                zs = [logits(k_ref[0, r, :], crm_ref[0, 0, j], offs_m) for j, r in zip(js, rows)]
                if diagonal:
                    ok = lax.broadcasted_iota(jnp.int32, (t, t), 1) <= lax.broadcasted_iota(jnp.int32, (t, t), 0)
                    zs[-1] = jnp.where(jnp.concatenate([ok, ok], axis=0), zs[-1], NEG)
                update(zs, rows)

            def body(j, c):
                tiles([j], False)
                return c

            if not fuse_tail:
                lax.fori_loop(n_skip, i, body, 0)
                tiles([i], True)
                return
            n_live = i - n_skip
            lax.fori_loop(n_skip, jnp.maximum(i - 2, n_skip), body, 0)

            @pl.when(n_live >= 2)
            def _():
                tiles([i - 2, i - 1, i], True)

            @pl.when(n_live == 1)
            def _():
                tiles([i - 1, i], True)

            @pl.when(n_live == 0)
            def _():
                tiles([i], True)

        def finish(l_a, l_b):
            acc = acc_ref[...]
            out = jnp.where(first, acc[:t] / l_a, acc[t:] / l_b)
            o_ref[0, pl.ds(r0, t), :] = out.astype(o_ref.dtype)

        @pl.when(fast)
        def _():
            ref = jnp.concatenate([ra_ref[pl.ds(r0, t), :], rb_ref[pl.ds(r0, t), :]], axis=0)

            def update(zs, rows):
                ps = [jnp.exp2(z).astype(BF16) for z in zs]
                for half, v_aug in ((slice(0, t), va_ref), (slice(t, 2 * t), vb_ref)):
                    pv = _dot(ps[0][half], v_aug[rows[0], :])
                    for p, r in zip(ps[1:], rows[1:]):
                        pv = pv + _dot(p[half], v_aug[r, :])
                    acc_ref[half] += pv

            run(update, ref, jnp.concatenate([ref] * reps, axis=1), True)

            def denominators(acc, hot):
                hi = acc.astype(BF16)
                lo = (acc - hi.astype(F32)).astype(BF16)
                return _dot(hi, hot) + _dot(lo, hot)

            finish(denominators(acc_ref[0:t], hot_la), denominators(acc_ref[t:], hot_lb))

        @pl.when(jnp.logical_not(fast))
        def _():
            cq2 = head_cols(cc_ref[0, pl.ds(r0, t), :], lane)
            m_ref[...] = jnp.full_like(m_ref, NEG)
            l_ref[...] = jnp.zeros_like(l_ref)

            def update(zs, rows):
                (z,), (r,) = zs, rows
                m_old = m_ref[...]
                m_new = jnp.maximum(m_old, jnp.max(z, axis=1, keepdims=True) + cq2)
                p = jnp.exp2(z - (m_new - cq2))
                alpha = jnp.exp2(m_old - m_new)
                l_ref[...] = alpha * l_ref[...] + jnp.sum(p, axis=1, keepdims=True)
                acc_ref[...] = alpha * acc_ref[...] + _dot(p.astype(BF16), v_ref[0, r, :])
                m_ref[...] = m_new

            run(update, jnp.zeros((2 * t, 1), F32), jnp.zeros((2 * t, 1), F32), False)
            finish(l_ref[0:t], l_ref[t:])

        return carry

    lax.fori_loop(0, nt, main_tile, 0)


def _fox(fq, fk, fv, cum_col, cum_row, batch, seq_len):
    t = FOX_T
    nt = (seq_len - BLOCK) // t
    npairs = FOX_HEADS // 2
    assert nt + 1 <= LANES
    s_main = nt * t
    r3 = lambda a: a.reshape(batch, seq_len, a.shape[-1])
    cr = cum_row.reshape(batch, npairs, 2, seq_len)
    cr_prefix = cr[..., s_main:]
    cr_main = cr[..., :s_main].reshape(batch, npairs, 2, nt, t).transpose(0, 1, 3, 2, 4)
    bounds = jnp.concatenate(
        [cr[..., t - 1:s_main:t], cr[..., seq_len - 1:],
         jnp.zeros((batch, npairs, 2, LANES - nt - 1), F32)], axis=-1)
    nt8 = -(-nt // 8) * 8
    starts = jnp.pad(cr[..., 0:s_main:t], ((0, 0), (0, 0), (0, 0), (0, nt8 - nt)))
    starts = jnp.broadcast_to(starts[..., None], (batch, npairs, 2, nt8, LANES))
    whole = pl.BlockSpec((1, seq_len, LANES), lambda b, p: (b, 0, p))
    out = pl.pallas_call(
        _fox_kernel,
        grid=(batch, npairs),
        in_specs=[whole, whole, whole,
                  pl.BlockSpec((1, seq_len, LANES), lambda b, p: (b, 0, 0)),
                  pl.BlockSpec((1, 1, 2, BLOCK), lambda b, p: (b, p, 0, 0)),
                  pl.BlockSpec((1, 1, nt, 2, t), lambda b, p: (b, p, 0, 0, 0)),
                  pl.BlockSpec((1, 1, 2, LANES), lambda b, p: (b, p, 0, 0)),
                  pl.BlockSpec((1, 1, 2, nt8, LANES), lambda b, p: (b, p, 0, 0, 0))],
        out_specs=whole,
        out_shape=jax.ShapeDtypeStruct((batch, seq_len, FOX_WIDTH), BF16),
        scratch_shapes=[pltpu.VMEM((seq_len, LANES), BF16), pltpu.VMEM((seq_len, LANES), BF16),
                        pltpu.VMEM((s_main, LANES), F32), pltpu.VMEM((s_main, LANES), F32),
                        pltpu.VMEM((nt8, LANES), F32),
                        pltpu.SMEM((3, nt), jnp.int32),
                        pltpu.VMEM((2 * t, 1), F32), pltpu.VMEM((2 * t, 1), F32),
                        pltpu.VMEM((2 * t, LANES), F32)],
        compiler_params=pltpu.CompilerParams(
            dimension_semantics=("parallel", "parallel"), vmem_limit_bytes=VMEM_LIMIT),
    )(r3(fq), r3(fk), r3(fv), cum_col, cr_prefix, cr_main, bounds, starts)
    return out.reshape(batch * seq_len, FOX_WIDTH)


def _post_kernel(of_ref, og_ref, gr_ref, gma_ref, gmb_ref, h_ref, ggla_ref, wfo_ref, wgo_ref,
                 wout_ref, gpost_ref, o_ref):
    tm = h_ref.shape[0]
    halves = (slice(0, tm // 2), slice(tm // 2, tm))
    ggla = ggla_ref[...]
    a = [_dot(of_ref[hs, :], wfo_ref[...]) for hs in halves]
    b = []
    for hs in halves:
        og = og_ref[hs, :]
        heads = [_rms(og[:, hh * GLA_DV:(hh + 1) * GLA_DV], ggla[:, hh * GLA_DV:(hh + 1) * GLA_DV])
                 for hh in range(GLA_HEADS)]
        gr = gr_ref[hs, :].astype(F32)
        o_gla = (jnp.concatenate(heads, axis=1) * (gr * jax.nn.sigmoid(gr))).astype(BF16)
        b.append(_dot(o_gla, wgo_ref[...]))
    mix = []
    for hs, a_h, b_h in zip(halves, a, b):
        y = jax.nn.sigmoid(gma_ref[hs, :].astype(F32)) * a_h + jax.nn.sigmoid(gmb_ref[hs, :].astype(F32)) * b_h
        mix.append(_dot(y.astype(BF16), wout_ref[...]))
    for hs, m_h in zip(halves, mix):
        o_ref[hs, :] = h_ref[hs, :] + _rms(m_h, gpost_ref[...])


def _post(o_fox, o_gla, gr, gma, gmb, h, ggla, wfo, wgo, wout, gpost, layer):
    n = h.shape[0]
    tm = ROW_TM
    row = lambda a: pl.BlockSpec((tm, a.shape[1]), lambda i: (i, 0))
    full = lambda a: (_resident(a.shape[1:], layer) if a.ndim == 3
                      else pl.BlockSpec(a.shape, lambda i: (0, 0)))
    args = (o_fox, o_gla, gr, gma, gmb, h, ggla, wfo, wgo, wout, gpost)
    return pl.pallas_call(
        _post_kernel,
        grid=(n // tm,),
        in_specs=[row(a) for a in args[:6]] + [full(a) for a in args[6:]],
        out_specs=pl.BlockSpec((tm, D_MODEL), lambda i: (i, 0)),
        out_shape=jax.ShapeDtypeStruct((n, D_MODEL), F32),
        compiler_params=pltpu.CompilerParams(
            dimension_semantics=("parallel",), vmem_limit_bytes=VMEM_LIMIT),
    )(*args)


def kernel(x, meta_tokens, w_in, w_alpha_up, b_alpha, b_f, g_gla_out, w_fox_o, w_gla_o, w_out,
           w_ffn1_gu, w_ffn1_down, w_ffn2_gu, w_ffn2_down,
           g_pre_ffn1, g_post_ffn1, g_pre_mix, g_post_mix, g_pre_ffn2, g_post_ffn2):
    batch, seq, d = x.shape
    seq_len = seq + BLOCK
    depth = w_in.shape[0]
    assert d == D_MODEL and (seq_len - BLOCK) % FOX_T == 0 and seq_len % ROW_TM == 0
    assert seq_len % FFN_TM == 0 and seq % FFN_TM_MAIN == 0

    pad = jnp.zeros((batch, N_PAD, d), x.dtype)
    meta = jnp.broadcast_to(meta_tokens.astype(x.dtype)[None], (batch, N_META, d))
    prefix = jnp.concatenate([pad, meta], axis=1)
    flat = lambda a: a.reshape(batch * seq_len, a.shape[-1])

    row2 = lambda a: a.reshape(1, -1).astype(F32)
    w_perm = jnp.pad(w_in, ((0, 0), (0, 0), (0, N_IN_PERM - N_IN))).astype(BF16)
    w1_gu, w1_down = w_ffn1_gu.astype(BF16), w_ffn1_down.astype(BF16)
    w2_gu, w2_down = w_ffn2_gu.astype(BF16), w_ffn2_down.astype(BF16)
    wfo, wgo, wout = w_fox_o.astype(BF16), w_gla_o.astype(BF16), w_out.astype(BF16)
    for l in range(depth):
        wa_pad = jnp.zeros((LANES, GLA_KWIDTH), F32).at[
            FOX_HEADS:FOX_HEADS + GLA_GATE_RANK].set(w_alpha_up[l]).astype(BF16)
        bf_pad = jnp.zeros((1, LANES), F32).at[0, :FOX_HEADS].set(b_f[l])

        if l == 0:
            h = _ffn_first(x, prefix, row2(g_pre_ffn1[l]), w1_gu, w1_down, row2(g_post_ffn1[l]), l)
        else:
            h = _ffn(h, row2(g_pre_ffn1[l]), w1_gu, w1_down, row2(g_post_ffn1[l]), l)
        fq, fk, fv, gq, gk, gv, gr, gma, gmb, la, lf = _proj(
            flat(h), row2(g_pre_mix[l]), w_perm, wa_pad, row2(b_alpha[l]), bf_pad, seq_len, l)
        o_gla, cum_col, cum_row = _gla(gq, gk, gv, la, lf, batch, seq_len)
        o_fox = _fox(fq, fk, fv, cum_col, cum_row, batch, seq_len)
        h = _post(o_fox, flat(o_gla), gr, gma, gmb, flat(h),
                  row2(g_gla_out[l]), wfo, wgo, wout, row2(g_post_mix[l]), l).reshape(batch, seq_len, d)
        h = _ffn(h, row2(g_pre_ffn2[l]), w2_gu, w2_down, row2(g_post_ffn2[l]), l,
                 main_rows=seq if l == depth - 1 else None)
    return h
```

```python
import functools

import jax
import jax.numpy as jnp
import numpy as np
from jax import lax
from jax.experimental import pallas as pl
from jax.experimental.pallas import tpu as pltpu

D_MODEL = 1024
N_META = 16
BLOCK = 128
N_PAD = BLOCK - N_META
FOX_HEADS = 8
FOX_HEAD_DIM = 64
FOX_WIDTH = FOX_HEADS * FOX_HEAD_DIM
GLA_HEADS = 4
GLA_WIDTH = 512
GLA_DV = 128
GLA_DK = 64
GLA_KWIDTH = GLA_HEADS * GLA_DK
GLA_GATE_RANK = 16
GLA_GATE_TEMP = 16.0
D_FF = 2816
RMS_EPS = 1e-6

LANES = 128
SUB = 16
N_SUB = BLOCK // SUB
EXP_CAP = 80.0
NEG = -1e30

FFN_TM = 640
FFN_TM_MAIN = 512
FFN_TF = 256
ROW_TM = 640
REGROUP_ROWS = 256
FOX_T = 512
LOG2E = 1.4426950408889634
PRUNE_BITS = 160.0
NORM_UP = 1.01
FAST_MAX_BITS = 50.0
VMEM_LIMIT = 56 * 1024 * 1024

BF16 = jnp.bfloat16
F32 = jnp.float32

C_FQ, C_FK, C_FV = 0, 512, 1024
C_GQ, C_GK, C_GV = 1536, 1792, 2048
C_GR, C_GMA, C_GMB = 2560, 3072, 4096
C_SMALL = 5120
N_IN = 5144
N_IN_PERM = 5248
W_IN_MOVES = ((0, 0, 1536), (1536, 1544, 1024), (2560, 2584, 2560), (5120, 1536, 8), (5128, 2568, 16))


def _dot(a, b):
    return jnp.dot(a, b, preferred_element_type=F32)


def _dot_nt(a, b):
    return lax.dot_general(a, b, (((1,), (1,)), ((), ())), preferred_element_type=F32)


def _dot_tn(a, b):
    return lax.dot_general(a, b, (((0,), (0,)), ((), ())), preferred_element_type=F32)


def _rms(x, g):
    return x * lax.rsqrt(jnp.mean(x * x, axis=-1, keepdims=True) + RMS_EPS) * g


def _log_sigmoid(x):
    return jnp.minimum(x, 0.0) - jnp.log1p(jnp.exp(-jnp.abs(x)))


def _split3(x):
    hi = x.astype(BF16)
    r1 = x - hi.astype(F32)
    mid = r1.astype(BF16)
    lo = (r1 - mid.astype(F32)).astype(BF16)
    return hi, mid, lo


def _dot_exact_lhs(m, parts):
    return _dot(m, parts[0]) + _dot(m, parts[1]) + _dot(m, parts[2])


def _ffn_rows(h, gpre_ref, wgu_ref, wd_ref, gpost_ref, xn_ref, a_ref):
    m = h.shape[0]
    parts = (slice(0, m // 2), slice(m // 2, m)) if m >= 2 * FFN_TF else (slice(0, m),)
    out = []
    for rows in parts:
        h_p = h[rows]
        xn_ref[rows] = _rms(h_p, gpre_ref[...]).astype(BF16)
        for c in range(D_FF // FFN_TF):
            lo = c * FFN_TF
            g = _dot(xn_ref[rows], wgu_ref[:, lo:lo + FFN_TF])
            u = _dot(xn_ref[rows], wgu_ref[:, D_FF + lo:D_FF + lo + FFN_TF])
            a_ref[rows, lo:lo + FFN_TF] = (g * jax.nn.sigmoid(g) * u).astype(BF16)
        y = _dot(a_ref[rows], wd_ref[...])
        out.append(h_p + 0.5 * _rms(y, gpost_ref[...]))
    return out[0] if len(out) == 1 else jnp.concatenate(out, axis=0)


def _ffn_kernel(h_ref, gpre_ref, wgu_ref, wd_ref, gpost_ref, o_ref, xn_ref, a_ref):
    o_ref[...] = _ffn_rows(h_ref[...], gpre_ref, wgu_ref, wd_ref, gpost_ref, xn_ref, a_ref)


def _ffn_first_kernel(x_ref, p_ref, gpre_ref, wgu_ref, wd_ref, gpost_ref, o_ref, xn_ref, a_ref):
    last = pl.program_id(1) == pl.num_programs(1) - 1

    @pl.when(jnp.logical_not(last))
    def _():
        o_ref[...] = _ffn_rows(x_ref[...], gpre_ref, wgu_ref, wd_ref, gpost_ref, xn_ref, a_ref)

    @pl.when(last)
    def _():
        o_ref[0:BLOCK] = _ffn_rows(p_ref[...], gpre_ref, wgu_ref, wd_ref, gpost_ref, xn_ref, a_ref)


def _resident(shape, layer):
    return pl.BlockSpec((None,) + shape, lambda *_: (layer,) + (0,) * len(shape),
                        pipeline_mode=pl.Buffered(1))


def _ffn(h, gpre, w_gu, w_down, gpost, layer, main_rows=None):
    batch, seq_len, _ = h.shape
    if main_rows is None:
        tm, rows = FFN_TM, seq_len
    else:
        tm, rows = FFN_TM_MAIN, main_rows
    tile = pl.BlockSpec((None, tm, D_MODEL), lambda b, i: (b, i, 0))
    vec = pl.BlockSpec((1, D_MODEL), lambda b, i: (0, 0))
    return pl.pallas_call(
        _ffn_kernel,
        grid=(batch, rows // tm),
        in_specs=[tile, vec, _resident((D_MODEL, 2 * D_FF), layer), _resident((D_FF, D_MODEL), layer), vec],
        out_specs=tile,
        out_shape=jax.ShapeDtypeStruct((batch, rows, D_MODEL), F32),
        scratch_shapes=[pltpu.VMEM((tm, D_MODEL), BF16), pltpu.VMEM((tm, D_FF), BF16)],
        compiler_params=pltpu.CompilerParams(
            dimension_semantics=("parallel", "parallel"), vmem_limit_bytes=VMEM_LIMIT),
    )(h, gpre, w_gu, w_down, gpost)


def _ffn_first(x, prefix, gpre, w_gu, w_down, gpost, layer):
    batch, seq, _ = x.shape
    tm = FFN_TM_MAIN
    n_main = seq // tm
    vec = pl.BlockSpec((1, D_MODEL), lambda b, i: (0, 0))
    return pl.pallas_call(
        _ffn_first_kernel,
        grid=(batch, n_main + 1),
        in_specs=[pl.BlockSpec((None, tm, D_MODEL), lambda b, i: (b, jnp.minimum(i, n_main - 1), 0)),
                  pl.BlockSpec((None, BLOCK, D_MODEL), lambda b, i: (b, 0, 0)),
                  vec, _resident((D_MODEL, 2 * D_FF), layer), _resident((D_FF, D_MODEL), layer), vec],
        out_specs=pl.BlockSpec((None, tm, D_MODEL), lambda b, i: (b, i, 0)),
        out_shape=jax.ShapeDtypeStruct((batch, seq + BLOCK, D_MODEL), F32),
        scratch_shapes=[pltpu.VMEM((tm, D_MODEL), BF16), pltpu.VMEM((tm, D_FF), BF16)],
        compiler_params=pltpu.CompilerParams(
            dimension_semantics=("parallel", "arbitrary"), vmem_limit_bytes=VMEM_LIMIT),
    )(x, prefix, gpre, w_gu, w_down, gpost)


def _proj_kernel(h_ref, g_ref, w_ref, wa_ref, ba_ref, bf_ref,
                 fq_ref, fk_ref, fv_ref, gq_ref, gk_ref, gv_ref, gr_ref, gma_ref, gmb_ref,
                 la_ref, lf_ref, *, tiles_per_seq, main_rows):
    tm = h_ref.shape[0]
    xn = _rms(h_ref[...], g_ref[...]).astype(BF16)
    pos = (pl.program_id(0) % tiles_per_seq) * tm + lax.broadcasted_iota(jnp.int32, (tm, 1), 0)
    valid = jnp.logical_or(pos < main_rows, pos >= main_rows + N_PAD).astype(F32)

    def proj(c0, width):
        return _dot(xn, w_ref[:, c0:c0 + width])

    fq_ref[...] = (proj(C_FQ, FOX_WIDTH) * (FOX_HEAD_DIM ** -0.5 * LOG2E)).astype(BF16)
    fk_ref[...] = proj(C_FK, FOX_WIDTH).astype(BF16)
    fv_ref[...] = proj(C_FV, FOX_WIDTH).astype(BF16)
    gq_ref[...] = proj(C_GQ, GLA_KWIDTH) * (GLA_DK ** -0.5)
    gk_ref[...] = proj(C_GK, GLA_KWIDTH) * valid
    gv_ref[...] = proj(C_GV, GLA_WIDTH).astype(BF16)
    gr_ref[...] = proj(C_GR, GLA_WIDTH).astype(BF16)
    gma_ref[...] = proj(C_GMA, D_MODEL).astype(BF16)
    gmb_ref[...] = proj(C_GMB, D_MODEL).astype(BF16)
    small = proj(C_SMALL, LANES)
    lane = lax.broadcasted_iota(jnp.int32, (tm, LANES), 1)
    lf = _log_sigmoid(small + bf_ref[...]) * (valid * LOG2E)
    lf_ref[...] = jnp.where(lane < FOX_HEADS, lf, 0.0)
    xa = _dot(small.astype(BF16), wa_ref[...]) + ba_ref[...]
    la_ref[...] = _log_sigmoid(xa) * (valid * (1.0 / GLA_GATE_TEMP))


def _proj(h, g, w_perm, wa_pad, ba, bf_pad, seq_len, layer):
    n = h.shape[0]
    tm = ROW_TM
    row = lambda width: pl.BlockSpec((tm, width), lambda i: (i, 0))
    full = lambda a: (_resident(a.shape[1:], layer) if a.ndim == 3
                      else pl.BlockSpec(a.shape, lambda i: (0, 0)))
    outs = [
        (FOX_WIDTH, BF16), (FOX_WIDTH, BF16), (FOX_WIDTH, BF16),
        (GLA_KWIDTH, F32), (GLA_KWIDTH, F32), (GLA_WIDTH, BF16),
        (GLA_WIDTH, BF16), (D_MODEL, BF16), (D_MODEL, BF16),
        (GLA_KWIDTH, F32), (LANES, F32),
    ]
    return pl.pallas_call(
        functools.partial(_proj_kernel, tiles_per_seq=seq_len // tm, main_rows=seq_len - BLOCK),
        grid=(n // tm,),
        in_specs=[row(D_MODEL), full(g), full(w_perm), full(wa_pad), full(ba), full(bf_pad)],
        out_specs=[row(w) for w, _ in outs],
        out_shape=[jax.ShapeDtypeStruct((n, w), dt) for w, dt in outs],
        compiler_params=pltpu.CompilerParams(
            dimension_semantics=("parallel",), vmem_limit_bytes=VMEM_LIMIT),
    )(h, g, w_perm, wa_pad, ba, bf_pad)


def _regroup_kernel(w_ref, o_ref):
    for dst, src, width in W_IN_MOVES:
        o_ref[:, dst:dst + width] = w_ref[:, src:src + width].astype(BF16)
    o_ref[:, N_IN:] = jnp.zeros((o_ref.shape[0], N_IN_PERM - N_IN), BF16)


def _regroup_w_in(w_in):
    depth = w_in.shape[0]
    tr = REGROUP_ROWS
    return pl.pallas_call(
        _regroup_kernel,
        grid=(depth, D_MODEL // tr),
        in_specs=[pl.BlockSpec((None, tr, N_IN), lambda l, i: (l, i, 0))],
        out_specs=pl.BlockSpec((None, tr, N_IN_PERM), lambda l, i: (l, i, 0)),
        out_shape=jax.ShapeDtypeStruct((depth, D_MODEL, N_IN_PERM), BF16),
        compiler_params=pltpu.CompilerParams(
            dimension_semantics=("parallel", "parallel"), vmem_limit_bytes=VMEM_LIMIT),
    )(w_in)


def _gla_kernel(q_ref, k_ref, v_ref, la_ref, lf_ref, o_ref, cc_ref, cr_ref, st_ref, carry_ref):
    @pl.when(pl.program_id(0) == 0)
    def _():
        st_ref[...] = jnp.zeros_like(st_ref)
        carry_ref[...] = jnp.zeros_like(carry_ref)

    row = lax.broadcasted_iota(jnp.int32, (BLOCK, BLOCK), 0)
    col = lax.broadcasted_iota(jnp.int32, (BLOCK, BLOCK), 1)
    causal = row >= col
    tril = causal.astype(BF16)
    tril_sub = (causal & ((row // SUB) == (col // SUB))).astype(BF16)
    row_w = lax.broadcasted_iota(jnp.int32, (BLOCK, GLA_KWIDTH), 0)
    row_blk = row // SUB

    batches = range(q_ref.shape[0])
    problems = [(b, h) for b in batches for h in range(GLA_HEADS)]

    cb_all, cw_all = [], []
    for b in batches:
        cl = _dot_exact_lhs(tril, _split3(lf_ref[b])) + carry_ref[b]
        carry_ref[b] = cl[BLOCK - 1:BLOCK, :]
        cc_ref[b] = cl
        cr_ref[b] = cl.T[0:FOX_HEADS, :]
        la_parts = _split3(la_ref[b])
        cb_all.append(_dot_exact_lhs(tril, la_parts))
        cw_all.append(_dot_exact_lhs(tril_sub, la_parts))

    ops = {}
    for b in batches:
        cb, cw = cb_all[b], cw_all[b]
        last = cb[BLOCK - 1:BLOCK, :]
        q = q_ref[b]
        k = k_ref[b]
        q_glob = (q * jnp.exp(cb)).astype(BF16)
        q_loc = (q * jnp.exp(cw)).astype(BF16)
        k_end = (k * jnp.exp(last - cb)).astype(BF16)
        decay = jnp.exp(last)
        k_sub = []
        for i in range(N_SUB):
            ref = cb[i * SUB - 1:i * SUB, :] if i > 0 else jnp.zeros_like(last)
            e = jnp.exp(jnp.minimum(ref - cb, EXP_CAP))
            k_sub.append(jnp.where(row_w < (i + 1) * SUB, k * e, 0.0).astype(BF16))
        for pair in range(GLA_HEADS // 2):
            sl = slice(pair * LANES, (pair + 1) * LANES)
            k_cat = jnp.concatenate([ks[:, sl] for ks in k_sub], axis=1)
            for half in range(2):
                h = 2 * pair + half
                head_lanes = (col < GLA_DK) if half == 0 else (col >= GLA_DK)
                q_h = jnp.where(head_lanes, q_loc[:, sl], 0.0)
                q_cat = jnp.concatenate(
                    [jnp.where(row_blk == i, q_h, 0.0) for i in range(N_SUB)], axis=1)
                ops[b, h] = (q_cat, k_cat, jnp.where(head_lanes, q_glob[:, sl], 0.0), k_end[:, sl],
                             decay[:, sl], v_ref[b, :, h * GLA_DV:(h + 1) * GLA_DV])

    att, inter, upd = {}, {}, {}
    for p in problems:
        q_cat, k_cat, q_g, k_e, _, v_h = ops[p]
        att[p] = _dot_nt(q_cat, k_cat)
        inter[p] = _dot_nt(q_g, st_ref[p].astype(BF16))
        upd[p] = _dot_tn(v_h, k_e)

    for p in problems:
        b, h = p
        a = jnp.where(causal, att[p], 0.0).astype(BF16)
        o_ref[b, :, h * GLA_DV:(h + 1) * GLA_DV] = inter[p] + _dot(a, ops[p][5])
        st_ref[p] = st_ref[p] * ops[p][4] + upd[p]


def _gla(gq, gk, gv, la, lf, batch, seq_len):
    nc = seq_len // BLOCK
    chunk = lambda c: (c + nc - 1) % nc
    blk = lambda width: pl.BlockSpec((batch, BLOCK, width), lambda c: (0, chunk(c), 0))
    r3 = lambda a: a.reshape(batch, seq_len, a.shape[-1])
    return pl.pallas_call(
        _gla_kernel,
        grid=(nc,),
        in_specs=[blk(GLA_KWIDTH), blk(GLA_KWIDTH), blk(GLA_WIDTH), blk(GLA_KWIDTH), blk(LANES)],
        out_specs=[blk(GLA_WIDTH), blk(LANES),
                   pl.BlockSpec((batch, FOX_HEADS, BLOCK), lambda c: (0, 0, chunk(c)))],
        out_shape=[jax.ShapeDtypeStruct((batch, seq_len, GLA_WIDTH), F32),
                   jax.ShapeDtypeStruct((batch, seq_len, LANES), F32),
                   jax.ShapeDtypeStruct((batch, FOX_HEADS, seq_len), F32)],
        scratch_shapes=[pltpu.VMEM((batch, GLA_HEADS, GLA_DV, LANES), F32),
                        pltpu.VMEM((batch, 1, LANES), F32)],
        compiler_params=pltpu.CompilerParams(
            dimension_semantics=("arbitrary",), vmem_limit_bytes=VMEM_LIMIT),
    )(r3(gq), r3(gk), r3(gv), r3(la), r3(lf))


def _fox_kernel(q_ref, k_ref, v_ref, cc_ref, crp_ref, crm_ref, cb_ref, cs_ref, o_ref,
                va_ref, vb_ref, ra_ref, rb_ref, ub_ref, plan_ref, m_ref, l_ref, acc_ref):
    t = FOX_T
    nt = crm_ref.shape[2]
    s_main = nt * t
    pre = slice(s_main, s_main + BLOCK)
    pair = pl.program_id(1)

    def stack_heads(x, lane_first):
        return jnp.concatenate([jnp.where(lane_first, x, 0.0), jnp.where(lane_first, 0.0, x)], axis=0)

    def head_cols(cc, lane):
        return jnp.concatenate(
            [jnp.sum(jnp.where(lane == 2 * pair + hh, cc, 0.0), axis=1, keepdims=True)
             for hh in range(2)], axis=0)

    sel = lax.broadcasted_iota(jnp.int32, (8, LANES), 0) == lax.broadcasted_iota(
        jnp.int32, (8, LANES), 1) // FOX_HEAD_DIM
    sel = sel.astype(BF16)

    def max_sq_norm(x):
        return jnp.max(_dot_nt(sel, x * x), axis=1, keepdims=True)

    lane_b = lax.broadcasted_iota(jnp.int32, (BLOCK, LANES), 1)
    first_b = lane_b < FOX_HEAD_DIM
    kp = k_ref[0, pre, :]
    vp = v_ref[0, pre, :]
    ckp = crp_ref[0, 0]
    row_b = lax.broadcasted_iota(jnp.int32, (BLOCK, BLOCK), 0)
    ok_b = (lane_b <= row_b) & ((lane_b >= N_PAD) | (lane_b == row_b))
    ok_b = jnp.concatenate([ok_b, ok_b], axis=0)
    s = _dot_nt(stack_heads(q_ref[0, pre, :], first_b), kp)
    z = jnp.concatenate([s[:BLOCK] - ckp[0:1], s[BLOCK:] - ckp[1:2]], axis=0)
    z = jnp.where(ok_b, z, NEG)
    cq2 = head_cols(cc_ref[0, pre, :], lane_b)
    m = jnp.max(z, axis=1, keepdims=True) + cq2
    p = jnp.exp2(z - (m - cq2))
    out = _dot(p.astype(BF16), vp) / jnp.sum(p, axis=1, keepdims=True)
    o_ref[0, pre, :] = jnp.where(first_b, out[:BLOCK], out[BLOCK:]).astype(o_ref.dtype)

    lane_s = lax.broadcasted_iota(jnp.int32, v_ref.shape[1:], 1)
    v_all = v_ref[0]
    va_ref[...] = jnp.where(lane_s < FOX_HEAD_DIM, v_all, jnp.where(lane_s == FOX_HEAD_DIM, 1.0, 0.0).astype(BF16))
    vb_ref[...] = jnp.where(lane_s >= FOX_HEAD_DIM, v_all, jnp.where(lane_s == 0, 1.0, 0.0).astype(BF16))

    k_max = jnp.sqrt(max_sq_norm(k_ref[0]))
    lane = lax.broadcasted_iota(jnp.int32, (t, LANES), 1)
    first = lane < FOX_HEAD_DIM
    row_l = lax.broadcasted_iota(jnp.int32, (LANES, LANES), 0)
    row_w = lax.broadcasted_iota(jnp.int32, (LANES, 2 * LANES), 0)
    second = lax.broadcasted_iota(jnp.int32, (LANES, 2 * LANES), 1) >= LANES
    ind_ab = ((row_w >= FOX_HEAD_DIM) == second).astype(BF16)
    hot_ab = (row_w == 2 * pair + second.astype(jnp.int32)).astype(BF16)
    hot_la = (row_l == FOX_HEAD_DIM).astype(BF16)
    hot_lb = (row_l == 0).astype(BF16)
    reps = t // LANES

    def prepare(i, carry):
        rows = pl.ds(pl.multiple_of(i * t, t), t)
        q = q_ref[0, rows, :]
        nq = _dot(q * q, ind_ab)
        cq = _dot(cc_ref[0, rows, :].astype(BF16), hot_ab)
        ub_a = jnp.sqrt(nq[:, :LANES]) * (k_max[0:1] * NORM_UP)
        ub_b = jnp.sqrt(nq[:, LANES:]) * (k_max[1:2] * NORM_UP)
        ra_ref[rows, :] = ub_a - cq[:, :LANES]
        rb_ref[rows, :] = ub_b - cq[:, LANES:]
        ub_ref[pl.ds(i, 1), :] = jnp.max(jnp.maximum(ub_a, ub_b), axis=0, keepdims=True)
        return carry

    lax.fori_loop(0, nt, prepare, 0)

    ub_t = ub_ref[0:nt, :]
    fast_v = ub_t <= FAST_MAX_BITS
    bound = jnp.where(fast_v, PRUNE_BITS, PRUNE_BITS + 2.0 * ub_t)
    cb = cb_ref[0, 0]
    dead = ((cs_ref[0, 0, 0, 0:nt, :] - cb[0:1]) < -bound) & ((cs_ref[0, 0, 1, 0:nt, :] - cb[1:2]) < -bound)
    tile_q = lax.broadcasted_iota(jnp.int32, (nt, LANES), 0)
    tile_k = lax.broadcasted_iota(jnp.int32, (nt, LANES), 1)
    n_skip_v = jnp.sum(jnp.where(dead & (tile_k < tile_q), 1, 0), axis=1, keepdims=True)
    skip_pre_v = jnp.sum(jnp.where(dead & (tile_k == nt), 1, 0), axis=1, keepdims=True)
    fast_i = jnp.where(fast_v[:, 0:1], 1, 0)
    tile_1 = tile_q[:, 0:1]
    for i in range(nt):
        for row, vec in enumerate((fast_i, n_skip_v, skip_pre_v)):
            plan_ref[row, i] = jnp.sum(jnp.where(tile_1 == i, vec, 0))

    def main_tile(i, carry):
        r0 = pl.multiple_of(i * t, t)
        q2 = stack_heads(q_ref[0, pl.ds(r0, t), :], first)
        fast = plan_ref[0, i] == 1
        n_skip = plan_ref[1, i]
        skip_prefix = plan_ref[2, i]

        def logits(kt, ck, offs):
            s = _dot_nt(q2, kt)
            return jnp.concatenate([s[:t] - ck[0:1] - offs[:t], s[t:] - ck[1:2] - offs[t:]], axis=0)

        def run(update, offs_p, offs_m, fuse_tail):
            acc_ref[...] = jnp.zeros_like(acc_ref)

            @pl.when(skip_prefix == 0)
            def _():
                z = logits(kp, ckp, offs_p)
                lane_k = lax.broadcasted_iota(jnp.int32, (2 * t, BLOCK), 1)
                update([jnp.where(lane_k >= N_PAD, z, NEG)], [pre])

            def tiles(js, diagonal):
                rows = [pl.ds(pl.multiple_of(j * t, t), t) for j in js]
                zs = [logits(k_ref[0, r, :], crm_ref[0, 0, j], offs_m) for j, r in zip(js, rows)]
                if diagonal:
                    ok = lax.broadcasted_iota(jnp.int32, (t, t), 1) <= lax.broadcasted_iota(jnp.int32, (t, t), 0)
                    zs[-1] = jnp.where(jnp.concatenate([ok, ok], axis=0), zs[-1], NEG)
                update(zs, rows)

            def body(j, c):
                tiles([j], False)
                return c

            if not fuse_tail:
                lax.fori_loop(n_skip, i, body, 0)
                tiles([i], True)
                return
            n_live = i - n_skip
            lax.fori_loop(n_skip, jnp.maximum(i - 2, n_skip), body, 0)

            @pl.when(n_live >= 2)
            def _():
                tiles([i - 2, i - 1, i], True)

            @pl.when(n_live == 1)
            def _():
                tiles([i - 1, i], True)

            @pl.when(n_live == 0)
            def _():
                tiles([i], True)

        def finish(l_a, l_b):
            acc = acc_ref[...]
            out = jnp.where(first, acc[:t] / l_a, acc[t:] / l_b)
            o_ref[0, pl.ds(r0, t), :] = out.astype(o_ref.dtype)

        @pl.when(fast)
        def _():
            ref = jnp.concatenate([ra_ref[pl.ds(r0, t), :], rb_ref[pl.ds(r0, t), :]], axis=0)

            def update(zs, rows):
                ps = [jnp.exp2(z).astype(BF16) for z in zs]
                for half, v_aug in ((slice(0, t), va_ref), (slice(t, 2 * t), vb_ref)):
                    pv = _dot(ps[0][half], v_aug[rows[0], :])
                    for p, r in zip(ps[1:], rows[1:]):
                        pv = pv + _dot(p[half], v_aug[r, :])
                    acc_ref[half] += pv

            run(update, ref, jnp.concatenate([ref] * reps, axis=1), True)

            def denominators(acc, hot):
                hi = acc.astype(BF16)
                lo = (acc - hi.astype(F32)).astype(BF16)
                return _dot(hi, hot) + _dot(lo, hot)

            finish(denominators(acc_ref[0:t], hot_la), denominators(acc_ref[t:], hot_lb))

        @pl.when(jnp.logical_not(fast))
        def _():
            cq2 = head_cols(cc_ref[0, pl.ds(r0, t), :], lane)
            m_ref[...] = jnp.full_like(m_ref, NEG)
            l_ref[...] = jnp.zeros_like(l_ref)

            def update(zs, rows):
                (z,), (r,) = zs, rows
                m_old = m_ref[...]
                m_new = jnp.maximum(m_old, jnp.max(z, axis=1, keepdims=True) + cq2)
                p = jnp.exp2(z - (m_new - cq2))
                alpha = jnp.exp2(m_old - m_new)
                l_ref[...] = alpha * l_ref[...] + jnp.sum(p, axis=1, keepdims=True)
                acc_ref[...] = alpha * acc_ref[...] + _dot(p.astype(BF16), v_ref[0, r, :])
                m_ref[...] = m_new

            run(update, jnp.zeros((2 * t, 1), F32), jnp.zeros((2 * t, 1), F32), False)
            finish(l_ref[0:t], l_ref[t:])

        return carry

    lax.fori_loop(0, nt, main_tile, 0)


def _fox(fq, fk, fv, cum_col, cum_row, batch, seq_len):
    t = FOX_T
    nt = (seq_len - BLOCK) // t
    npairs = FOX_HEADS // 2
    assert nt + 1 <= LANES
    s_main = nt * t
    r3 = lambda a: a.reshape(batch, seq_len, a.shape[-1])
    cr = cum_row.reshape(batch, npairs, 2, seq_len)
    cr_prefix = cr[..., s_main:]
    cr_main = cr[..., :s_main].reshape(batch, npairs, 2, nt, t).transpose(0, 1, 3, 2, 4)
    bounds = jnp.concatenate(
        [cr[..., t - 1:s_main:t], cr[..., seq_len - 1:],
         jnp.zeros((batch, npairs, 2, LANES - nt - 1), F32)], axis=-1)
    nt8 = -(-nt // 8) * 8
    starts = jnp.pad(cr[..., 0:s_main:t], ((0, 0), (0, 0), (0, 0), (0, nt8 - nt)))
    starts = jnp.broadcast_to(starts[..., None], (batch, npairs, 2, nt8, LANES))
    whole = pl.BlockSpec((1, seq_len, LANES), lambda b, p: (b, 0, p))
    out = pl.pallas_call(
        _fox_kernel,
        grid=(batch, npairs),
        in_specs=[whole, whole, whole,
                  pl.BlockSpec((1, seq_len, LANES), lambda b, p: (b, 0, 0)),
                  pl.BlockSpec((1, 1, 2, BLOCK), lambda b, p: (b, p, 0, 0)),
                  pl.BlockSpec((1, 1, nt, 2, t), lambda b, p: (b, p, 0, 0, 0)),
                  pl.BlockSpec((1, 1, 2, LANES), lambda b, p: (b, p, 0, 0)),
                  pl.BlockSpec((1, 1, 2, nt8, LANES), lambda b, p: (b, p, 0, 0, 0))],
        out_specs=whole,
        out_shape=jax.ShapeDtypeStruct((batch, seq_len, FOX_WIDTH), BF16),
        scratch_shapes=[pltpu.VMEM((seq_len, LANES), BF16), pltpu.VMEM((seq_len, LANES), BF16),
                        pltpu.VMEM((s_main, LANES), F32), pltpu.VMEM((s_main, LANES), F32),
                        pltpu.VMEM((nt8, LANES), F32),
                        pltpu.SMEM((3, nt), jnp.int32),
                        pltpu.VMEM((2 * t, 1), F32), pltpu.VMEM((2 * t, 1), F32),
                        pltpu.VMEM((2 * t, LANES), F32)],
        compiler_params=pltpu.CompilerParams(
            dimension_semantics=("parallel", "parallel"), vmem_limit_bytes=VMEM_LIMIT),
    )(r3(fq), r3(fk), r3(fv), cum_col, cr_prefix, cr_main, bounds, starts)
    return out.reshape(batch * seq_len, FOX_WIDTH)


def _post_kernel(of_ref, og_ref, gr_ref, gma_ref, gmb_ref, h_ref, ggla_ref, wfo_ref, wgo_ref,
                 wout_ref, gpost_ref, o_ref):
    tm = h_ref.shape[0]
    halves = (slice(0, tm // 2), slice(tm // 2, tm))
    ggla = ggla_ref[...]
    a = [_dot(of_ref[hs, :], wfo_ref[...]) for hs in halves]
    b = []
    for hs in halves:
        og = og_ref[hs, :]
        heads = [_rms(og[:, hh * GLA_DV:(hh + 1) * GLA_DV], ggla[:, hh * GLA_DV:(hh + 1) * GLA_DV])
                 for hh in range(GLA_HEADS)]
        gr = gr_ref[hs, :].astype(F32)
        o_gla = (jnp.concatenate(heads, axis=1) * (gr * jax.nn.sigmoid(gr))).astype(BF16)
        b.append(_dot(o_gla, wgo_ref[...]))
    mix = []
    for hs, a_h, b_h in zip(halves, a, b):
        y = jax.nn.sigmoid(gma_ref[hs, :].astype(F32)) * a_h + jax.nn.sigmoid(gmb_ref[hs, :].astype(F32)) * b_h
        mix.append(_dot(y.astype(BF16), wout_ref[...]))
    for hs, m_h in zip(halves, mix):
        o_ref[hs, :] = h_ref[hs, :] + _rms(m_h, gpost_ref[...])


def _post(o_fox, o_gla, gr, gma, gmb, h, ggla, wfo, wgo, wout, gpost, layer):
    n = h.shape[0]
    tm = ROW_TM
    row = lambda a: pl.BlockSpec((tm, a.shape[1]), lambda i: (i, 0))
    full = lambda a: (_resident(a.shape[1:], layer) if a.ndim == 3
                      else pl.BlockSpec(a.shape, lambda i: (0, 0)))
    args = (o_fox, o_gla, gr, gma, gmb, h, ggla, wfo, wgo, wout, gpost)
    return pl.pallas_call(
        _post_kernel,
        grid=(n // tm,),
        in_specs=[row(a) for a in args[:6]] + [full(a) for a in args[6:]],
        out_specs=pl.BlockSpec((tm, D_MODEL), lambda i: (i, 0)),
        out_shape=jax.ShapeDtypeStruct((n, D_MODEL), F32),
        compiler_params=pltpu.CompilerParams(
            dimension_semantics=("parallel",), vmem_limit_bytes=VMEM_LIMIT),
    )(*args)


def kernel(x, meta_tokens, w_in, w_alpha_up, b_alpha, b_f, g_gla_out, w_fox_o, w_gla_o, w_out,
           w_ffn1_gu, w_ffn1_down, w_ffn2_gu, w_ffn2_down,
           g_pre_ffn1, g_post_ffn1, g_pre_mix, g_post_mix, g_pre_ffn2, g_post_ffn2):
    batch, seq, d = x.shape
    seq_len = seq + BLOCK
    depth = w_in.shape[0]
    assert d == D_MODEL and (seq_len - BLOCK) % FOX_T == 0 and seq_len % ROW_TM == 0
    assert seq_len % FFN_TM == 0 and seq % FFN_TM_MAIN == 0

    pad = jnp.zeros((batch, N_PAD, d), x.dtype)
    meta = jnp.broadcast_to(meta_tokens.astype(x.dtype)[None], (batch, N_META, d))
    prefix = jnp.concatenate([pad, meta], axis=1)
    flat = lambda a: a.reshape(batch * seq_len, a.shape[-1])

    row2 = lambda a: a.reshape(1, -1).astype(F32)
    w_perm = _regroup_w_in(w_in)
    w1_gu, w1_down = w_ffn1_gu.astype(BF16), w_ffn1_down.astype(BF16)
    w2_gu, w2_down = w_ffn2_gu.astype(BF16), w_ffn2_down.astype(BF16)
    wfo, wgo, wout = w_fox_o.astype(BF16), w_gla_o.astype(BF16), w_out.astype(BF16)
    for l in range(depth):
        wa_pad = jnp.zeros((LANES, GLA_KWIDTH), F32).at[
            FOX_HEADS:FOX_HEADS + GLA_GATE_RANK].set(w_alpha_up[l]).astype(BF16)
        bf_pad = jnp.zeros((1, LANES), F32).at[0, :FOX_HEADS].set(b_f[l])

        if l == 0:
            h = _ffn_first(x, prefix, row2(g_pre_ffn1[l]), w1_gu, w1_down, row2(g_post_ffn1[l]), l)
        else:
            h = _ffn(h, row2(g_pre_ffn1[l]), w1_gu, w1_down, row2(g_post_ffn1[l]), l)
        fq, fk, fv, gq, gk, gv, gr, gma, gmb, la, lf = _proj(
            flat(h), row2(g_pre_mix[l]), w_perm, wa_pad, row2(b_alpha[l]), bf_pad, seq_len, l)
        o_gla, cum_col, cum_row = _gla(gq, gk, gv, la, lf, batch, seq_len)
        o_fox = _fox(fq, fk, fv, cum_col, cum_row, batch, seq_len)
        h = _post(o_fox, flat(o_gla), gr, gma, gmb, flat(h),
                  row2(g_gla_out[l]), wfo, wgo, wout, row2(g_post_mix[l]), l).reshape(batch, seq_len, d)
        h = _ffn(h, row2(g_pre_ffn2[l]), w2_gu, w2_down, row2(g_post_ffn2[l]), l,
                 main_rows=seq if l == depth - 1 else None)
    return h
```

```python
import functools

import jax
import jax.numpy as jnp
import numpy as np
from jax import lax
from jax.experimental import pallas as pl
from jax.experimental.pallas import tpu as pltpu

D_MODEL = 1024
N_META = 16
BLOCK = 128
N_PAD = BLOCK - N_META
FOX_HEADS = 8
FOX_HEAD_DIM = 64
FOX_WIDTH = FOX_HEADS * FOX_HEAD_DIM
GLA_HEADS = 4
GLA_WIDTH = 512
GLA_DV = 128
GLA_DK = 64
GLA_KWIDTH = GLA_HEADS * GLA_DK
GLA_GATE_RANK = 16
GLA_GATE_TEMP = 16.0
D_FF = 2816
RMS_EPS = 1e-6

LANES = 128
SUB = 16
N_SUB = BLOCK // SUB
EXP_CAP = 80.0
NEG = -1e30

FFN_TM = 640
FFN_TM_MAIN = 512
FFN_TF = 256
ROW_TM = 640
FOX_T = 512
LOG2E = 1.4426950408889634
PRUNE_BITS = 160.0
NORM_UP = 1.01
FAST_MAX_BITS = 50.0
VMEM_LIMIT = 56 * 1024 * 1024

BF16 = jnp.bfloat16
F32 = jnp.float32

C_FQ, C_FK, C_FV = 0, 512, 1024
C_GQ, C_GK, C_GV = 1536, 1792, 2048
C_GR, C_GMA, C_GMB = 2560, 3072, 4096
C_SMALL = 5120
N_IN = 5144
N_IN_PERM = 5248
W_IN_MOVES = ((0, 0, 1536), (1536, 1544, 1024), (2560, 2584, 2560), (5120, 1536, 8), (5128, 2568, 16))


def _dot(a, b):
    return jnp.dot(a, b, preferred_element_type=F32)


def _dot_nt(a, b):
    return lax.dot_general(a, b, (((1,), (1,)), ((), ())), preferred_element_type=F32)


def _dot_tn(a, b):
    return lax.dot_general(a, b, (((0,), (0,)), ((), ())), preferred_element_type=F32)


def _rms(x, g):
    return x * lax.rsqrt(jnp.mean(x * x, axis=-1, keepdims=True) + RMS_EPS) * g


def _log_sigmoid(x):
    return jnp.minimum(x, 0.0) - jnp.log1p(jnp.exp(-jnp.abs(x)))


def _split3(x):
    hi = x.astype(BF16)
    r1 = x - hi.astype(F32)
    mid = r1.astype(BF16)
    lo = (r1 - mid.astype(F32)).astype(BF16)
    return hi, mid, lo


def _dot_exact_lhs(m, parts):
    return _dot(m, parts[0]) + _dot(m, parts[1]) + _dot(m, parts[2])


def _ffn_rows(h, gpre_ref, wgu_ref, wd_ref, gpost_ref, xn_ref, a_ref):
    m = h.shape[0]
    parts = (slice(0, m // 2), slice(m // 2, m)) if m >= 2 * FFN_TF else (slice(0, m),)
    out = []
    for rows in parts:
        h_p = h[rows]
        xn_ref[rows] = _rms(h_p, gpre_ref[...]).astype(BF16)
        for c in range(D_FF // FFN_TF):
            lo = c * FFN_TF
            g = _dot(xn_ref[rows], wgu_ref[:, lo:lo + FFN_TF])
            u = _dot(xn_ref[rows], wgu_ref[:, D_FF + lo:D_FF + lo + FFN_TF])
            a_ref[rows, lo:lo + FFN_TF] = (g * jax.nn.sigmoid(g) * u).astype(BF16)
        y = _dot(a_ref[rows], wd_ref[...])
        out.append(h_p + 0.5 * _rms(y, gpost_ref[...]))
    return out[0] if len(out) == 1 else jnp.concatenate(out, axis=0)


def _ffn_kernel(h_ref, gpre_ref, wgu_ref, wd_ref, gpost_ref, o_ref, xn_ref, a_ref):
    o_ref[...] = _ffn_rows(h_ref[...], gpre_ref, wgu_ref, wd_ref, gpost_ref, xn_ref, a_ref)


def _ffn_first_kernel(x_ref, p_ref, gpre_ref, wgu_ref, wd_ref, gpost_ref, o_ref, xn_ref, a_ref):
    last = pl.program_id(1) == pl.num_programs(1) - 1

    @pl.when(jnp.logical_not(last))
    def _():
        o_ref[...] = _ffn_rows(x_ref[...], gpre_ref, wgu_ref, wd_ref, gpost_ref, xn_ref, a_ref)

    @pl.when(last)
    def _():
        o_ref[0:BLOCK] = _ffn_rows(p_ref[...], gpre_ref, wgu_ref, wd_ref, gpost_ref, xn_ref, a_ref)


def _resident(shape, layer):
    return pl.BlockSpec((None,) + shape, lambda *_: (layer,) + (0,) * len(shape),
                        pipeline_mode=pl.Buffered(1))


def _ffn(h, gpre, w_gu, w_down, gpost, layer, main_rows=None):
    batch, seq_len, _ = h.shape
    if main_rows is None:
        tm, rows = FFN_TM, seq_len
    else:
        tm, rows = FFN_TM_MAIN, main_rows
    tile = pl.BlockSpec((None, tm, D_MODEL), lambda b, i: (b, i, 0))
    vec = pl.BlockSpec((1, D_MODEL), lambda b, i: (0, 0))
    return pl.pallas_call(
        _ffn_kernel,
        grid=(batch, rows // tm),
        in_specs=[tile, vec, _resident((D_MODEL, 2 * D_FF), layer), _resident((D_FF, D_MODEL), layer), vec],
        out_specs=tile,
        out_shape=jax.ShapeDtypeStruct((batch, rows, D_MODEL), F32),
        scratch_shapes=[pltpu.VMEM((tm, D_MODEL), BF16), pltpu.VMEM((tm, D_FF), BF16)],
        compiler_params=pltpu.CompilerParams(
            dimension_semantics=("parallel", "parallel"), vmem_limit_bytes=VMEM_LIMIT),
    )(h, gpre, w_gu, w_down, gpost)


def _ffn_first(x, prefix, gpre, w_gu, w_down, gpost, layer):
    batch, seq, _ = x.shape
    tm = FFN_TM_MAIN
    n_main = seq // tm
    vec = pl.BlockSpec((1, D_MODEL), lambda b, i: (0, 0))
    return pl.pallas_call(
        _ffn_first_kernel,
        grid=(batch, n_main + 1),
        in_specs=[pl.BlockSpec((None, tm, D_MODEL), lambda b, i: (b, jnp.minimum(i, n_main - 1), 0)),
                  pl.BlockSpec((None, BLOCK, D_MODEL), lambda b, i: (b, 0, 0)),
                  vec, _resident((D_MODEL, 2 * D_FF), layer), _resident((D_FF, D_MODEL), layer), vec],
        out_specs=pl.BlockSpec((None, tm, D_MODEL), lambda b, i: (b, i, 0)),
        out_shape=jax.ShapeDtypeStruct((batch, seq + BLOCK, D_MODEL), F32),
        scratch_shapes=[pltpu.VMEM((tm, D_MODEL), BF16), pltpu.VMEM((tm, D_FF), BF16)],
        compiler_params=pltpu.CompilerParams(
            dimension_semantics=("parallel", "arbitrary"), vmem_limit_bytes=VMEM_LIMIT),
    )(x, prefix, gpre, w_gu, w_down, gpost)


def _proj_kernel(h_ref, g_ref, w_ref, wa_ref, ba_ref, bf_ref,
                 fq_ref, fk_ref, fv_ref, gq_ref, gk_ref, gv_ref, gr_ref, gma_ref, gmb_ref,
                 la_ref, lf_ref, ws_ref, *, tiles_per_seq, main_rows):
    @pl.when(pl.program_id(0) == 0)
    def _():
        for dst, src, width in W_IN_MOVES:
            ws_ref[:, dst:dst + width] = w_ref[:, src:src + width]
        ws_ref[:, N_IN:] = jnp.zeros((D_MODEL, N_IN_PERM - N_IN), BF16)

    w_ref = ws_ref
    tm = h_ref.shape[0]
    xn = _rms(h_ref[...], g_ref[...]).astype(BF16)
    pos = (pl.program_id(0) % tiles_per_seq) * tm + lax.broadcasted_iota(jnp.int32, (tm, 1), 0)
    valid = jnp.logical_or(pos < main_rows, pos >= main_rows + N_PAD).astype(F32)

    def proj(c0, width):
        return _dot(xn, w_ref[:, c0:c0 + width])

    fq_ref[...] = (proj(C_FQ, FOX_WIDTH) * (FOX_HEAD_DIM ** -0.5 * LOG2E)).astype(BF16)
    fk_ref[...] = proj(C_FK, FOX_WIDTH).astype(BF16)
    fv_ref[...] = proj(C_FV, FOX_WIDTH).astype(BF16)
    gq_ref[...] = proj(C_GQ, GLA_KWIDTH) * (GLA_DK ** -0.5)
    gk_ref[...] = proj(C_GK, GLA_KWIDTH) * valid
    gv_ref[...] = proj(C_GV, GLA_WIDTH).astype(BF16)
    gr_ref[...] = proj(C_GR, GLA_WIDTH).astype(BF16)
    gma_ref[...] = proj(C_GMA, D_MODEL).astype(BF16)
    gmb_ref[...] = proj(C_GMB, D_MODEL).astype(BF16)
    small = proj(C_SMALL, LANES)
    lane = lax.broadcasted_iota(jnp.int32, (tm, LANES), 1)
    lf = _log_sigmoid(small + bf_ref[...]) * (valid * LOG2E)
    lf_ref[...] = jnp.where(lane < FOX_HEADS, lf, 0.0)
    xa = _dot(small.astype(BF16), wa_ref[...]) + ba_ref[...]
    la_ref[...] = _log_sigmoid(xa) * (valid * (1.0 / GLA_GATE_TEMP))


def _proj(h, g, w_perm, wa_pad, ba, bf_pad, seq_len, layer):
    n = h.shape[0]
    tm = ROW_TM
    row = lambda width: pl.BlockSpec((tm, width), lambda i: (i, 0))
    full = lambda a: (_resident(a.shape[1:], layer) if a.ndim == 3
                      else pl.BlockSpec(a.shape, lambda i: (0, 0)))
    outs = [
        (FOX_WIDTH, BF16), (FOX_WIDTH, BF16), (FOX_WIDTH, BF16),
        (GLA_KWIDTH, F32), (GLA_KWIDTH, F32), (GLA_WIDTH, BF16),
        (GLA_WIDTH, BF16), (D_MODEL, BF16), (D_MODEL, BF16),
        (GLA_KWIDTH, F32), (LANES, F32),
    ]
    return pl.pallas_call(
        functools.partial(_proj_kernel, tiles_per_seq=seq_len // tm, main_rows=seq_len - BLOCK),
        grid=(n // tm,),
        in_specs=[row(D_MODEL), full(g), full(w_perm), full(wa_pad), full(ba), full(bf_pad)],
        out_specs=[row(w) for w, _ in outs],
        out_shape=[jax.ShapeDtypeStruct((n, w), dt) for w, dt in outs],
        scratch_shapes=[pltpu.VMEM((D_MODEL, N_IN_PERM), BF16)],
        compiler_params=pltpu.CompilerParams(
            dimension_semantics=("arbitrary",), vmem_limit_bytes=VMEM_LIMIT),
    )(h, g, w_perm, wa_pad, ba, bf_pad)


def _gla_kernel(q_ref, k_ref, v_ref, la_ref, lf_ref, o_ref, cc_ref, cr_ref, st_ref, carry_ref):
    @pl.when(pl.program_id(0) == 0)
    def _():
        st_ref[...] = jnp.zeros_like(st_ref)
        carry_ref[...] = jnp.zeros_like(carry_ref)

    row = lax.broadcasted_iota(jnp.int32, (BLOCK, BLOCK), 0)
    col = lax.broadcasted_iota(jnp.int32, (BLOCK, BLOCK), 1)
    causal = row >= col
    tril = causal.astype(BF16)
    tril_sub = (causal & ((row // SUB) == (col // SUB))).astype(BF16)
    row_w = lax.broadcasted_iota(jnp.int32, (BLOCK, GLA_KWIDTH), 0)
    row_blk = row // SUB

    batches = range(q_ref.shape[0])
    problems = [(b, h) for b in batches for h in range(GLA_HEADS)]

    cb_all, cw_all = [], []
    for b in batches:
        cl = _dot_exact_lhs(tril, _split3(lf_ref[b])) + carry_ref[b]
        carry_ref[b] = cl[BLOCK - 1:BLOCK, :]
        cc_ref[b] = cl
        cr_ref[b] = cl.T[0:FOX_HEADS, :]
        la_parts = _split3(la_ref[b])
        cb_all.append(_dot_exact_lhs(tril, la_parts))
        cw_all.append(_dot_exact_lhs(tril_sub, la_parts))

    ops = {}
    for b in batches:
        cb, cw = cb_all[b], cw_all[b]
        last = cb[BLOCK - 1:BLOCK, :]
        q = q_ref[b]
        k = k_ref[b]
        q_glob = (q * jnp.exp(cb)).astype(BF16)
        q_loc = (q * jnp.exp(cw)).astype(BF16)
        k_end = (k * jnp.exp(last - cb)).astype(BF16)
        decay = jnp.exp(last)
        k_sub = []
        for i in range(N_SUB):
            ref = cb[i * SUB - 1:i * SUB, :] if i > 0 else jnp.zeros_like(last)
            e = jnp.exp(jnp.minimum(ref - cb, EXP_CAP))
            k_sub.append(jnp.where(row_w < (i + 1) * SUB, k * e, 0.0).astype(BF16))
        for pair in range(GLA_HEADS // 2):
            sl = slice(pair * LANES, (pair + 1) * LANES)
            k_cat = jnp.concatenate([ks[:, sl] for ks in k_sub], axis=1)
            for half in range(2):
                h = 2 * pair + half
                head_lanes = (col < GLA_DK) if half == 0 else (col >= GLA_DK)
                q_h = jnp.where(head_lanes, q_loc[:, sl], 0.0)
                q_cat = jnp.concatenate(
                    [jnp.where(row_blk == i, q_h, 0.0) for i in range(N_SUB)], axis=1)
                ops[b, h] = (q_cat, k_cat, jnp.where(head_lanes, q_glob[:, sl], 0.0), k_end[:, sl],
                             decay[:, sl], v_ref[b, :, h * GLA_DV:(h + 1) * GLA_DV])

    att, inter, upd = {}, {}, {}
    for p in problems:
        q_cat, k_cat, q_g, k_e, _, v_h = ops[p]
        att[p] = _dot_nt(q_cat, k_cat)
        inter[p] = _dot_nt(q_g, st_ref[p].astype(BF16))
        upd[p] = _dot_tn(v_h, k_e)

    for p in problems:
        b, h = p
        a = jnp.where(causal, att[p], 0.0).astype(BF16)
        o_ref[b, :, h * GLA_DV:(h + 1) * GLA_DV] = inter[p] + _dot(a, ops[p][5])
        st_ref[p] = st_ref[p] * ops[p][4] + upd[p]


def _gla(gq, gk, gv, la, lf, batch, seq_len):
    nc = seq_len // BLOCK
    chunk = lambda c: (c + nc - 1) % nc
    blk = lambda width: pl.BlockSpec((batch, BLOCK, width), lambda c: (0, chunk(c), 0))
    r3 = lambda a: a.reshape(batch, seq_len, a.shape[-1])
    return pl.pallas_call(
        _gla_kernel,
        grid=(nc,),
        in_specs=[blk(GLA_KWIDTH), blk(GLA_KWIDTH), blk(GLA_WIDTH), blk(GLA_KWIDTH), blk(LANES)],
        out_specs=[blk(GLA_WIDTH), blk(LANES),
                   pl.BlockSpec((batch, FOX_HEADS, BLOCK), lambda c: (0, 0, chunk(c)))],
        out_shape=[jax.ShapeDtypeStruct((batch, seq_len, GLA_WIDTH), F32),
                   jax.ShapeDtypeStruct((batch, seq_len, LANES), F32),
                   jax.ShapeDtypeStruct((batch, FOX_HEADS, seq_len), F32)],
        scratch_shapes=[pltpu.VMEM((batch, GLA_HEADS, GLA_DV, LANES), F32),
                        pltpu.VMEM((batch, 1, LANES), F32)],
        compiler_params=pltpu.CompilerParams(
            dimension_semantics=("arbitrary",), vmem_limit_bytes=VMEM_LIMIT),
    )(r3(gq), r3(gk), r3(gv), r3(la), r3(lf))


def _fox_kernel(q_ref, k_ref, v_ref, cc_ref, crp_ref, crm_ref, cb_ref, cs_ref, o_ref,
                va_ref, vb_ref, ra_ref, rb_ref, ub_ref, plan_ref, m_ref, l_ref, acc_ref):
    t = FOX_T
    nt = crm_ref.shape[2]
    s_main = nt * t
    pre = slice(s_main, s_main + BLOCK)
    pair = pl.program_id(1)

    def stack_heads(x, lane_first):
        return jnp.concatenate([jnp.where(lane_first, x, 0.0), jnp.where(lane_first, 0.0, x)], axis=0)

    def head_cols(cc, lane):
        return jnp.concatenate(
            [jnp.sum(jnp.where(lane == 2 * pair + hh, cc, 0.0), axis=1, keepdims=True)
             for hh in range(2)], axis=0)

    sel = lax.broadcasted_iota(jnp.int32, (8, LANES), 0) == lax.broadcasted_iota(
        jnp.int32, (8, LANES), 1) // FOX_HEAD_DIM
    sel = sel.astype(BF16)

    def max_sq_norm(x):
        return jnp.max(_dot_nt(sel, x * x), axis=1, keepdims=True)

    lane_b = lax.broadcasted_iota(jnp.int32, (BLOCK, LANES), 1)
    first_b = lane_b < FOX_HEAD_DIM
    kp = k_ref[0, pre, :]
    vp = v_ref[0, pre, :]
    ckp = crp_ref[0, 0]
    row_b = lax.broadcasted_iota(jnp.int32, (BLOCK, BLOCK), 0)
    ok_b = (lane_b <= row_b) & ((lane_b >= N_PAD) | (lane_b == row_b))
    ok_b = jnp.concatenate([ok_b, ok_b], axis=0)
    s = _dot_nt(stack_heads(q_ref[0, pre, :], first_b), kp)
    z = jnp.concatenate([s[:BLOCK] - ckp[0:1], s[BLOCK:] - ckp[1:2]], axis=0)
    z = jnp.where(ok_b, z, NEG)
    cq2 = head_cols(cc_ref[0, pre, :], lane_b)
    m = jnp.max(z, axis=1, keepdims=True) + cq2
    p = jnp.exp2(z - (m - cq2))
    out = _dot(p.astype(BF16), vp) / jnp.sum(p, axis=1, keepdims=True)
    o_ref[0, pre, :] = jnp.where(first_b, out[:BLOCK], out[BLOCK:]).astype(o_ref.dtype)

    lane_s = lax.broadcasted_iota(jnp.int32, v_ref.shape[1:], 1)
    v_all = v_ref[0]
    va_ref[...] = jnp.where(lane_s < FOX_HEAD_DIM, v_all, jnp.where(lane_s == FOX_HEAD_DIM, 1.0, 0.0).astype(BF16))
    vb_ref[...] = jnp.where(lane_s >= FOX_HEAD_DIM, v_all, jnp.where(lane_s == 0, 1.0, 0.0).astype(BF16))

    k_max = jnp.sqrt(max_sq_norm(k_ref[0]))
    lane = lax.broadcasted_iota(jnp.int32, (t, LANES), 1)
    first = lane < FOX_HEAD_DIM
    row_l = lax.broadcasted_iota(jnp.int32, (LANES, LANES), 0)
    row_w = lax.broadcasted_iota(jnp.int32, (LANES, 2 * LANES), 0)
    second = lax.broadcasted_iota(jnp.int32, (LANES, 2 * LANES), 1) >= LANES
    ind_ab = ((row_w >= FOX_HEAD_DIM) == second).astype(BF16)
    hot_ab = (row_w == 2 * pair + second.astype(jnp.int32)).astype(BF16)
    hot_la = (row_l == FOX_HEAD_DIM).astype(BF16)
    hot_lb = (row_l == 0).astype(BF16)
    reps = t // LANES
    th = t // 2

    def prepare(i, carry):
        rows = pl.ds(pl.multiple_of(i * t, t), t)
        q = q_ref[0, rows, :]
        nq = _dot(q * q, ind_ab)
        cq = _dot(cc_ref[0, rows, :].astype(BF16), hot_ab)
        ub_a = jnp.sqrt(nq[:, :LANES]) * (k_max[0:1] * NORM_UP)
        ub_b = jnp.sqrt(nq[:, LANES:]) * (k_max[1:2] * NORM_UP)
        ra_ref[rows, :] = ub_a - cq[:, :LANES]
        rb_ref[rows, :] = ub_b - cq[:, LANES:]
        ub_ref[pl.ds(i, 1), :] = jnp.max(jnp.maximum(ub_a, ub_b), axis=0, keepdims=True)
        return carry

    lax.fori_loop(0, nt, prepare, 0)

    ub_t = ub_ref[0:nt, :]
    fast_v = ub_t <= FAST_MAX_BITS
    bound = jnp.where(fast_v, PRUNE_BITS, PRUNE_BITS + 2.0 * ub_t)
    cb = cb_ref[0, 0]
    dead = ((cs_ref[0, 0, 0, 0:nt, :] - cb[0:1]) < -bound) & ((cs_ref[0, 0, 1, 0:nt, :] - cb[1:2]) < -bound)
    tile_q = lax.broadcasted_iota(jnp.int32, (nt, LANES), 0)
    tile_k = lax.broadcasted_iota(jnp.int32, (nt, LANES), 1)
    n_skip_v = jnp.sum(jnp.where(dead & (tile_k < tile_q), 1, 0), axis=1, keepdims=True)
    skip_pre_v = jnp.sum(jnp.where(dead & (tile_k == nt), 1, 0), axis=1, keepdims=True)
    fast_i = jnp.where(fast_v[:, 0:1], 1, 0)
    tile_1 = tile_q[:, 0:1]
    for i in range(nt):
        for row, vec in enumerate((fast_i, n_skip_v, skip_pre_v)):
            plan_ref[row, i] = jnp.sum(jnp.where(tile_1 == i, vec, 0))

    def main_tile(i, carry):
        r0 = pl.multiple_of(i * t, t)
        q2 = stack_heads(q_ref[0, pl.ds(r0, t), :], first)
        fast = plan_ref[0, i] == 1
        n_skip = plan_ref[1, i]
        skip_prefix = plan_ref[2, i]

        def logits(kt, ck, offs):
            s = _dot_nt(q2, kt)
            return jnp.concatenate([s[:t] - ck[0:1] - offs[:t], s[t:] - ck[1:2] - offs[t:]], axis=0)

        def run(update, offs_p, offs_m, fuse_tail):
            acc_ref[...] = jnp.zeros_like(acc_ref)

            @pl.when(skip_prefix == 0)
            def _():
                z = logits(kp, ckp, offs_p)
                lane_k = lax.broadcasted_iota(jnp.int32, (2 * t, BLOCK), 1)
                update([jnp.where(lane_k >= N_PAD, z, NEG)], [pre])

            def tiles(js, diagonal):
                split = diagonal and fuse_tail
                full = js[:-1] if split else js
                rows = [pl.ds(pl.multiple_of(j * t, t), t) for j in full]
                zs = [logits(k_ref[0, r, :], crm_ref[0, 0, j], offs_m) for j, r in zip(full, rows)]
                if not diagonal:
                    update(zs, rows)
                    return
                if not split:
                    ok = lax.broadcasted_iota(jnp.int32, (t, t), 1) <= lax.broadcasted_iota(jnp.int32, (t, t), 0)
                    zs[-1] = jnp.where(jnp.concatenate([ok, ok], axis=0), zs[-1], NEG)
                    update(zs, rows)
                    return
                base = pl.multiple_of(js[-1] * t, t)
                ck = crm_ref[0, 0, js[-1]]
                tri = lax.broadcasted_iota(jnp.int32, (th, th), 1) <= lax.broadcasted_iota(jnp.int32, (th, th), 0)
                r_left = pl.ds(base, th)
                z_left = logits(k_ref[0, r_left, :], ck[:, :th], offs_m[:, :th])
                lower_rows = (lax.broadcasted_iota(jnp.int32, (2 * t, th), 0) // th) % 2 == 1
                z_left = jnp.where(lower_rows | jnp.concatenate([tri] * 4, axis=0), z_left, NEG)
                r_right = pl.ds(base + th, th)
                q_lo = jnp.concatenate([q2[th:t], q2[t + th:]], axis=0)
                o_lo = jnp.concatenate([offs_m[th:t, :th], offs_m[t + th:, :th]], axis=0)
                s = _dot_nt(q_lo, k_ref[0, r_right, :])
                z_right = jnp.concatenate([s[:th] - ck[0:1, th:] - o_lo[:th], s[th:] - ck[1:2, th:] - o_lo[th:]],
                                          axis=0)
                z_right = jnp.where(jnp.concatenate([tri, tri], axis=0), z_right, NEG)
                update(zs + [z_left], rows + [r_left], (z_right, r_right))

            def body(j, c):
                tiles([j], False)
                return c

            if not fuse_tail:
                lax.fori_loop(n_skip, i, body, 0)
                tiles([i], True)
                return
            n_live = i - n_skip
            lax.fori_loop(n_skip, jnp.maximum(i - 2, n_skip), body, 0)

            @pl.when(n_live >= 2)
            def _():
                tiles([i - 2, i - 1, i], True)

            @pl.when(n_live == 1)
            def _():
                tiles([i - 1, i], True)

            @pl.when(n_live == 0)
            def _():
                tiles([i], True)

        def finish(l_a, l_b):
            acc = acc_ref[...]
            out = jnp.where(first, acc[:t] / l_a, acc[t:] / l_b)
            o_ref[0, pl.ds(r0, t), :] = out.astype(o_ref.dtype)

        @pl.when(fast)
        def _():
            ref = jnp.concatenate([ra_ref[pl.ds(r0, t), :], rb_ref[pl.ds(r0, t), :]], axis=0)

            def update(zs, rows, lower=None):
                ps = [jnp.exp2(z).astype(BF16) for z in zs]
                p_lo = None if lower is None else jnp.exp2(lower[0]).astype(BF16)
                for half, v_aug in ((slice(0, t), va_ref), (slice(t, 2 * t), vb_ref)):
                    pv = _dot(ps[0][half], v_aug[rows[0], :])
                    for p, r in zip(ps[1:], rows[1:]):
                        pv = pv + _dot(p[half], v_aug[r, :])
                    acc_ref[half] += pv
                if lower is not None:
                    acc_ref[th:t] += _dot(p_lo[:th], va_ref[lower[1], :])
                    acc_ref[t + th:] += _dot(p_lo[th:], vb_ref[lower[1], :])

            run(update, ref, jnp.concatenate([ref] * reps, axis=1), True)

            def denominators(acc, hot):
                hi = acc.astype(BF16)
                lo = (acc - hi.astype(F32)).astype(BF16)
                return _dot(hi, hot) + _dot(lo, hot)

            finish(denominators(acc_ref[0:t], hot_la), denominators(acc_ref[t:], hot_lb))

        @pl.when(jnp.logical_not(fast))
        def _():
            cq2 = head_cols(cc_ref[0, pl.ds(r0, t), :], lane)
            m_ref[...] = jnp.full_like(m_ref, NEG)
            l_ref[...] = jnp.zeros_like(l_ref)

            def update(zs, rows):
                (z,), (r,) = zs, rows
                m_old = m_ref[...]
                m_new = jnp.maximum(m_old, jnp.max(z, axis=1, keepdims=True) + cq2)
                p = jnp.exp2(z - (m_new - cq2))
                alpha = jnp.exp2(m_old - m_new)
                l_ref[...] = alpha * l_ref[...] + jnp.sum(p, axis=1, keepdims=True)
                acc_ref[...] = alpha * acc_ref[...] + _dot(p.astype(BF16), v_ref[0, r, :])
                m_ref[...] = m_new

            run(update, jnp.zeros((2 * t, 1), F32), jnp.zeros((2 * t, 1), F32), False)
            finish(l_ref[0:t], l_ref[t:])

        return carry

    lax.fori_loop(0, nt, main_tile, 0)


def _fox(fq, fk, fv, cum_col, cum_row, batch, seq_len):
    t = FOX_T
    nt = (seq_len - BLOCK) // t
    npairs = FOX_HEADS // 2
    assert nt + 1 <= LANES
    s_main = nt * t
    r3 = lambda a: a.reshape(batch, seq_len, a.shape[-1])
    cr = cum_row.reshape(batch, npairs, 2, seq_len)
    cr_prefix = cr[..., s_main:]
    cr_main = cr[..., :s_main].reshape(batch, npairs, 2, nt, t).transpose(0, 1, 3, 2, 4)
    bounds = jnp.concatenate(
        [cr[..., t - 1:s_main:t], cr[..., seq_len - 1:],
         jnp.zeros((batch, npairs, 2, LANES - nt - 1), F32)], axis=-1)
    nt8 = -(-nt // 8) * 8
    starts = jnp.pad(cr[..., 0:s_main:t], ((0, 0), (0, 0), (0, 0), (0, nt8 - nt)))
    starts = jnp.broadcast_to(starts[..., None], (batch, npairs, 2, nt8, LANES))
    whole = pl.BlockSpec((1, seq_len, LANES), lambda b, p: (b, 0, p))
    out = pl.pallas_call(
        _fox_kernel,
        grid=(batch, npairs),
        in_specs=[whole, whole, whole,
                  pl.BlockSpec((1, seq_len, LANES), lambda b, p: (b, 0, 0)),
                  pl.BlockSpec((1, 1, 2, BLOCK), lambda b, p: (b, p, 0, 0)),
                  pl.BlockSpec((1, 1, nt, 2, t), lambda b, p: (b, p, 0, 0, 0)),
                  pl.BlockSpec((1, 1, 2, LANES), lambda b, p: (b, p, 0, 0)),
                  pl.BlockSpec((1, 1, 2, nt8, LANES), lambda b, p: (b, p, 0, 0, 0))],
        out_specs=whole,
        out_shape=jax.ShapeDtypeStruct((batch, seq_len, FOX_WIDTH), BF16),
        scratch_shapes=[pltpu.VMEM((seq_len, LANES), BF16), pltpu.VMEM((seq_len, LANES), BF16),
                        pltpu.VMEM((s_main, LANES), F32), pltpu.VMEM((s_main, LANES), F32),
                        pltpu.VMEM((nt8, LANES), F32),
                        pltpu.SMEM((3, nt), jnp.int32),
                        pltpu.VMEM((2 * t, 1), F32), pltpu.VMEM((2 * t, 1), F32),
                        pltpu.VMEM((2 * t, LANES), F32)],
        compiler_params=pltpu.CompilerParams(
            dimension_semantics=("parallel", "parallel"), vmem_limit_bytes=VMEM_LIMIT),
    )(r3(fq), r3(fk), r3(fv), cum_col, cr_prefix, cr_main, bounds, starts)
    return out.reshape(batch * seq_len, FOX_WIDTH)


def _post_kernel(of_ref, og_ref, gr_ref, gma_ref, gmb_ref, h_ref, ggla_ref, wfo_ref, wgo_ref,
                 wout_ref, gpost_ref, o_ref):
    tm = h_ref.shape[0]
    halves = (slice(0, tm // 2), slice(tm // 2, tm))
    ggla = ggla_ref[...]
    a = [_dot(of_ref[hs, :], wfo_ref[...]) for hs in halves]
    b = []
    for hs in halves:
        og = og_ref[hs, :]
        heads = [_rms(og[:, hh * GLA_DV:(hh + 1) * GLA_DV], ggla[:, hh * GLA_DV:(hh + 1) * GLA_DV])
                 for hh in range(GLA_HEADS)]
        gr = gr_ref[hs, :].astype(F32)
        o_gla = (jnp.concatenate(heads, axis=1) * (gr * jax.nn.sigmoid(gr))).astype(BF16)
        b.append(_dot(o_gla, wgo_ref[...]))
    mix = []
    for hs, a_h, b_h in zip(halves, a, b):
        y = jax.nn.sigmoid(gma_ref[hs, :].astype(F32)) * a_h + jax.nn.sigmoid(gmb_ref[hs, :].astype(F32)) * b_h
        mix.append(_dot(y.astype(BF16), wout_ref[...]))
    for hs, m_h in zip(halves, mix):
        o_ref[hs, :] = h_ref[hs, :] + _rms(m_h, gpost_ref[...])


def _post(o_fox, o_gla, gr, gma, gmb, h, ggla, wfo, wgo, wout, gpost, layer):
    n = h.shape[0]
    tm = ROW_TM
    row = lambda a: pl.BlockSpec((tm, a.shape[1]), lambda i: (i, 0))
    full = lambda a: (_resident(a.shape[1:], layer) if a.ndim == 3
                      else pl.BlockSpec(a.shape, lambda i: (0, 0)))
    args = (o_fox, o_gla, gr, gma, gmb, h, ggla, wfo, wgo, wout, gpost)
    return pl.pallas_call(
        _post_kernel,
        grid=(n // tm,),
        in_specs=[row(a) for a in args[:6]] + [full(a) for a in args[6:]],
        out_specs=pl.BlockSpec((tm, D_MODEL), lambda i: (i, 0)),
        out_shape=jax.ShapeDtypeStruct((n, D_MODEL), F32),
        compiler_params=pltpu.CompilerParams(
            dimension_semantics=("parallel",), vmem_limit_bytes=VMEM_LIMIT),
    )(*args)


def kernel(x, meta_tokens, w_in, w_alpha_up, b_alpha, b_f, g_gla_out, w_fox_o, w_gla_o, w_out,
           w_ffn1_gu, w_ffn1_down, w_ffn2_gu, w_ffn2_down,
           g_pre_ffn1, g_post_ffn1, g_pre_mix, g_post_mix, g_pre_ffn2, g_post_ffn2):
    batch, seq, d = x.shape
    seq_len = seq + BLOCK
    depth = w_in.shape[0]
    assert d == D_MODEL and (seq_len - BLOCK) % FOX_T == 0 and seq_len % ROW_TM == 0
    assert seq_len % FFN_TM == 0 and seq % FFN_TM_MAIN == 0

    pad = jnp.zeros((batch, N_PAD, d), x.dtype)
    meta = jnp.broadcast_to(meta_tokens.astype(x.dtype)[None], (batch, N_META, d))
    prefix = jnp.concatenate([pad, meta], axis=1)
    flat = lambda a: a.reshape(batch * seq_len, a.shape[-1])

    row2 = lambda a: a.reshape(1, -1).astype(F32)
    w_perm = w_in.astype(BF16)
    w1_gu, w1_down = w_ffn1_gu.astype(BF16), w_ffn1_down.astype(BF16)
    w2_gu, w2_down = w_ffn2_gu.astype(BF16), w_ffn2_down.astype(BF16)
    wfo, wgo, wout = w_fox_o.astype(BF16), w_gla_o.astype(BF16), w_out.astype(BF16)
    for l in range(depth):
        wa_pad = jnp.zeros((LANES, GLA_KWIDTH), F32).at[
            FOX_HEADS:FOX_HEADS + GLA_GATE_RANK].set(w_alpha_up[l]).astype(BF16)
        bf_pad = jnp.zeros((1, LANES), F32).at[0, :FOX_HEADS].set(b_f[l])

        if l == 0:
            h = _ffn_first(x, prefix, row2(g_pre_ffn1[l]), w1_gu, w1_down, row2(g_post_ffn1[l]), l)
        else:
            h = _ffn(h, row2(g_pre_ffn1[l]), w1_gu, w1_down, row2(g_post_ffn1[l]), l)
        fq, fk, fv, gq, gk, gv, gr, gma, gmb, la, lf = _proj(
            flat(h), row2(g_pre_mix[l]), w_perm, wa_pad, row2(b_alpha[l]), bf_pad, seq_len, l)
        o_gla, cum_col, cum_row = _gla(gq, gk, gv, la, lf, batch, seq_len)
        o_fox = _fox(fq, fk, fv, cum_col, cum_row, batch, seq_len)
        h = _post(o_fox, flat(o_gla), gr, gma, gmb, flat(h),
                  row2(g_gla_out[l]), wfo, wgo, wout, row2(g_post_mix[l]), l).reshape(batch, seq_len, d)
        h = _ffn(h, row2(g_pre_ffn2[l]), w2_gu, w2_down, row2(g_post_ffn2[l]), l,
                 main_rows=seq if l == depth - 1 else None)
    return h
```

```python
import functools

import jax
import jax.numpy as jnp
import numpy as np
from jax import lax
from jax.experimental import pallas as pl
from jax.experimental.pallas import tpu as pltpu

D_MODEL = 1024
N_META = 16
BLOCK = 128
N_PAD = BLOCK - N_META
FOX_HEADS = 8
FOX_HEAD_DIM = 64
FOX_WIDTH = FOX_HEADS * FOX_HEAD_DIM
GLA_HEADS = 4
GLA_WIDTH = 512
GLA_DV = 128
GLA_DK = 64
GLA_KWIDTH = GLA_HEADS * GLA_DK
GLA_GATE_RANK = 16
GLA_GATE_TEMP = 16.0
D_FF = 2816
RMS_EPS = 1e-6

LANES = 128
SUB = 16
N_SUB = BLOCK // SUB
EXP_CAP = 80.0
NEG = -1e30

FFN_TM = 640
FFN_TM_MAIN = 512
FFN_TF = 256
FFN_W_ROWS = 256
ROW_TM = 640
FOX_T = 512
LOG2E = 1.4426950408889634
PRUNE_BITS = 160.0
NORM_UP = 1.01
FAST_MAX_BITS = 50.0
VMEM_LIMIT = 56 * 1024 * 1024

BF16 = jnp.bfloat16
F32 = jnp.float32

C_FQ, C_FK, C_FV = 0, 512, 1024
C_GQ, C_GK, C_GV = 1536, 1792, 2048
C_GR, C_GMA, C_GMB = 2560, 3072, 4096
C_SMALL = 5120
N_IN = 5144
N_IN_PERM = 5248
W_IN_MOVES = ((0, 0, 1536), (1536, 1544, 1024), (2560, 2584, 2560), (5120, 1536, 8), (5128, 2568, 16))


def _dot(a, b):
    return jnp.dot(a, b, preferred_element_type=F32)


def _dot_nt(a, b):
    return lax.dot_general(a, b, (((1,), (1,)), ((), ())), preferred_element_type=F32)


def _dot_tn(a, b):
    return lax.dot_general(a, b, (((0,), (0,)), ((), ())), preferred_element_type=F32)


def _rms(x, g):
    return x * lax.rsqrt(jnp.mean(x * x, axis=-1, keepdims=True) + RMS_EPS) * g


def _log_sigmoid(x):
    return jnp.minimum(x, 0.0) - jnp.log1p(jnp.exp(-jnp.abs(x)))


def _split3(x):
    hi = x.astype(BF16)
    r1 = x - hi.astype(F32)
    mid = r1.astype(BF16)
    lo = (r1 - mid.astype(F32)).astype(BF16)
    return hi, mid, lo


def _dot_exact_lhs(m, parts):
    return _dot(m, parts[0]) + _dot(m, parts[1]) + _dot(m, parts[2])


def _ffn_rows(h, gpre_ref, wgu_ref, wd_ref, gpost_ref, xn_ref, a_ref):
    m = h.shape[0]
    parts = (slice(0, m // 2), slice(m // 2, m)) if m >= 2 * FFN_TF else (slice(0, m),)
    out = []
    for rows in parts:
        h_p = h[rows]
        xn_ref[rows] = _rms(h_p, gpre_ref[...]).astype(BF16)
        for c in range(D_FF // FFN_TF):
            lo = c * FFN_TF
            g = _dot(xn_ref[rows], wgu_ref[:, lo:lo + FFN_TF])
            u = _dot(xn_ref[rows], wgu_ref[:, D_FF + lo:D_FF + lo + FFN_TF])
            a_ref[rows, lo:lo + FFN_TF] = (g * jax.nn.sigmoid(g) * u).astype(BF16)
        y = _dot(a_ref[rows], wd_ref[...])
        out.append(h_p + 0.5 * _rms(y, gpost_ref[...]))
    return out[0] if len(out) == 1 else jnp.concatenate(out, axis=0)


def _stream_weight(w_hbm, layer, n_rows, stage_ref, sem, sem_row, consume):
    rows = stage_ref.shape[1]
    n = n_rows // rows

    def chunk(c):
        return pltpu.make_async_copy(w_hbm.at[layer, pl.ds(c * rows, rows), :], stage_ref.at[c % 2],
                                     sem.at[sem_row, c % 2])

    chunk(0).start()
    for c in range(n):
        if c + 1 < n:
            chunk(c + 1).start()
        chunk(c).wait()
        consume(slice(c * rows, (c + 1) * rows), stage_ref.at[c % 2])


def _cast_into(dst_ref):
    def consume(rows, staged):
        dst_ref[rows, :] = staged[...].astype(BF16)
    return consume


def _ffn_weights(wgu_hbm, wd_hbm, layer, wgu_ref, wd_ref, sgu_ref, sd_ref, sem):
    @pl.when((pl.program_id(0) == 0) & (pl.program_id(1) == 0))
    def _():
        _stream_weight(wgu_hbm, layer, D_MODEL, sgu_ref, sem, 0, _cast_into(wgu_ref))
        _stream_weight(wd_hbm, layer, D_FF, sd_ref, sem, 1, _cast_into(wd_ref))


def _ffn_kernel(h_ref, gpre_ref, wgu_hbm, wd_hbm, gpost_ref, o_ref,
                xn_ref, a_ref, wgu_ref, wd_ref, sgu_ref, sd_ref, sem, *, layer):
    _ffn_weights(wgu_hbm, wd_hbm, layer, wgu_ref, wd_ref, sgu_ref, sd_ref, sem)
    o_ref[...] = _ffn_rows(h_ref[...], gpre_ref, wgu_ref, wd_ref, gpost_ref, xn_ref, a_ref)


def _ffn_first_kernel(x_ref, p_ref, gpre_ref, wgu_hbm, wd_hbm, gpost_ref, o_ref,
                      xn_ref, a_ref, wgu_ref, wd_ref, sgu_ref, sd_ref, sem, *, layer):
    _ffn_weights(wgu_hbm, wd_hbm, layer, wgu_ref, wd_ref, sgu_ref, sd_ref, sem)
    last = pl.program_id(1) == pl.num_programs(1) - 1

    @pl.when(jnp.logical_not(last))
    def _():
        o_ref[...] = _ffn_rows(x_ref[...], gpre_ref, wgu_ref, wd_ref, gpost_ref, xn_ref, a_ref)

    @pl.when(last)
    def _():
        o_ref[0:BLOCK] = _ffn_rows(p_ref[...], gpre_ref, wgu_ref, wd_ref, gpost_ref, xn_ref, a_ref)


def _resident(shape, layer):
    return pl.BlockSpec((None,) + shape, lambda *_: (layer,) + (0,) * len(shape),
                        pipeline_mode=pl.Buffered(1))


def _ffn_scratch(tm):
    return [pltpu.VMEM((tm, D_MODEL), BF16), pltpu.VMEM((tm, D_FF), BF16),
            pltpu.VMEM((D_MODEL, 2 * D_FF), BF16), pltpu.VMEM((D_FF, D_MODEL), BF16),
            pltpu.VMEM((2, FFN_W_ROWS, 2 * D_FF), F32), pltpu.VMEM((2, FFN_W_ROWS, D_MODEL), F32),
            pltpu.SemaphoreType.DMA((2, 2))]


def _ffn(h, gpre, w_gu, w_down, gpost, layer, main_rows=None):
    batch, seq_len, _ = h.shape
    if main_rows is None:
        tm, rows = FFN_TM, seq_len
    else:
        tm, rows = FFN_TM_MAIN, main_rows
    tile = pl.BlockSpec((None, tm, D_MODEL), lambda b, i: (b, i, 0))
    vec = pl.BlockSpec((1, D_MODEL), lambda b, i: (0, 0))
    hbm = pl.BlockSpec(memory_space=pl.ANY)
    return pl.pallas_call(
        functools.partial(_ffn_kernel, layer=layer),
        grid=(batch, rows // tm),
        in_specs=[tile, vec, hbm, hbm, vec],
        out_specs=tile,
        out_shape=jax.ShapeDtypeStruct((batch, rows, D_MODEL), F32),
        scratch_shapes=_ffn_scratch(tm),
        compiler_params=pltpu.CompilerParams(
            dimension_semantics=("arbitrary", "arbitrary"), vmem_limit_bytes=VMEM_LIMIT),
    )(h, gpre, w_gu, w_down, gpost)


def _ffn_first(x, prefix, gpre, w_gu, w_down, gpost, layer):
    batch, seq, _ = x.shape
    tm = FFN_TM_MAIN
    n_main = seq // tm
    vec = pl.BlockSpec((1, D_MODEL), lambda b, i: (0, 0))
    hbm = pl.BlockSpec(memory_space=pl.ANY)
    return pl.pallas_call(
        functools.partial(_ffn_first_kernel, layer=layer),
        grid=(batch, n_main + 1),
        in_specs=[pl.BlockSpec((None, tm, D_MODEL), lambda b, i: (b, jnp.minimum(i, n_main - 1), 0)),
                  pl.BlockSpec((None, BLOCK, D_MODEL), lambda b, i: (b, 0, 0)),
                  vec, hbm, hbm, vec],
        out_specs=pl.BlockSpec((None, tm, D_MODEL), lambda b, i: (b, i, 0)),
        out_shape=jax.ShapeDtypeStruct((batch, seq + BLOCK, D_MODEL), F32),
        scratch_shapes=_ffn_scratch(tm),
        compiler_params=pltpu.CompilerParams(
            dimension_semantics=("arbitrary", "arbitrary"), vmem_limit_bytes=VMEM_LIMIT),
    )(x, prefix, gpre, w_gu, w_down, gpost)


def _proj_kernel(h_ref, g_ref, w_ref, wa_ref, ba_ref, bf_ref,
                 fq_ref, fk_ref, fv_ref, gq_ref, gk_ref, gv_ref, gr_ref, gma_ref, gmb_ref,
                 la_ref, lf_ref, ws_ref, *, tiles_per_seq, main_rows):
    @pl.when(pl.program_id(0) == 0)
    def _():
        for dst, src, width in W_IN_MOVES:
            ws_ref[:, dst:dst + width] = w_ref[:, src:src + width]
        ws_ref[:, N_IN:] = jnp.zeros((D_MODEL, N_IN_PERM - N_IN), BF16)

    w_ref = ws_ref
    tm = h_ref.shape[0]
    m = tm // 2
    halves = (slice(0, m), slice(m, tm))
    xns = {}
    for rows in halves:
        if rows.start not in xns:
            xns[rows.start] = _rms(h_ref[rows, :], g_ref[...]).astype(BF16)
        xn = xns[rows.start]
        pos = ((pl.program_id(0) % tiles_per_seq) * tm + rows.start
               + lax.broadcasted_iota(jnp.int32, (m, 1), 0))
        valid = jnp.logical_or(pos < main_rows, pos >= main_rows + N_PAD).astype(F32)

        def proj(c0, width):
            return _dot(xn, w_ref[:, c0:c0 + width])

        small = proj(C_SMALL, LANES)
        if rows.start == 0:
            xns[m] = _rms(h_ref[halves[1], :], g_ref[...]).astype(BF16)
        fq_ref[rows, :] = (proj(C_FQ, FOX_WIDTH) * (FOX_HEAD_DIM ** -0.5 * LOG2E)).astype(BF16)
        fk_ref[rows, :] = proj(C_FK, FOX_WIDTH).astype(BF16)
        lane = lax.broadcasted_iota(jnp.int32, (m, LANES), 1)
        lf = _log_sigmoid(small + bf_ref[...]) * (valid * LOG2E)
        lf_ref[rows, :] = jnp.where(lane < FOX_HEADS, lf, 0.0)
        xa = _dot(small.astype(BF16), wa_ref[...]) + ba_ref[...]
        la_ref[rows, :] = _log_sigmoid(xa) * (valid * (1.0 / GLA_GATE_TEMP))
        fv_ref[rows, :] = proj(C_FV, FOX_WIDTH).astype(BF16)
        gq_ref[rows, :] = proj(C_GQ, GLA_KWIDTH) * (GLA_DK ** -0.5)
        gk_ref[rows, :] = proj(C_GK, GLA_KWIDTH) * valid
        gv_ref[rows, :] = proj(C_GV, GLA_WIDTH).astype(BF16)
        gr_ref[rows, :] = proj(C_GR, GLA_WIDTH).astype(BF16)
        gma_ref[rows, :] = proj(C_GMA, D_MODEL).astype(BF16)
        gmb_ref[rows, :] = proj(C_GMB, D_MODEL).astype(BF16)


def _proj(h, g, w_perm, wa_pad, ba, bf_pad, seq_len, layer):
    n = h.shape[0]
    tm = ROW_TM
    row = lambda width: pl.BlockSpec((tm, width), lambda i: (i, 0))
    full = lambda a: (_resident(a.shape[1:], layer) if a.ndim == 3
                      else pl.BlockSpec(a.shape, lambda i: (0, 0)))
    outs = [
        (FOX_WIDTH, BF16), (FOX_WIDTH, BF16), (FOX_WIDTH, BF16),
        (GLA_KWIDTH, F32), (GLA_KWIDTH, F32), (GLA_WIDTH, BF16),
        (GLA_WIDTH, BF16), (D_MODEL, BF16), (D_MODEL, BF16),
        (GLA_KWIDTH, F32), (LANES, F32),
    ]
    return pl.pallas_call(
        functools.partial(_proj_kernel, tiles_per_seq=seq_len // tm, main_rows=seq_len - BLOCK),
        grid=(n // tm,),
        in_specs=[row(D_MODEL), full(g), full(w_perm), full(wa_pad), full(ba), full(bf_pad)],
        out_specs=[row(w) for w, _ in outs],
        out_shape=[jax.ShapeDtypeStruct((n, w), dt) for w, dt in outs],
        scratch_shapes=[pltpu.VMEM((D_MODEL, N_IN_PERM), BF16)],
        compiler_params=pltpu.CompilerParams(
            dimension_semantics=("arbitrary",), vmem_limit_bytes=VMEM_LIMIT),
    )(h, g, w_perm, wa_pad, ba, bf_pad)


def _gla_kernel(q_ref, k_ref, v_ref, la_ref, lf_ref, o_ref, cc_ref, cr_ref, st_ref, carry_ref):
    @pl.when(pl.program_id(0) == 0)
    def _():
        st_ref[...] = jnp.zeros_like(st_ref)
        carry_ref[...] = jnp.zeros_like(carry_ref)

    row = lax.broadcasted_iota(jnp.int32, (BLOCK, BLOCK), 0)
    col = lax.broadcasted_iota(jnp.int32, (BLOCK, BLOCK), 1)
    causal = row >= col
    tril = causal.astype(BF16)
    tril_sub = (causal & ((row // SUB) == (col // SUB))).astype(BF16)
    row_w = lax.broadcasted_iota(jnp.int32, (BLOCK, GLA_KWIDTH), 0)
    row_blk = row // SUB

    batches = range(q_ref.shape[0])
    problems = [(b, h) for b in batches for h in range(GLA_HEADS)]

    cb_all, cw_all = [], []
    for b in batches:
        cl = _dot_exact_lhs(tril, _split3(lf_ref[b])) + carry_ref[b]
        carry_ref[b] = cl[BLOCK - 1:BLOCK, :]
        cc_ref[b] = cl
        cr_ref[b] = cl.T[0:FOX_HEADS, :]
        la_parts = _split3(la_ref[b])
        cb_all.append(_dot_exact_lhs(tril, la_parts))
        cw_all.append(_dot_exact_lhs(tril_sub, la_parts))

    ops = {}
    for b in batches:
        cb, cw = cb_all[b], cw_all[b]
        last = cb[BLOCK - 1:BLOCK, :]
        q = q_ref[b]
        k = k_ref[b]
        q_glob = (q * jnp.exp(cb)).astype(BF16)
        q_loc = (q * jnp.exp(cw)).astype(BF16)
        k_end = (k * jnp.exp(last - cb)).astype(BF16)
        decay = jnp.exp(last)
        k_sub = []
        for i in range(N_SUB):
            ref = cb[i * SUB - 1:i * SUB, :] if i > 0 else jnp.zeros_like(last)
            e = jnp.exp(jnp.minimum(ref - cb, EXP_CAP))
            k_sub.append(jnp.where(row_w < (i + 1) * SUB, k * e, 0.0).astype(BF16))
        for pair in range(GLA_HEADS // 2):
            sl = slice(pair * LANES, (pair + 1) * LANES)
            k_cat = jnp.concatenate([ks[:, sl] for ks in k_sub], axis=1)
            for half in range(2):
                h = 2 * pair + half
                head_lanes = (col < GLA_DK) if half == 0 else (col >= GLA_DK)
                q_h = jnp.where(head_lanes, q_loc[:, sl], 0.0)
                q_cat = jnp.concatenate(
                    [jnp.where(row_blk == i, q_h, 0.0) for i in range(N_SUB)], axis=1)
                ops[b, h] = (q_cat, k_cat, jnp.where(head_lanes, q_glob[:, sl], 0.0), k_end[:, sl],
                             decay[:, sl], v_ref[b, :, h * GLA_DV:(h + 1) * GLA_DV])

    att, inter, upd = {}, {}, {}
    for p in problems:
        q_cat, k_cat, q_g, k_e, _, v_h = ops[p]
        att[p] = _dot_nt(q_cat, k_cat)
        inter[p] = _dot_nt(q_g, st_ref[p].astype(BF16))
        upd[p] = _dot_tn(v_h, k_e)

    for p in problems:
        b, h = p
        a = jnp.where(causal, att[p], 0.0).astype(BF16)
        o_ref[b, :, h * GLA_DV:(h + 1) * GLA_DV] = inter[p] + _dot(a, ops[p][5])
        st_ref[p] = st_ref[p] * ops[p][4] + upd[p]


def _gla(gq, gk, gv, la, lf, batch, seq_len):
    nc = seq_len // BLOCK
    chunk = lambda c: (c + nc - 1) % nc
    blk = lambda width: pl.BlockSpec((batch, BLOCK, width), lambda c: (0, chunk(c), 0))
    r3 = lambda a: a.reshape(batch, seq_len, a.shape[-1])
    return pl.pallas_call(
        _gla_kernel,
        grid=(nc,),
        in_specs=[blk(GLA_KWIDTH), blk(GLA_KWIDTH), blk(GLA_WIDTH), blk(GLA_KWIDTH), blk(LANES)],
        out_specs=[blk(GLA_WIDTH), blk(LANES),
                   pl.BlockSpec((batch, FOX_HEADS, BLOCK), lambda c: (0, 0, chunk(c)))],
        out_shape=[jax.ShapeDtypeStruct((batch, seq_len, GLA_WIDTH), F32),
                   jax.ShapeDtypeStruct((batch, seq_len, LANES), F32),
                   jax.ShapeDtypeStruct((batch, FOX_HEADS, seq_len), F32)],
        scratch_shapes=[pltpu.VMEM((batch, GLA_HEADS, GLA_DV, LANES), F32),
                        pltpu.VMEM((batch, 1, LANES), F32)],
        compiler_params=pltpu.CompilerParams(
            dimension_semantics=("arbitrary",), vmem_limit_bytes=VMEM_LIMIT),
    )(r3(gq), r3(gk), r3(gv), r3(la), r3(lf))


def _fox_kernel(q_ref, k_ref, v_ref, cc_ref, crp_ref, crm_ref, cb_ref, cs_ref, o_ref,
                ra_ref, rb_ref, ub_ref, plan_ref, m_ref, l_ref, acc_ref):
    t = FOX_T
    nt = crm_ref.shape[2]
    s_main = nt * t
    pre = slice(s_main, s_main + BLOCK)
    pair = pl.program_id(1)

    def stack_heads(x, lane_first):
        return jnp.concatenate([jnp.where(lane_first, x, 0.0), jnp.where(lane_first, 0.0, x)], axis=0)

    def head_cols(cc, lane):
        return jnp.concatenate(
            [jnp.sum(jnp.where(lane == 2 * pair + hh, cc, 0.0), axis=1, keepdims=True)
             for hh in range(2)], axis=0)

    lane_b = lax.broadcasted_iota(jnp.int32, (BLOCK, LANES), 1)
    first_b = lane_b < FOX_HEAD_DIM
    kp = k_ref[0, pre, :]
    vp = v_ref[0, pre, :]
    ckp = crp_ref[0, 0]
    row_b = lax.broadcasted_iota(jnp.int32, (BLOCK, BLOCK), 0)
    ok_b = (lane_b <= row_b) & ((lane_b >= N_PAD) | (lane_b == row_b))
    ok_b = jnp.concatenate([ok_b, ok_b], axis=0)
    s = _dot_nt(stack_heads(q_ref[0, pre, :], first_b), kp)
    z = jnp.concatenate([s[:BLOCK] - ckp[0:1], s[BLOCK:] - ckp[1:2]], axis=0)
    z = jnp.where(ok_b, z, NEG)
    cq2 = head_cols(cc_ref[0, pre, :], lane_b)
    m = jnp.max(z, axis=1, keepdims=True) + cq2
    p = jnp.exp2(z - (m - cq2))
    out = _dot(p.astype(BF16), vp) / jnp.sum(p, axis=1, keepdims=True)
    o_ref[0, pre, :] = jnp.where(first_b, out[:BLOCK], out[BLOCK:]).astype(o_ref.dtype)

    def v_with_ones(rows, head_b):
        v = v_ref[0, rows, :]
        lane_v = lax.broadcasted_iota(jnp.int32, v.shape, 1)
        if head_b:
            return jnp.where(lane_v >= FOX_HEAD_DIM, v, jnp.where(lane_v == 0, 1.0, 0.0).astype(BF16))
        return jnp.where(lane_v < FOX_HEAD_DIM, v, jnp.where(lane_v == FOX_HEAD_DIM, 1.0, 0.0).astype(BF16))

    lane = lax.broadcasted_iota(jnp.int32, (t, LANES), 1)
    first = lane < FOX_HEAD_DIM
    row_l = lax.broadcasted_iota(jnp.int32, (LANES, LANES), 0)
    row_w = lax.broadcasted_iota(jnp.int32, (LANES, 2 * LANES), 0)
    second = lax.broadcasted_iota(jnp.int32, (LANES, 2 * LANES), 1) >= LANES
    ind_ab = ((row_w >= FOX_HEAD_DIM) == second).astype(BF16)
    hot_ab = (row_w == 2 * pair + second.astype(jnp.int32)).astype(BF16)
    hot_la = (row_l == FOX_HEAD_DIM).astype(BF16)
    hot_lb = (row_l == 0).astype(BF16)
    reps = t // LANES
    th = t // 2

    sel = lax.broadcasted_iota(jnp.int32, (8, LANES), 0) == lax.broadcasted_iota(
        jnp.int32, (8, LANES), 1) // FOX_HEAD_DIM
    k_all = k_ref[0]
    k_max = jnp.sqrt(jnp.max(_dot_nt(sel.astype(BF16), k_all * k_all), axis=1, keepdims=True))

    def prepare(i, carry):
        rows = pl.ds(pl.multiple_of(i * t, t), t)
        q = q_ref[0, rows, :]
        nq = _dot(q * q, ind_ab)
        cq = _dot(cc_ref[0, rows, :].astype(BF16), hot_ab)
        ub_a = jnp.sqrt(nq[:, :LANES]) * (k_max[0:1] * NORM_UP)
        ub_b = jnp.sqrt(nq[:, LANES:]) * (k_max[1:2] * NORM_UP)
        ra_ref[rows, :] = ub_a - cq[:, :LANES]
        rb_ref[rows, :] = ub_b - cq[:, LANES:]
        ub_ref[pl.ds(i, 1), :] = jnp.max(jnp.maximum(ub_a, ub_b), axis=0, keepdims=True)
        return carry

    lax.fori_loop(0, nt, prepare, 0)

    ub_t = ub_ref[0:nt, :]
    fast_v = ub_t <= FAST_MAX_BITS
    bound = jnp.where(fast_v, PRUNE_BITS, PRUNE_BITS + 2.0 * ub_t)
    cb = cb_ref[0, 0]
    dead = ((cs_ref[0, 0, 0, 0:nt, :] - cb[0:1]) < -bound) & ((cs_ref[0, 0, 1, 0:nt, :] - cb[1:2]) < -bound)
    tile_q = lax.broadcasted_iota(jnp.int32, (nt, LANES), 0)
    tile_k = lax.broadcasted_iota(jnp.int32, (nt, LANES), 1)
    n_skip_v = jnp.sum(jnp.where(dead & (tile_k < tile_q), 1, 0), axis=1, keepdims=True)
    skip_pre_v = jnp.sum(jnp.where(dead & (tile_k == nt), 1, 0), axis=1, keepdims=True)
    fast_i = jnp.where(fast_v[:, 0:1], 1, 0)
    tile_1 = tile_q[:, 0:1]
    for i in range(nt):
        for row, vec in enumerate((fast_i, n_skip_v, skip_pre_v)):
            plan_ref[row, i] = jnp.sum(jnp.where(tile_1 == i, vec, 0))

    def main_tile(i, carry):
        r0 = pl.multiple_of(i * t, t)
        q2 = stack_heads(q_ref[0, pl.ds(r0, t), :], first)
        fast = plan_ref[0, i] == 1
        n_skip = plan_ref[1, i]
        skip_prefix = plan_ref[2, i]

        def logits(kt, ck, offs):
            s = _dot_nt(q2, kt)
            return jnp.concatenate([s[:t] - ck[0:1] - offs[:t], s[t:] - ck[1:2] - offs[t:]], axis=0)

        def run(update, offs_p, offs_m, fuse_tail):
            acc_ref[...] = jnp.zeros_like(acc_ref)

            @pl.when(skip_prefix == 0)
            def _():
                z = logits(kp, ckp, offs_p)
                lane_k = lax.broadcasted_iota(jnp.int32, (2 * t, BLOCK), 1)
                update([jnp.where(lane_k >= N_PAD, z, NEG)], [pre])

            def tiles(js, diagonal):
                split = diagonal and fuse_tail
                full = js[:-1] if split else js
                rows = [pl.ds(pl.multiple_of(j * t, t), t) for j in full]
                zs = [logits(k_ref[0, r, :], crm_ref[0, 0, j], offs_m) for j, r in zip(full, rows)]
                if not diagonal:
                    update(zs, rows)
                    return
                if not split:
                    ok = lax.broadcasted_iota(jnp.int32, (t, t), 1) <= lax.broadcasted_iota(jnp.int32, (t, t), 0)
                    zs[-1] = jnp.where(jnp.concatenate([ok, ok], axis=0), zs[-1], NEG)
                    update(zs, rows)
                    return
                base = pl.multiple_of(js[-1] * t, t)
                ck = crm_ref[0, 0, js[-1]]
                tri = lax.broadcasted_iota(jnp.int32, (th, th), 1) <= lax.broadcasted_iota(jnp.int32, (th, th), 0)
                r_left = pl.ds(base, th)
                z_left = logits(k_ref[0, r_left, :], ck[:, :th], offs_m[:, :th])
                lower_rows = (lax.broadcasted_iota(jnp.int32, (2 * t, th), 0) // th) % 2 == 1
                z_left = jnp.where(lower_rows | jnp.concatenate([tri] * 4, axis=0), z_left, NEG)
                r_right = pl.ds(base + th, th)
                q_lo = jnp.concatenate([q2[th:t], q2[t + th:]], axis=0)
                o_lo = jnp.concatenate([offs_m[th:t, :th], offs_m[t + th:, :th]], axis=0)
                s = _dot_nt(q_lo, k_ref[0, r_right, :])
                z_right = jnp.concatenate([s[:th] - ck[0:1, th:] - o_lo[:th], s[th:] - ck[1:2, th:] - o_lo[th:]],
                                          axis=0)
                z_right = jnp.where(jnp.concatenate([tri, tri], axis=0), z_right, NEG)
                update(zs + [z_left], rows + [r_left], (z_right, r_right))

            def body(j, c):
                tiles([j], False)
                return c

            if not fuse_tail:
                lax.fori_loop(n_skip, i, body, 0)
                tiles([i], True)
                return
            n_live = i - n_skip
            lax.fori_loop(n_skip, jnp.maximum(i - 2, n_skip), body, 0)

            @pl.when(n_live >= 2)
            def _():
                tiles([i - 2, i - 1, i], True)

            @pl.when(n_live == 1)
            def _():
                tiles([i - 1, i], True)

            @pl.when(n_live == 0)
            def _():
                tiles([i], True)

        def finish(l_a, l_b):
            acc = acc_ref[...]
            out = jnp.where(first, acc[:t] / l_a, acc[t:] / l_b)
            o_ref[0, pl.ds(r0, t), :] = out.astype(o_ref.dtype)

        @pl.when(fast)
        def _():
            ref = jnp.concatenate([ra_ref[pl.ds(r0, t), :], rb_ref[pl.ds(r0, t), :]], axis=0)

            def update(zs, rows, lower=None):
                ps = [jnp.exp2(z).astype(BF16) for z in zs]
                p_lo = None if lower is None else jnp.exp2(lower[0]).astype(BF16)
                for half, head_b in ((slice(0, t), False), (slice(t, 2 * t), True)):
                    pv = _dot(ps[0][half], v_with_ones(rows[0], head_b))
                    for p, r in zip(ps[1:], rows[1:]):
                        pv = pv + _dot(p[half], v_with_ones(r, head_b))
                    acc_ref[half] += pv
                if lower is not None:
                    acc_ref[th:t] += _dot(p_lo[:th], v_with_ones(lower[1], False))
                    acc_ref[t + th:] += _dot(p_lo[th:], v_with_ones(lower[1], True))

            run(update, ref, jnp.concatenate([ref] * reps, axis=1), True)

            def denominators(acc, hot):
                hi = acc.astype(BF16)
                lo = (acc - hi.astype(F32)).astype(BF16)
                return _dot(hi, hot) + _dot(lo, hot)

            finish(denominators(acc_ref[0:t], hot_la), denominators(acc_ref[t:], hot_lb))

        @pl.when(jnp.logical_not(fast))
        def _():
            cq2 = head_cols(cc_ref[0, pl.ds(r0, t), :], lane)
            m_ref[...] = jnp.full_like(m_ref, NEG)
            l_ref[...] = jnp.zeros_like(l_ref)

            def update(zs, rows):
                (z,), (r,) = zs, rows
                m_old = m_ref[...]
                m_new = jnp.maximum(m_old, jnp.max(z, axis=1, keepdims=True) + cq2)
                p = jnp.exp2(z - (m_new - cq2))
                alpha = jnp.exp2(m_old - m_new)
                l_ref[...] = alpha * l_ref[...] + jnp.sum(p, axis=1, keepdims=True)
                acc_ref[...] = alpha * acc_ref[...] + _dot(p.astype(BF16), v_ref[0, r, :])
                m_ref[...] = m_new

            run(update, jnp.zeros((2 * t, 1), F32), jnp.zeros((2 * t, 1), F32), False)
            finish(l_ref[0:t], l_ref[t:])

        return carry

    lax.fori_loop(0, nt, main_tile, 0)


def _fox(fq, fk, fv, cum_col, cum_row, batch, seq_len):
    t = FOX_T
    nt = (seq_len - BLOCK) // t
    npairs = FOX_HEADS // 2
    assert nt + 1 <= LANES
    s_main = nt * t
    r3 = lambda a: a.reshape(batch, seq_len, a.shape[-1])
    cr = cum_row.reshape(batch, npairs, 2, seq_len)
    cr_prefix = cr[..., s_main:]
    cr_main = cr[..., :s_main].reshape(batch, npairs, 2, nt, t).transpose(0, 1, 3, 2, 4)
    bounds = jnp.concatenate(
        [cr[..., t - 1:s_main:t], cr[..., seq_len - 1:],
         jnp.zeros((batch, npairs, 2, LANES - nt - 1), F32)], axis=-1)
    nt8 = -(-nt // 8) * 8
    starts = jnp.pad(cr[..., 0:s_main:t], ((0, 0), (0, 0), (0, 0), (0, nt8 - nt)))
    starts = jnp.broadcast_to(starts[..., None], (batch, npairs, 2, nt8, LANES))
    whole = pl.BlockSpec((1, seq_len, LANES), lambda b, p: (b, 0, p))
    out = pl.pallas_call(
        _fox_kernel,
        grid=(batch, npairs),
        in_specs=[whole, whole, whole,
                  pl.BlockSpec((1, seq_len, LANES), lambda b, p: (b, 0, 0)),
                  pl.BlockSpec((1, 1, 2, BLOCK), lambda b, p: (b, p, 0, 0)),
                  pl.BlockSpec((1, 1, nt, 2, t), lambda b, p: (b, p, 0, 0, 0)),
                  pl.BlockSpec((1, 1, 2, LANES), lambda b, p: (b, p, 0, 0)),
                  pl.BlockSpec((1, 1, 2, nt8, LANES), lambda b, p: (b, p, 0, 0, 0))],
        out_specs=whole,
        out_shape=jax.ShapeDtypeStruct((batch, seq_len, FOX_WIDTH), BF16),
        scratch_shapes=[pltpu.VMEM((s_main, LANES), F32), pltpu.VMEM((s_main, LANES), F32),
                        pltpu.VMEM((nt8, LANES), F32),
                        pltpu.SMEM((3, nt), jnp.int32),
                        pltpu.VMEM((2 * t, 1), F32), pltpu.VMEM((2 * t, 1), F32),
                        pltpu.VMEM((2 * t, LANES), F32)],
        compiler_params=pltpu.CompilerParams(
            dimension_semantics=("parallel", "parallel"), vmem_limit_bytes=VMEM_LIMIT),
    )(r3(fq), r3(fk), r3(fv), cum_col, cr_prefix, cr_main, bounds, starts)
    return out.reshape(batch * seq_len, FOX_WIDTH)


def _post_kernel(of_ref, og_ref, gr_ref, gma_ref, gmb_ref, h_ref, ggla_ref, wfo_ref, wgo_ref,
                 wout_ref, gpost_ref, o_ref):
    tm = h_ref.shape[0]
    halves = (slice(0, tm // 2), slice(tm // 2, tm))
    ggla = ggla_ref[...]
    a = [_dot(of_ref[hs, :], wfo_ref[...]) for hs in halves]
    b = []
    for hs in halves:
        og = og_ref[hs, :]
        heads = [_rms(og[:, hh * GLA_DV:(hh + 1) * GLA_DV], ggla[:, hh * GLA_DV:(hh + 1) * GLA_DV])
                 for hh in range(GLA_HEADS)]
        gr = gr_ref[hs, :].astype(F32)
        o_gla = (jnp.concatenate(heads, axis=1) * (gr * jax.nn.sigmoid(gr))).astype(BF16)
        b.append(_dot(o_gla, wgo_ref[...]))
    mix = []
    for hs, a_h, b_h in zip(halves, a, b):
        y = jax.nn.sigmoid(gma_ref[hs, :].astype(F32)) * a_h + jax.nn.sigmoid(gmb_ref[hs, :].astype(F32)) * b_h
        mix.append(_dot(y.astype(BF16), wout_ref[...]))
    for hs, m_h in zip(halves, mix):
        o_ref[hs, :] = h_ref[hs, :] + _rms(m_h, gpost_ref[...])


def _post(o_fox, o_gla, gr, gma, gmb, h, ggla, wfo, wgo, wout, gpost, layer):
    n = h.shape[0]
    tm = ROW_TM
    row = lambda a: pl.BlockSpec((tm, a.shape[1]), lambda i: (i, 0))
    full = lambda a: (_resident(a.shape[1:], layer) if a.ndim == 3
                      else pl.BlockSpec(a.shape, lambda i: (0, 0)))
    args = (o_fox, o_gla, gr, gma, gmb, h, ggla, wfo, wgo, wout, gpost)
    return pl.pallas_call(
        _post_kernel,
        grid=(n // tm,),
        in_specs=[row(a) for a in args[:6]] + [full(a) for a in args[6:]],
        out_specs=pl.BlockSpec((tm, D_MODEL), lambda i: (i, 0)),
        out_shape=jax.ShapeDtypeStruct((n, D_MODEL), F32),
        compiler_params=pltpu.CompilerParams(
            dimension_semantics=("parallel",), vmem_limit_bytes=VMEM_LIMIT),
    )(*args)


def kernel(x, meta_tokens, w_in, w_alpha_up, b_alpha, b_f, g_gla_out, w_fox_o, w_gla_o, w_out,
           w_ffn1_gu, w_ffn1_down, w_ffn2_gu, w_ffn2_down,
           g_pre_ffn1, g_post_ffn1, g_pre_mix, g_post_mix, g_pre_ffn2, g_post_ffn2):
    batch, seq, d = x.shape
    seq_len = seq + BLOCK
    depth = w_in.shape[0]
    assert d == D_MODEL and (seq_len - BLOCK) % FOX_T == 0 and seq_len % ROW_TM == 0
    assert seq_len % FFN_TM == 0 and seq % FFN_TM_MAIN == 0

    pad = jnp.zeros((batch, N_PAD, d), x.dtype)
    meta = jnp.broadcast_to(meta_tokens.astype(x.dtype)[None], (batch, N_META, d))
    prefix = jnp.concatenate([pad, meta], axis=1)
    flat = lambda a: a.reshape(batch * seq_len, a.shape[-1])

    row2 = lambda a: a.reshape(1, -1).astype(F32)
    w_perm = w_in.astype(BF16)
    w1_gu, w1_down, w2_gu, w2_down = w_ffn1_gu, w_ffn1_down, w_ffn2_gu, w_ffn2_down
    wfo, wgo, wout = w_fox_o.astype(BF16), w_gla_o.astype(BF16), w_out.astype(BF16)
    for l in range(depth):
        wa_pad = jnp.zeros((LANES, GLA_KWIDTH), F32).at[
            FOX_HEADS:FOX_HEADS + GLA_GATE_RANK].set(w_alpha_up[l]).astype(BF16)
        bf_pad = jnp.zeros((1, LANES), F32).at[0, :FOX_HEADS].set(b_f[l])

        if l == 0:
            h = _ffn_first(x, prefix, row2(g_pre_ffn1[l]), w1_gu, w1_down, row2(g_post_ffn1[l]), l)
        else:
            h = _ffn(h, row2(g_pre_ffn1[l]), w1_gu, w1_down, row2(g_post_ffn1[l]), l)
        fq, fk, fv, gq, gk, gv, gr, gma, gmb, la, lf = _proj(
            flat(h), row2(g_pre_mix[l]), w_perm, wa_pad, row2(b_alpha[l]), bf_pad, seq_len, l)
        o_gla, cum_col, cum_row = _gla(gq, gk, gv, la, lf, batch, seq_len)
        o_fox = _fox(fq, fk, fv, cum_col, cum_row, batch, seq_len)
        h = _post(o_fox, flat(o_gla), gr, gma, gmb, flat(h),
                  row2(g_gla_out[l]), wfo, wgo, wout, row2(g_post_mix[l]), l).reshape(batch, seq_len, d)
        h = _ffn(h, row2(g_pre_ffn2[l]), w2_gu, w2_down, row2(g_post_ffn2[l]), l,
                 main_rows=seq if l == depth - 1 else None)
    return h
```

```python
import functools

import jax
import jax.numpy as jnp
import numpy as np
from jax import lax
from jax.experimental import pallas as pl
from jax.experimental.pallas import tpu as pltpu

D_MODEL = 1024
N_META = 16
BLOCK = 128
N_PAD = BLOCK - N_META
FOX_HEADS = 8
FOX_HEAD_DIM = 64
FOX_WIDTH = FOX_HEADS * FOX_HEAD_DIM
GLA_HEADS = 4
GLA_WIDTH = 512
GLA_DV = 128
GLA_DK = 64
GLA_KWIDTH = GLA_HEADS * GLA_DK
GLA_GATE_RANK = 16
GLA_GATE_TEMP = 16.0
D_FF = 2816
RMS_EPS = 1e-6

LANES = 128
SUB = 16
N_SUB = BLOCK // SUB
EXP_CAP = 80.0
NEG = -1e30

FFN_TM = 640
FFN_TM_MAIN = 512
FFN_TF = 256
FFN_W_ROWS = 256
ROW_TM = 640
FOX_T = 512
LOG2E = 1.4426950408889634
PRUNE_BITS = 160.0
NORM_UP = 1.01
FAST_MAX_BITS = 50.0
VMEM_LIMIT = 56 * 1024 * 1024

BF16 = jnp.bfloat16
F32 = jnp.float32

C_FQ, C_FK, C_FV = 0, 512, 1024
C_GQ, C_GK, C_GV = 1536, 1792, 2048
C_GR, C_GMA, C_GMB = 2560, 3072, 4096
C_SMALL = 5120
N_IN = 5144
N_IN_PERM = 5248
W_IN_MOVES = ((0, 0, 1536), (1536, 1544, 1024), (2560, 2584, 2560), (5120, 1536, 8), (5128, 2568, 16))


def _dot(a, b):
    return jnp.dot(a, b, preferred_element_type=F32)


def _dot_nt(a, b):
    return lax.dot_general(a, b, (((1,), (1,)), ((), ())), preferred_element_type=F32)


def _dot_tn(a, b):
    return lax.dot_general(a, b, (((0,), (0,)), ((), ())), preferred_element_type=F32)


def _rms(x, g):
    return x * lax.rsqrt(jnp.mean(x * x, axis=-1, keepdims=True) + RMS_EPS) * g


def _log_sigmoid(x):
    return jnp.minimum(x, 0.0) - jnp.log1p(jnp.exp(-jnp.abs(x)))


def _split3(x):
    hi = x.astype(BF16)
    r1 = x - hi.astype(F32)
    mid = r1.astype(BF16)
    lo = (r1 - mid.astype(F32)).astype(BF16)
    return hi, mid, lo


def _dot_exact_lhs(m, parts):
    return _dot(m, parts[0]) + _dot(m, parts[1]) + _dot(m, parts[2])


def _ffn_rows(h, gpre_ref, wgu_ref, wd_ref, gpost_ref, xn_ref, a_ref):
    m = h.shape[0]
    parts = (slice(0, m // 2), slice(m // 2, m)) if m >= 2 * FFN_TF else (slice(0, m),)
    out = []
    for rows in parts:
        h_p = h[rows]
        xn_ref[rows] = _rms(h_p, gpre_ref[...]).astype(BF16)
        for c in range(D_FF // FFN_TF):
            lo = c * FFN_TF
            g = _dot(xn_ref[rows], wgu_ref[:, lo:lo + FFN_TF])
            u = _dot(xn_ref[rows], wgu_ref[:, D_FF + lo:D_FF + lo + FFN_TF])
            a_ref[rows, lo:lo + FFN_TF] = (g * jax.nn.sigmoid(g) * u).astype(BF16)
        y = _dot(a_ref[rows], wd_ref[...])
        out.append(h_p + 0.5 * _rms(y, gpost_ref[...]))
    return out[0] if len(out) == 1 else jnp.concatenate(out, axis=0)


def _stream_weight(w_hbm, layer, n_rows, stage_ref, sem, sem_row, consume):
    rows = stage_ref.shape[1]
    n = n_rows // rows

    def chunk(c):
        return pltpu.make_async_copy(w_hbm.at[layer, pl.ds(c * rows, rows), :], stage_ref.at[c % 2],
                                     sem.at[sem_row, c % 2])

    chunk(0).start()
    for c in range(n):
        if c + 1 < n:
            chunk(c + 1).start()
        chunk(c).wait()
        consume(slice(c * rows, (c + 1) * rows), stage_ref.at[c % 2])


def _cast_into(dst_ref):
    def consume(rows, staged):
        dst_ref[rows, :] = staged[...].astype(BF16)
    return consume


def _ffn_weights(wgu_hbm, wd_hbm, layer, wgu_ref, wd_ref, sgu_ref, sd_ref, sem):
    @pl.when((pl.program_id(0) == 0) & (pl.program_id(1) == 0))
    def _():
        _stream_weight(wgu_hbm, layer, D_MODEL, sgu_ref, sem, 0, _cast_into(wgu_ref))
        _stream_weight(wd_hbm, layer, D_FF, sd_ref, sem, 1, _cast_into(wd_ref))


def _ffn_kernel(h_ref, gpre_ref, wgu_hbm, wd_hbm, gpost_ref, o_ref,
                xn_ref, a_ref, wgu_ref, wd_ref, sgu_ref, sd_ref, sem, *, layer):
    _ffn_weights(wgu_hbm, wd_hbm, layer, wgu_ref, wd_ref, sgu_ref, sd_ref, sem)
    o_ref[...] = _ffn_rows(h_ref[...], gpre_ref, wgu_ref, wd_ref, gpost_ref, xn_ref, a_ref)


def _ffn_first_kernel(x_ref, p_ref, gpre_ref, wgu_hbm, wd_hbm, gpost_ref, o_ref,
                      xn_ref, a_ref, wgu_ref, wd_ref, sgu_ref, sd_ref, sem, *, layer):
    _ffn_weights(wgu_hbm, wd_hbm, layer, wgu_ref, wd_ref, sgu_ref, sd_ref, sem)
    last = pl.program_id(1) == pl.num_programs(1) - 1

    @pl.when(jnp.logical_not(last))
    def _():
        o_ref[...] = _ffn_rows(x_ref[...], gpre_ref, wgu_ref, wd_ref, gpost_ref, xn_ref, a_ref)

    @pl.when(last)
    def _():
        o_ref[0:BLOCK] = _ffn_rows(p_ref[...], gpre_ref, wgu_ref, wd_ref, gpost_ref, xn_ref, a_ref)


def _resident(shape, layer):
    return pl.BlockSpec((None,) + shape, lambda *_: (layer,) + (0,) * len(shape),
                        pipeline_mode=pl.Buffered(1))


def _ffn_scratch(tm):
    return [pltpu.VMEM((tm, D_MODEL), BF16), pltpu.VMEM((tm, D_FF), BF16),
            pltpu.VMEM((D_MODEL, 2 * D_FF), BF16), pltpu.VMEM((D_FF, D_MODEL), BF16),
            pltpu.VMEM((2, FFN_W_ROWS, 2 * D_FF), F32), pltpu.VMEM((2, FFN_W_ROWS, D_MODEL), F32),
            pltpu.SemaphoreType.DMA((2, 2))]


def _ffn(h, gpre, w_gu, w_down, gpost, layer, main_rows=None):
    batch, seq_len, _ = h.shape
    if main_rows is None:
        tm, rows = FFN_TM, seq_len
    else:
        tm, rows = FFN_TM_MAIN, main_rows
    tile = pl.BlockSpec((None, tm, D_MODEL), lambda b, i: (b, i, 0))
    vec = pl.BlockSpec((1, D_MODEL), lambda b, i: (0, 0))
    hbm = pl.BlockSpec(memory_space=pl.ANY)
    return pl.pallas_call(
        functools.partial(_ffn_kernel, layer=layer),
        grid=(batch, rows // tm),
        in_specs=[tile, vec, hbm, hbm, vec],
        out_specs=tile,
        out_shape=jax.ShapeDtypeStruct((batch, rows, D_MODEL), F32),
        scratch_shapes=_ffn_scratch(tm),
        compiler_params=pltpu.CompilerParams(
            dimension_semantics=("arbitrary", "arbitrary"), vmem_limit_bytes=VMEM_LIMIT),
    )(h, gpre, w_gu, w_down, gpost)


def _ffn_first(x, prefix, gpre, w_gu, w_down, gpost, layer):
    batch, seq, _ = x.shape
    tm = FFN_TM_MAIN
    n_main = seq // tm
    vec = pl.BlockSpec((1, D_MODEL), lambda b, i: (0, 0))
    hbm = pl.BlockSpec(memory_space=pl.ANY)
    return pl.pallas_call(
        functools.partial(_ffn_first_kernel, layer=layer),
        grid=(batch, n_main + 1),
        in_specs=[pl.BlockSpec((None, tm, D_MODEL), lambda b, i: (b, jnp.minimum(i, n_main - 1), 0)),
                  pl.BlockSpec((None, BLOCK, D_MODEL), lambda b, i: (b, 0, 0)),
                  vec, hbm, hbm, vec],
        out_specs=pl.BlockSpec((None, tm, D_MODEL), lambda b, i: (b, i, 0)),
        out_shape=jax.ShapeDtypeStruct((batch, seq + BLOCK, D_MODEL), F32),
        scratch_shapes=_ffn_scratch(tm),
        compiler_params=pltpu.CompilerParams(
            dimension_semantics=("arbitrary", "arbitrary"), vmem_limit_bytes=VMEM_LIMIT),
    )(x, prefix, gpre, w_gu, w_down, gpost)


def _proj_kernel(h_ref, g_ref, w_ref, wa_ref, ba_ref, bf_ref,
                 fq_ref, fk_ref, fv_ref, gq_ref, gk_ref, gv_ref, gr_ref, gma_ref, gmb_ref,
                 la_ref, lf_ref, ws_ref, *, tiles_per_seq, main_rows):
    @pl.when(pl.program_id(0) == 0)
    def _():
        for dst, src, width in W_IN_MOVES:
            ws_ref[:, dst:dst + width] = w_ref[:, src:src + width]
        ws_ref[:, N_IN:] = jnp.zeros((D_MODEL, N_IN_PERM - N_IN), BF16)

    w_ref = ws_ref
    tm = h_ref.shape[0]
    m = tm // 2
    halves = (slice(0, m), slice(m, tm))
    xns = {}
    for rows in halves:
        if rows.start not in xns:
            xns[rows.start] = _rms(h_ref[rows, :], g_ref[...]).astype(BF16)
        xn = xns[rows.start]
        pos = ((pl.program_id(0) % tiles_per_seq) * tm + rows.start
               + lax.broadcasted_iota(jnp.int32, (m, 1), 0))
        valid = jnp.logical_or(pos < main_rows, pos >= main_rows + N_PAD).astype(F32)

        def proj(c0, width):
            return _dot(xn, w_ref[:, c0:c0 + width])

        small = proj(C_SMALL, LANES)
        if rows.start == 0:
            xns[m] = _rms(h_ref[halves[1], :], g_ref[...]).astype(BF16)
        fq_ref[rows, :] = (proj(C_FQ, FOX_WIDTH) * (FOX_HEAD_DIM ** -0.5 * LOG2E)).astype(BF16)
        fk_ref[rows, :] = proj(C_FK, FOX_WIDTH).astype(BF16)
        lane = lax.broadcasted_iota(jnp.int32, (m, LANES), 1)
        lf = _log_sigmoid(small + bf_ref[...]) * (valid * LOG2E)
        lf_ref[rows, :] = jnp.where(lane < FOX_HEADS, lf, 0.0)
        xa = _dot(small.astype(BF16), wa_ref[...]) + ba_ref[...]
        la_ref[rows, :] = _log_sigmoid(xa) * (valid * (1.0 / GLA_GATE_TEMP))
        fv_ref[rows, :] = proj(C_FV, FOX_WIDTH).astype(BF16)
        gq_ref[rows, :] = proj(C_GQ, GLA_KWIDTH) * (GLA_DK ** -0.5)
        gk_ref[rows, :] = proj(C_GK, GLA_KWIDTH) * valid
        gv_ref[rows, :] = proj(C_GV, GLA_WIDTH).astype(BF16)
        gr_ref[rows, :] = proj(C_GR, GLA_WIDTH).astype(BF16)
        gma_ref[rows, :] = proj(C_GMA, D_MODEL).astype(BF16)
        gmb_ref[rows, :] = proj(C_GMB, D_MODEL).astype(BF16)


def _proj(h, g, w_perm, wa_pad, ba, bf_pad, seq_len, layer):
    n = h.shape[0]
    tm = ROW_TM
    row = lambda width: pl.BlockSpec((tm, width), lambda i: (i, 0))
    full = lambda a: (_resident(a.shape[1:], layer) if a.ndim == 3
                      else pl.BlockSpec(a.shape, lambda i: (0, 0)))
    outs = [
        (FOX_WIDTH, BF16), (FOX_WIDTH, BF16), (FOX_WIDTH, BF16),
        (GLA_KWIDTH, F32), (GLA_KWIDTH, F32), (GLA_WIDTH, BF16),
        (GLA_WIDTH, BF16), (D_MODEL, BF16), (D_MODEL, BF16),
        (GLA_KWIDTH, F32), (LANES, F32),
    ]
    return pl.pallas_call(
        functools.partial(_proj_kernel, tiles_per_seq=seq_len // tm, main_rows=seq_len - BLOCK),
        grid=(n // tm,),
        in_specs=[row(D_MODEL), full(g), full(w_perm), full(wa_pad), full(ba), full(bf_pad)],
        out_specs=[row(w) for w, _ in outs],
        out_shape=[jax.ShapeDtypeStruct((n, w), dt) for w, dt in outs],
        scratch_shapes=[pltpu.VMEM((D_MODEL, N_IN_PERM), BF16)],
        compiler_params=pltpu.CompilerParams(
            dimension_semantics=("arbitrary",), vmem_limit_bytes=VMEM_LIMIT),
    )(h, g, w_perm, wa_pad, ba, bf_pad)


def _gla_kernel(q_ref, k_ref, v_ref, la_ref, lf_ref, o_ref, cc_ref, cr_ref, st_ref, carry_ref):
    @pl.when(pl.program_id(0) == 0)
    def _():
        st_ref[...] = jnp.zeros_like(st_ref)
        carry_ref[...] = jnp.zeros_like(carry_ref)

    row = lax.broadcasted_iota(jnp.int32, (BLOCK, BLOCK), 0)
    col = lax.broadcasted_iota(jnp.int32, (BLOCK, BLOCK), 1)
    causal = row >= col
    tril = causal.astype(BF16)
    tril_sub = (causal & ((row // SUB) == (col // SUB))).astype(BF16)
    row_w = lax.broadcasted_iota(jnp.int32, (BLOCK, GLA_KWIDTH), 0)
    row_blk = row // SUB

    batches = range(q_ref.shape[0])
    problems = [(b, h) for b in batches for h in range(GLA_HEADS)]

    cb_all, cw_all = [], []
    for b in batches:
        cl = _dot_exact_lhs(tril, _split3(lf_ref[b])) + carry_ref[b]
        carry_ref[b] = cl[BLOCK - 1:BLOCK, :]
        cc_ref[b] = cl
        cr_ref[b] = cl.T[0:FOX_HEADS, :]
        la_parts = _split3(la_ref[b])
        cb_all.append(_dot_exact_lhs(tril, la_parts))
        cw_all.append(_dot_exact_lhs(tril_sub, la_parts))

    ops = {}
    for b in batches:
        cb, cw = cb_all[b], cw_all[b]
        last = cb[BLOCK - 1:BLOCK, :]
        q = q_ref[b]
        k = k_ref[b]
        q_glob = (q * jnp.exp(cb)).astype(BF16)
        q_loc = (q * jnp.exp(cw)).astype(BF16)
        k_end = (k * jnp.exp(last - cb)).astype(BF16)
        decay = jnp.exp(last)
        k_sub = []
        for i in range(N_SUB):
            ref = cb[i * SUB - 1:i * SUB, :] if i > 0 else jnp.zeros_like(last)
            e = jnp.exp(jnp.minimum(ref - cb, EXP_CAP))
            k_sub.append(jnp.where(row_w < (i + 1) * SUB, k * e, 0.0).astype(BF16))
        for pair in range(GLA_HEADS // 2):
            sl = slice(pair * LANES, (pair + 1) * LANES)
            k_cat = jnp.concatenate([ks[:, sl] for ks in k_sub], axis=1)
            for half in range(2):
                h = 2 * pair + half
                head_lanes = (col < GLA_DK) if half == 0 else (col >= GLA_DK)
                q_h = jnp.where(head_lanes, q_loc[:, sl], 0.0)
                q_cat = jnp.concatenate(
                    [jnp.where(row_blk == i, q_h, 0.0) for i in range(N_SUB)], axis=1)
                ops[b, h] = (q_cat, k_cat, jnp.where(head_lanes, q_glob[:, sl], 0.0), k_end[:, sl],
                             decay[:, sl], v_ref[b, :, h * GLA_DV:(h + 1) * GLA_DV])

    att, inter, upd = {}, {}, {}
    for p in problems:
        q_cat, k_cat, q_g, k_e, _, v_h = ops[p]
        att[p] = _dot_nt(q_cat, k_cat)
        inter[p] = _dot_nt(q_g, st_ref[p].astype(BF16))
        upd[p] = _dot_tn(v_h, k_e)

    for p in problems:
        b, h = p
        a = jnp.where(causal, att[p], 0.0).astype(BF16)
        o_ref[b, :, h * GLA_DV:(h + 1) * GLA_DV] = inter[p] + _dot(a, ops[p][5])
        st_ref[p] = st_ref[p] * ops[p][4] + upd[p]


def _gla(gq, gk, gv, la, lf, batch, seq_len):
    nc = seq_len // BLOCK
    chunk = lambda c: (c + nc - 1) % nc
    blk = lambda width: pl.BlockSpec((batch, BLOCK, width), lambda c: (0, chunk(c), 0))
    r3 = lambda a: a.reshape(batch, seq_len, a.shape[-1])
    return pl.pallas_call(
        _gla_kernel,
        grid=(nc,),
        in_specs=[blk(GLA_KWIDTH), blk(GLA_KWIDTH), blk(GLA_WIDTH), blk(GLA_KWIDTH), blk(LANES)],
        out_specs=[blk(GLA_WIDTH), blk(LANES),
                   pl.BlockSpec((batch, FOX_HEADS, BLOCK), lambda c: (0, 0, chunk(c)))],
        out_shape=[jax.ShapeDtypeStruct((batch, seq_len, GLA_WIDTH), F32),
                   jax.ShapeDtypeStruct((batch, seq_len, LANES), F32),
                   jax.ShapeDtypeStruct((batch, FOX_HEADS, seq_len), F32)],
        scratch_shapes=[pltpu.VMEM((batch, GLA_HEADS, GLA_DV, LANES), F32),
                        pltpu.VMEM((batch, 1, LANES), F32)],
        compiler_params=pltpu.CompilerParams(
            dimension_semantics=("arbitrary",), vmem_limit_bytes=VMEM_LIMIT),
    )(r3(gq), r3(gk), r3(gv), r3(la), r3(lf))


def _fox_kernel(q_ref, k_ref, v_ref, cc_ref, crp_ref, crm_ref, cb_ref, cs_ref, o_ref,
                ra_ref, rb_ref, ub_ref, plan_ref, m_ref, l_ref, acc_ref):
    t = FOX_T
    nt = crm_ref.shape[2]
    s_main = nt * t
    pre = slice(s_main, s_main + BLOCK)
    pair = pl.program_id(1)

    def stack_heads(x, lane_first):
        return jnp.concatenate([jnp.where(lane_first, x, 0.0), jnp.where(lane_first, 0.0, x)], axis=0)

    def head_cols(cc, lane):
        return jnp.concatenate(
            [jnp.sum(jnp.where(lane == 2 * pair + hh, cc, 0.0), axis=1, keepdims=True)
             for hh in range(2)], axis=0)

    lane_b = lax.broadcasted_iota(jnp.int32, (BLOCK, LANES), 1)
    first_b = lane_b < FOX_HEAD_DIM
    kp = k_ref[0, pre, :]
    vp = v_ref[0, pre, :]
    ckp = crp_ref[0, 0]
    row_b = lax.broadcasted_iota(jnp.int32, (BLOCK, BLOCK), 0)
    ok_b = (lane_b <= row_b) & ((lane_b >= N_PAD) | (lane_b == row_b))
    ok_b = jnp.concatenate([ok_b, ok_b], axis=0)
    s = _dot_nt(stack_heads(q_ref[0, pre, :], first_b), kp)
    z = jnp.concatenate([s[:BLOCK] - ckp[0:1], s[BLOCK:] - ckp[1:2]], axis=0)
    z = jnp.where(ok_b, z, NEG)
    cq2 = head_cols(cc_ref[0, pre, :], lane_b)
    m = jnp.max(z, axis=1, keepdims=True) + cq2
    p = jnp.exp2(z - (m - cq2))
    out = _dot(p.astype(BF16), vp) / jnp.sum(p, axis=1, keepdims=True)
    o_ref[0, pre, :] = jnp.where(first_b, out[:BLOCK], out[BLOCK:]).astype(o_ref.dtype)

    def v_with_ones(rows, head_b):
        v = v_ref[0, rows, :]
        lane_v = lax.broadcasted_iota(jnp.int32, v.shape, 1)
        if head_b:
            return jnp.where(lane_v >= FOX_HEAD_DIM, v, jnp.where(lane_v == 0, 1.0, 0.0).astype(BF16))
        return jnp.where(lane_v < FOX_HEAD_DIM, v, jnp.where(lane_v == FOX_HEAD_DIM, 1.0, 0.0).astype(BF16))

    lane = lax.broadcasted_iota(jnp.int32, (t, LANES), 1)
    first = lane < FOX_HEAD_DIM
    row_l = lax.broadcasted_iota(jnp.int32, (LANES, LANES), 0)
    row_w = lax.broadcasted_iota(jnp.int32, (LANES, 2 * LANES), 0)
    second = lax.broadcasted_iota(jnp.int32, (LANES, 2 * LANES), 1) >= LANES
    ind_ab = ((row_w >= FOX_HEAD_DIM) == second).astype(BF16)
    hot_ab = (row_w == 2 * pair + second.astype(jnp.int32)).astype(BF16)
    hot_la = (row_l == FOX_HEAD_DIM).astype(BF16)
    hot_lb = (row_l == 0).astype(BF16)
    reps = t // LANES
    th = t // 2

    sel = lax.broadcasted_iota(jnp.int32, (8, LANES), 0) == lax.broadcasted_iota(
        jnp.int32, (8, LANES), 1) // FOX_HEAD_DIM
    k_all = k_ref[0]
    k_max = jnp.sqrt(jnp.max(_dot_nt(sel.astype(BF16), k_all * k_all), axis=1, keepdims=True))

    def prepare(i, carry):
        rows = pl.ds(pl.multiple_of(i * t, t), t)
        q = q_ref[0, rows, :]
        nq = _dot(q * q, ind_ab)
        cq = _dot(cc_ref[0, rows, :].astype(BF16), hot_ab)
        ub_a = jnp.sqrt(nq[:, :LANES]) * (k_max[0:1] * NORM_UP)
        ub_b = jnp.sqrt(nq[:, LANES:]) * (k_max[1:2] * NORM_UP)
        ra_ref[rows, :] = ub_a - cq[:, :LANES]
        rb_ref[rows, :] = ub_b - cq[:, LANES:]
        ub_ref[pl.ds(i, 1), :] = jnp.max(jnp.maximum(ub_a, ub_b), axis=0, keepdims=True)
        return carry

    lax.fori_loop(0, nt, prepare, 0)

    ub_t = ub_ref[0:nt, :]
    fast_v = ub_t <= FAST_MAX_BITS
    bound = jnp.where(fast_v, PRUNE_BITS, PRUNE_BITS + 2.0 * ub_t)
    cb = cb_ref[0, 0]
    dead = ((cs_ref[0, 0, 0, 0, 0:nt, :] - cb[0:1]) < -bound) & ((cs_ref[0, 0, 1, 0, 0:nt, :] - cb[1:2]) < -bound)
    tile_q = lax.broadcasted_iota(jnp.int32, (nt, LANES), 0)
    tile_k = lax.broadcasted_iota(jnp.int32, (nt, LANES), 1)
    n_skip_v = jnp.sum(jnp.where(dead & (tile_k < tile_q), 1, 0), axis=1, keepdims=True)
    skip_pre_v = jnp.sum(jnp.where(dead & (tile_k == nt), 1, 0), axis=1, keepdims=True)
    dead_lo = ((cs_ref[0, 0, 0, 1, 0:nt, :] - cb[0:1]) < -bound) & ((cs_ref[0, 0, 1, 1, 0:nt, :] - cb[1:2]) < -bound)
    old_dead_v = jnp.sum(jnp.where(dead_lo & (tile_k == tile_q + (nt - 1)), 1, 0), axis=1, keepdims=True)
    fast_i = jnp.where(fast_v[:, 0:1], 1, 0)
    tile_1 = tile_q[:, 0:1]
    for i in range(nt):
        for row, vec in enumerate((fast_i, n_skip_v, skip_pre_v, old_dead_v)):
            plan_ref[row, i] = jnp.sum(jnp.where(tile_1 == i, vec, 0))

    def main_tile(i, carry):
        r0 = pl.multiple_of(i * t, t)
        q2 = stack_heads(q_ref[0, pl.ds(r0, t), :], first)
        fast = plan_ref[0, i] == 1
        n_skip = plan_ref[1, i]
        skip_prefix = plan_ref[2, i]
        old_half_dead = plan_ref[3, i] == 1

        def logits(kt, ck, offs):
            s = _dot_nt(q2, kt)
            return jnp.concatenate([s[:t] - ck[0:1] - offs[:t], s[t:] - ck[1:2] - offs[t:]], axis=0)

        def run(update, offs_p, offs_m, fuse_tail):
            acc_ref[...] = jnp.zeros_like(acc_ref)

            @pl.when(skip_prefix == 0)
            def _():
                z = logits(kp, ckp, offs_p)
                lane_k = lax.broadcasted_iota(jnp.int32, (2 * t, BLOCK), 1)
                update([jnp.where(lane_k >= N_PAD, z, NEG)], [pre])

            def tiles(js, diagonal, old_half_dead=False):
                split = diagonal and fuse_tail
                full = js[:-1] if split else js
                upper = None
                if old_half_dead:
                    full = full[1:]
                    base = pl.multiple_of(js[0] * t, t)
                    ck = crm_ref[0, 0, js[0]]
                    r_old = pl.ds(base, th)
                    q_up = jnp.concatenate([q2[0:th], q2[t:t + th]], axis=0)
                    o_up = jnp.concatenate([offs_m[0:th, :th], offs_m[t:t + th, :th]], axis=0)
                    s = _dot_nt(q_up, k_ref[0, r_old, :])
                    upper = (jnp.concatenate([s[:th] - ck[0:1, :th] - o_up[:th], s[th:] - ck[1:2, :th] - o_up[th:]],
                                             axis=0), r_old)
                    r_new = pl.ds(base + th, th)
                    z_new = logits(k_ref[0, r_new, :], ck[:, th:], offs_m[:, :th])
                rows = [pl.ds(pl.multiple_of(j * t, t), t) for j in full]
                zs = [logits(k_ref[0, r, :], crm_ref[0, 0, j], offs_m) for j, r in zip(full, rows)]
                if old_half_dead:
                    zs, rows = [z_new] + zs, [r_new] + rows
                if not diagonal:
                    update(zs, rows)
                    return
                if not split:
                    ok = lax.broadcasted_iota(jnp.int32, (t, t), 1) <= lax.broadcasted_iota(jnp.int32, (t, t), 0)
                    zs[-1] = jnp.where(jnp.concatenate([ok, ok], axis=0), zs[-1], NEG)
                    update(zs, rows)
                    return
                base = pl.multiple_of(js[-1] * t, t)
                ck = crm_ref[0, 0, js[-1]]
                tri = lax.broadcasted_iota(jnp.int32, (th, th), 1) <= lax.broadcasted_iota(jnp.int32, (th, th), 0)
                r_left = pl.ds(base, th)
                z_left = logits(k_ref[0, r_left, :], ck[:, :th], offs_m[:, :th])
                lower_rows = (lax.broadcasted_iota(jnp.int32, (2 * t, th), 0) // th) % 2 == 1
                z_left = jnp.where(lower_rows | jnp.concatenate([tri] * 4, axis=0), z_left, NEG)
                r_right = pl.ds(base + th, th)
                q_lo = jnp.concatenate([q2[th:t], q2[t + th:]], axis=0)
                o_lo = jnp.concatenate([offs_m[th:t, :th], offs_m[t + th:, :th]], axis=0)
                s = _dot_nt(q_lo, k_ref[0, r_right, :])
                z_right = jnp.concatenate([s[:th] - ck[0:1, th:] - o_lo[:th], s[th:] - ck[1:2, th:] - o_lo[th:]],
                                          axis=0)
                z_right = jnp.where(jnp.concatenate([tri, tri], axis=0), z_right, NEG)
                update(zs + [z_left], rows + [r_left], (z_right, r_right), upper)

            def body(j, c):
                tiles([j], False)
                return c

            if not fuse_tail:
                lax.fori_loop(n_skip, i, body, 0)
                tiles([i], True)
                return
            n_live = i - n_skip
            lax.fori_loop(n_skip, jnp.maximum(i - 2, n_skip), body, 0)

            @pl.when((n_live >= 2) & old_half_dead)
            def _():
                tiles([i - 2, i - 1, i], True, True)

            @pl.when((n_live >= 2) & jnp.logical_not(old_half_dead))
            def _():
                tiles([i - 2, i - 1, i], True)

            @pl.when(n_live == 1)
            def _():
                tiles([i - 1, i], True)

            @pl.when(n_live == 0)
            def _():
                tiles([i], True)

        def finish(l_a, l_b):
            acc = acc_ref[...]
            out = jnp.where(first, acc[:t] / l_a, acc[t:] / l_b)
            o_ref[0, pl.ds(r0, t), :] = out.astype(o_ref.dtype)

        @pl.when(fast)
        def _():
            ref = jnp.concatenate([ra_ref[pl.ds(r0, t), :], rb_ref[pl.ds(r0, t), :]], axis=0)

            def update(zs, rows, lower=None, upper=None):
                ps = [jnp.exp2(z).astype(BF16) for z in zs]
                p_lo = None if lower is None else jnp.exp2(lower[0]).astype(BF16)
                if upper is not None:
                    p_up = jnp.exp2(upper[0]).astype(BF16)
                    acc_ref[0:th] += _dot(p_up[:th], v_with_ones(upper[1], False))
                    acc_ref[t:t + th] += _dot(p_up[th:], v_with_ones(upper[1], True))
                for half, head_b in ((slice(0, t), False), (slice(t, 2 * t), True)):
                    pv = _dot(ps[0][half], v_with_ones(rows[0], head_b))
                    for p, r in zip(ps[1:], rows[1:]):
                        pv = pv + _dot(p[half], v_with_ones(r, head_b))
                    acc_ref[half] += pv
                if lower is not None:
                    acc_ref[th:t] += _dot(p_lo[:th], v_with_ones(lower[1], False))
                    acc_ref[t + th:] += _dot(p_lo[th:], v_with_ones(lower[1], True))

            run(update, ref, jnp.concatenate([ref] * reps, axis=1), True)

            def denominators(acc, hot):
                hi = acc.astype(BF16)
                lo = (acc - hi.astype(F32)).astype(BF16)
                return _dot(hi, hot) + _dot(lo, hot)

            finish(denominators(acc_ref[0:t], hot_la), denominators(acc_ref[t:], hot_lb))

        @pl.when(jnp.logical_not(fast))
        def _():
            cq2 = head_cols(cc_ref[0, pl.ds(r0, t), :], lane)
            m_ref[...] = jnp.full_like(m_ref, NEG)
            l_ref[...] = jnp.zeros_like(l_ref)

            def update(zs, rows):
                (z,), (r,) = zs, rows
                m_old = m_ref[...]
                m_new = jnp.maximum(m_old, jnp.max(z, axis=1, keepdims=True) + cq2)
                p = jnp.exp2(z - (m_new - cq2))
                alpha = jnp.exp2(m_old - m_new)
                l_ref[...] = alpha * l_ref[...] + jnp.sum(p, axis=1, keepdims=True)
                acc_ref[...] = alpha * acc_ref[...] + _dot(p.astype(BF16), v_ref[0, r, :])
                m_ref[...] = m_new

            run(update, jnp.zeros((2 * t, 1), F32), jnp.zeros((2 * t, 1), F32), False)
            finish(l_ref[0:t], l_ref[t:])

        return carry

    lax.fori_loop(0, nt, main_tile, 0)


def _fox(fq, fk, fv, cum_col, cum_row, batch, seq_len):
    t = FOX_T
    nt = (seq_len - BLOCK) // t
    npairs = FOX_HEADS // 2
    assert 2 * nt + 1 <= LANES
    s_main = nt * t
    r3 = lambda a: a.reshape(batch, seq_len, a.shape[-1])
    cr = cum_row.reshape(batch, npairs, 2, seq_len)
    cr_prefix = cr[..., s_main:]
    cr_main = cr[..., :s_main].reshape(batch, npairs, 2, nt, t).transpose(0, 1, 3, 2, 4)
    th = t // 2
    bounds = jnp.concatenate(
        [cr[..., t - 1:s_main:t], cr[..., seq_len - 1:], cr[..., th - 1:s_main:t],
         jnp.zeros((batch, npairs, 2, LANES - 2 * nt - 1), F32)], axis=-1)
    nt8 = -(-nt // 8) * 8
    starts = jnp.stack([cr[..., 0:s_main:t], cr[..., th:s_main:t]], axis=3)
    starts = jnp.pad(starts, ((0, 0),) * 4 + ((0, nt8 - nt),))
    starts = jnp.broadcast_to(starts[..., None], (batch, npairs, 2, 2, nt8, LANES))
    whole = pl.BlockSpec((1, seq_len, LANES), lambda b, p: (b, 0, p))
    out = pl.pallas_call(
        _fox_kernel,
        grid=(batch, npairs),
        in_specs=[whole, whole, whole,
                  pl.BlockSpec((1, seq_len, LANES), lambda b, p: (b, 0, 0)),
                  pl.BlockSpec((1, 1, 2, BLOCK), lambda b, p: (b, p, 0, 0)),
                  pl.BlockSpec((1, 1, nt, 2, t), lambda b, p: (b, p, 0, 0, 0)),
                  pl.BlockSpec((1, 1, 2, LANES), lambda b, p: (b, p, 0, 0)),
                  pl.BlockSpec((1, 1, 2, 2, nt8, LANES), lambda b, p: (b, p, 0, 0, 0, 0))],
        out_specs=whole,
        out_shape=jax.ShapeDtypeStruct((batch, seq_len, FOX_WIDTH), BF16),
        scratch_shapes=[pltpu.VMEM((s_main, LANES), F32), pltpu.VMEM((s_main, LANES), F32),
                        pltpu.VMEM((nt8, LANES), F32),
                        pltpu.SMEM((4, nt), jnp.int32),
                        pltpu.VMEM((2 * t, 1), F32), pltpu.VMEM((2 * t, 1), F32),
                        pltpu.VMEM((2 * t, LANES), F32)],
        compiler_params=pltpu.CompilerParams(
            dimension_semantics=("parallel", "parallel"), vmem_limit_bytes=VMEM_LIMIT),
    )(r3(fq), r3(fk), r3(fv), cum_col, cr_prefix, cr_main, bounds, starts)
    return out.reshape(batch * seq_len, FOX_WIDTH)


def _post_kernel(of_ref, og_ref, gr_ref, gma_ref, gmb_ref, h_ref, ggla_ref, wfo_ref, wgo_ref,
                 wout_ref, gpost_ref, o_ref):
    tm = h_ref.shape[0]
    halves = (slice(0, tm // 2), slice(tm // 2, tm))
    ggla = ggla_ref[...]
    a = [_dot(of_ref[hs, :], wfo_ref[...]) for hs in halves]
    b = []
    for hs in halves:
        og = og_ref[hs, :]
        heads = [_rms(og[:, hh * GLA_DV:(hh + 1) * GLA_DV], ggla[:, hh * GLA_DV:(hh + 1) * GLA_DV])
                 for hh in range(GLA_HEADS)]
        gr = gr_ref[hs, :].astype(F32)
        o_gla = (jnp.concatenate(heads, axis=1) * (gr * jax.nn.sigmoid(gr))).astype(BF16)
        b.append(_dot(o_gla, wgo_ref[...]))
    mix = []
    for hs, a_h, b_h in zip(halves, a, b):
        y = jax.nn.sigmoid(gma_ref[hs, :].astype(F32)) * a_h + jax.nn.sigmoid(gmb_ref[hs, :].astype(F32)) * b_h
        mix.append(_dot(y.astype(BF16), wout_ref[...]))
    for hs, m_h in zip(halves, mix):
        o_ref[hs, :] = h_ref[hs, :] + _rms(m_h, gpost_ref[...])


def _post(o_fox, o_gla, gr, gma, gmb, h, ggla, wfo, wgo, wout, gpost, layer):
    n = h.shape[0]
    tm = ROW_TM
    row = lambda a: pl.BlockSpec((tm, a.shape[1]), lambda i: (i, 0))
    full = lambda a: (_resident(a.shape[1:], layer) if a.ndim == 3
                      else pl.BlockSpec(a.shape, lambda i: (0, 0)))
    args = (o_fox, o_gla, gr, gma, gmb, h, ggla, wfo, wgo, wout, gpost)
    return pl.pallas_call(
        _post_kernel,
        grid=(n // tm,),
        in_specs=[row(a) for a in args[:6]] + [full(a) for a in args[6:]],
        out_specs=pl.BlockSpec((tm, D_MODEL), lambda i: (i, 0)),
        out_shape=jax.ShapeDtypeStruct((n, D_MODEL), F32),
        compiler_params=pltpu.CompilerParams(
            dimension_semantics=("parallel",), vmem_limit_bytes=VMEM_LIMIT),
    )(*args)


def kernel(x, meta_tokens, w_in, w_alpha_up, b_alpha, b_f, g_gla_out, w_fox_o, w_gla_o, w_out,
           w_ffn1_gu, w_ffn1_down, w_ffn2_gu, w_ffn2_down,
           g_pre_ffn1, g_post_ffn1, g_pre_mix, g_post_mix, g_pre_ffn2, g_post_ffn2):
    batch, seq, d = x.shape
    seq_len = seq + BLOCK
    depth = w_in.shape[0]
    assert d == D_MODEL and (seq_len - BLOCK) % FOX_T == 0 and seq_len % ROW_TM == 0
    assert seq_len % FFN_TM == 0 and seq % FFN_TM_MAIN == 0

    pad = jnp.zeros((batch, N_PAD, d), x.dtype)
    meta = jnp.broadcast_to(meta_tokens.astype(x.dtype)[None], (batch, N_META, d))
    prefix = jnp.concatenate([pad, meta], axis=1)
    flat = lambda a: a.reshape(batch * seq_len, a.shape[-1])

    row2 = lambda a: a.reshape(1, -1).astype(F32)
    w_perm = w_in.astype(BF16)
    w1_gu, w1_down, w2_gu, w2_down = w_ffn1_gu, w_ffn1_down, w_ffn2_gu, w_ffn2_down
    wfo, wgo, wout = w_fox_o.astype(BF16), w_gla_o.astype(BF16), w_out.astype(BF16)
    for l in range(depth):
        wa_pad = jnp.zeros((LANES, GLA_KWIDTH), F32).at[
            FOX_HEADS:FOX_HEADS + GLA_GATE_RANK].set(w_alpha_up[l]).astype(BF16)
        bf_pad = jnp.zeros((1, LANES), F32).at[0, :FOX_HEADS].set(b_f[l])

        if l == 0:
            h = _ffn_first(x, prefix, row2(g_pre_ffn1[l]), w1_gu, w1_down, row2(g_post_ffn1[l]), l)
        else:
            h = _ffn(h, row2(g_pre_ffn1[l]), w1_gu, w1_down, row2(g_post_ffn1[l]), l)
        fq, fk, fv, gq, gk, gv, gr, gma, gmb, la, lf = _proj(
            flat(h), row2(g_pre_mix[l]), w_perm, wa_pad, row2(b_alpha[l]), bf_pad, seq_len, l)
        o_gla, cum_col, cum_row = _gla(gq, gk, gv, la, lf, batch, seq_len)
        o_fox = _fox(fq, fk, fv, cum_col, cum_row, batch, seq_len)
        h = _post(o_fox, flat(o_gla), gr, gma, gmb, flat(h),
                  row2(g_gla_out[l]), wfo, wgo, wout, row2(g_post_mix[l]), l).reshape(batch, seq_len, d)
        h = _ffn(h, row2(g_pre_ffn2[l]), w2_gu, w2_down, row2(g_post_ffn2[l]), l,
                 main_rows=seq if l == depth - 1 else None)
    return h
```

```python
import functools

import jax
import jax.numpy as jnp
import numpy as np
from jax import lax
from jax.experimental import pallas as pl
from jax.experimental.pallas import tpu as pltpu

D_MODEL = 1024
N_META = 16
BLOCK = 128
N_PAD = BLOCK - N_META
FOX_HEADS = 8
FOX_HEAD_DIM = 64
FOX_WIDTH = FOX_HEADS * FOX_HEAD_DIM
GLA_HEADS = 4
GLA_WIDTH = 512
GLA_DV = 128
GLA_DK = 64
GLA_KWIDTH = GLA_HEADS * GLA_DK
GLA_GATE_RANK = 16
GLA_GATE_TEMP = 16.0
D_FF = 2816
RMS_EPS = 1e-6

LANES = 128
SUB = 16
N_SUB = BLOCK // SUB
EXP_CAP = 80.0
NEG = -1e30

FFN_TM = 640
FFN_TM_MAIN = 512
FFN_TF = 256
FFN_W_ROWS = 256
ROW_TM = 640
FOX_T = 512
LOG2E = 1.4426950408889634
PRUNE_BITS = 160.0
NORM_UP = 1.01
FAST_MAX_BITS = 50.0
VMEM_LIMIT = 56 * 1024 * 1024

BF16 = jnp.bfloat16
F32 = jnp.float32

C_FQ, C_FK, C_FV = 0, 512, 1024
C_GQ, C_GK, C_GV = 1536, 1792, 2048
C_GR, C_GMA, C_GMB = 2560, 3072, 4096
C_SMALL = 5120
N_IN = 5144
N_IN_PERM = 5248
W_IN_MOVES = ((0, 0, 1536), (1536, 1544, 1024), (2560, 2584, 2560), (5120, 1536, 8), (5128, 2568, 16))


def _dot(a, b):
    return jnp.dot(a, b, preferred_element_type=F32)


def _dot_nt(a, b):
    return lax.dot_general(a, b, (((1,), (1,)), ((), ())), preferred_element_type=F32)


def _dot_tn(a, b):
    return lax.dot_general(a, b, (((0,), (0,)), ((), ())), preferred_element_type=F32)


def _rms(x, g):
    return x * lax.rsqrt(jnp.mean(x * x, axis=-1, keepdims=True) + RMS_EPS) * g


def _log_sigmoid(x):
    return jnp.minimum(x, 0.0) - jnp.log1p(jnp.exp(-jnp.abs(x)))


def _split2(x):
    hi = x.astype(BF16)
    lo = (x - hi.astype(F32)).astype(BF16)
    return hi, lo


def _dot_exact_lhs(m, parts):
    return _dot(m, parts[0]) + _dot(m, parts[1])


def _ffn_rows(h, gpre_ref, wgu_ref, wd_ref, gpost_ref, xn_ref, a_ref):
    m = h.shape[0]
    parts = (slice(0, m // 2), slice(m // 2, m)) if m >= 2 * FFN_TF else (slice(0, m),)
    out = []
    for rows in parts:
        h_p = h[rows]
        xn_ref[rows] = _rms(h_p, gpre_ref[...]).astype(BF16)
        for c in range(D_FF // FFN_TF):
            lo = c * FFN_TF
            g = _dot(xn_ref[rows], wgu_ref[:, lo:lo + FFN_TF])
            u = _dot(xn_ref[rows], wgu_ref[:, D_FF + lo:D_FF + lo + FFN_TF])
            a_ref[rows, lo:lo + FFN_TF] = (g * jax.nn.sigmoid(g) * u).astype(BF16)
        y = _dot(a_ref[rows], wd_ref[...])
        out.append(h_p + 0.5 * _rms(y, gpost_ref[...]))
    return out[0] if len(out) == 1 else jnp.concatenate(out, axis=0)


def _stream_weight(w_hbm, layer, n_rows, stage_ref, sem, sem_row, consume):
    rows = stage_ref.shape[1]
    n = n_rows // rows

    def chunk(c):
        return pltpu.make_async_copy(w_hbm.at[layer, pl.ds(c * rows, rows), :], stage_ref.at[c % 2],
                                     sem.at[sem_row, c % 2])

    chunk(0).start()
    for c in range(n):
        if c + 1 < n:
            chunk(c + 1).start()
        chunk(c).wait()
        consume(slice(c * rows, (c + 1) * rows), stage_ref.at[c % 2])


def _cast_into(dst_ref):
    def consume(rows, staged):
        dst_ref[rows, :] = staged[...].astype(BF16)
    return consume


def _ffn_weights(wgu_hbm, wd_hbm, layer, wgu_ref, wd_ref, sgu_ref, sd_ref, sem):
    @pl.when((pl.program_id(0) == 0) & (pl.program_id(1) == 0))
    def _():
        _stream_weight(wgu_hbm, layer, D_MODEL, sgu_ref, sem, 0, _cast_into(wgu_ref))
        _stream_weight(wd_hbm, layer, D_FF, sd_ref, sem, 1, _cast_into(wd_ref))


def _ffn_kernel(h_ref, gpre_ref, wgu_hbm, wd_hbm, gpost_ref, o_ref,
                xn_ref, a_ref, wgu_ref, wd_ref, sgu_ref, sd_ref, sem, *, layer):
    _ffn_weights(wgu_hbm, wd_hbm, layer, wgu_ref, wd_ref, sgu_ref, sd_ref, sem)
    o_ref[...] = _ffn_rows(h_ref[...], gpre_ref, wgu_ref, wd_ref, gpost_ref, xn_ref, a_ref)


def _ffn_first_kernel(x_ref, p_ref, gpre_ref, wgu_hbm, wd_hbm, gpost_ref, o_ref,
                      xn_ref, a_ref, wgu_ref, wd_ref, sgu_ref, sd_ref, sem, *, layer):
    _ffn_weights(wgu_hbm, wd_hbm, layer, wgu_ref, wd_ref, sgu_ref, sd_ref, sem)
    last = pl.program_id(1) == pl.num_programs(1) - 1

    @pl.when(jnp.logical_not(last))
    def _():
        o_ref[...] = _ffn_rows(x_ref[...], gpre_ref, wgu_ref, wd_ref, gpost_ref, xn_ref, a_ref)

    @pl.when(last)
    def _():
        o_ref[0:BLOCK] = _ffn_rows(p_ref[...], gpre_ref, wgu_ref, wd_ref, gpost_ref, xn_ref, a_ref)


def _resident(shape, layer):
    return pl.BlockSpec((None,) + shape, lambda *_: (layer,) + (0,) * len(shape),
                        pipeline_mode=pl.Buffered(1))


def _ffn_scratch(tm):
    return [pltpu.VMEM((tm, D_MODEL), BF16), pltpu.VMEM((tm, D_FF), BF16),
            pltpu.VMEM((D_MODEL, 2 * D_FF), BF16), pltpu.VMEM((D_FF, D_MODEL), BF16),
            pltpu.VMEM((2, FFN_W_ROWS, 2 * D_FF), F32), pltpu.VMEM((2, FFN_W_ROWS, D_MODEL), F32),
            pltpu.SemaphoreType.DMA((2, 2))]


def _ffn(h, gpre, w_gu, w_down, gpost, layer, main_rows=None):
    batch, seq_len, _ = h.shape
    if main_rows is None:
        tm, rows = FFN_TM, seq_len
    else:
        tm, rows = FFN_TM_MAIN, main_rows
    tile = pl.BlockSpec((None, tm, D_MODEL), lambda b, i: (b, i, 0))
    vec = pl.BlockSpec((1, D_MODEL), lambda b, i: (0, 0))
    hbm = pl.BlockSpec(memory_space=pl.ANY)
    return pl.pallas_call(
        functools.partial(_ffn_kernel, layer=layer),
        grid=(batch, rows // tm),
        in_specs=[tile, vec, hbm, hbm, vec],
        out_specs=tile,
        out_shape=jax.ShapeDtypeStruct((batch, rows, D_MODEL), F32),
        scratch_shapes=_ffn_scratch(tm),
        compiler_params=pltpu.CompilerParams(
            dimension_semantics=("arbitrary", "arbitrary"), vmem_limit_bytes=VMEM_LIMIT),
    )(h, gpre, w_gu, w_down, gpost)


def _ffn_first(x, prefix, gpre, w_gu, w_down, gpost, layer):
    batch, seq, _ = x.shape
    tm = FFN_TM_MAIN
    n_main = seq // tm
    vec = pl.BlockSpec((1, D_MODEL), lambda b, i: (0, 0))
    hbm = pl.BlockSpec(memory_space=pl.ANY)
    return pl.pallas_call(
        functools.partial(_ffn_first_kernel, layer=layer),
        grid=(batch, n_main + 1),
        in_specs=[pl.BlockSpec((None, tm, D_MODEL), lambda b, i: (b, jnp.minimum(i, n_main - 1), 0)),
                  pl.BlockSpec((None, BLOCK, D_MODEL), lambda b, i: (b, 0, 0)),
                  vec, hbm, hbm, vec],
        out_specs=pl.BlockSpec((None, tm, D_MODEL), lambda b, i: (b, i, 0)),
        out_shape=jax.ShapeDtypeStruct((batch, seq + BLOCK, D_MODEL), F32),
        scratch_shapes=_ffn_scratch(tm),
        compiler_params=pltpu.CompilerParams(
            dimension_semantics=("arbitrary", "arbitrary"), vmem_limit_bytes=VMEM_LIMIT),
    )(x, prefix, gpre, w_gu, w_down, gpost)


def _proj_kernel(h_ref, g_ref, w_ref, wa_ref, ba_ref, bf_ref,
                 fq_ref, fk_ref, fv_ref, gq_ref, gk_ref, gv_ref, gr_ref, gma_ref, gmb_ref,
                 la_ref, lf_ref, ws_ref, *, tiles_per_seq, main_rows):
    @pl.when(pl.program_id(0) == 0)
    def _():
        for dst, src, width in W_IN_MOVES:
            ws_ref[:, dst:dst + width] = w_ref[:, src:src + width]
        ws_ref[:, N_IN:] = jnp.zeros((D_MODEL, N_IN_PERM - N_IN), BF16)

    w_ref = ws_ref
    tm = h_ref.shape[0]
    m = tm // 2
    halves = (slice(0, m), slice(m, tm))
    xns = {}
    for rows in halves:
        if rows.start not in xns:
            xns[rows.start] = _rms(h_ref[rows, :], g_ref[...]).astype(BF16)
        xn = xns[rows.start]
        pos = ((pl.program_id(0) % tiles_per_seq) * tm + rows.start
               + lax.broadcasted_iota(jnp.int32, (m, 1), 0))
        valid = jnp.logical_or(pos < main_rows, pos >= main_rows + N_PAD).astype(F32)

        def proj(c0, width):
            return _dot(xn, w_ref[:, c0:c0 + width])

        small = proj(C_SMALL, LANES)
        if rows.start == 0:
            xns[m] = _rms(h_ref[halves[1], :], g_ref[...]).astype(BF16)
        fq_ref[rows, :] = (proj(C_FQ, FOX_WIDTH) * (FOX_HEAD_DIM ** -0.5 * LOG2E)).astype(BF16)
        fk_ref[rows, :] = proj(C_FK, FOX_WIDTH).astype(BF16)
        lane = lax.broadcasted_iota(jnp.int32, (m, LANES), 1)
        lf = _log_sigmoid(small + bf_ref[...]) * (valid * LOG2E)
        lf_ref[rows, :] = jnp.where(lane < FOX_HEADS, lf, 0.0)
        xa = _dot(small.astype(BF16), wa_ref[...]) + ba_ref[...]
        la_ref[rows, :] = _log_sigmoid(xa) * (valid * (1.0 / GLA_GATE_TEMP))
        fv_ref[rows, :] = proj(C_FV, FOX_WIDTH).astype(BF16)
        gq_ref[rows, :] = proj(C_GQ, GLA_KWIDTH) * (GLA_DK ** -0.5)
        gk_ref[rows, :] = proj(C_GK, GLA_KWIDTH) * valid
        gv_ref[rows, :] = proj(C_GV, GLA_WIDTH).astype(BF16)
        gr_ref[rows, :] = proj(C_GR, GLA_WIDTH).astype(BF16)
        gma_ref[rows, :] = proj(C_GMA, D_MODEL).astype(BF16)
        gmb_ref[rows, :] = proj(C_GMB, D_MODEL).astype(BF16)


def _proj(h, g, w_perm, wa_pad, ba, bf_pad, seq_len, layer):
    n = h.shape[0]
    tm = ROW_TM
    row = lambda width: pl.BlockSpec((tm, width), lambda i: (i, 0))
    full = lambda a: (_resident(a.shape[1:], layer) if a.ndim == 3
                      else pl.BlockSpec(a.shape, lambda i: (0, 0)))
    outs = [
        (FOX_WIDTH, BF16), (FOX_WIDTH, BF16), (FOX_WIDTH, BF16),
        (GLA_KWIDTH, F32), (GLA_KWIDTH, F32), (GLA_WIDTH, BF16),
        (GLA_WIDTH, BF16), (D_MODEL, BF16), (D_MODEL, BF16),
        (GLA_KWIDTH, F32), (LANES, F32),
    ]
    return pl.pallas_call(
        functools.partial(_proj_kernel, tiles_per_seq=seq_len // tm, main_rows=seq_len - BLOCK),
        grid=(n // tm,),
        in_specs=[row(D_MODEL), full(g), full(w_perm), full(wa_pad), full(ba), full(bf_pad)],
        out_specs=[row(w) for w, _ in outs],
        out_shape=[jax.ShapeDtypeStruct((n, w), dt) for w, dt in outs],
        scratch_shapes=[pltpu.VMEM((D_MODEL, N_IN_PERM), BF16)],
        compiler_params=pltpu.CompilerParams(
            dimension_semantics=("arbitrary",), vmem_limit_bytes=VMEM_LIMIT),
    )(h, g, w_perm, wa_pad, ba, bf_pad)


def _gla_kernel(q_ref, k_ref, v_ref, la_ref, lf_ref, o_ref, cc_ref, cr_ref, st_ref, carry_ref):
    @pl.when(pl.program_id(0) == 0)
    def _():
        st_ref[...] = jnp.zeros_like(st_ref)
        carry_ref[...] = jnp.zeros_like(carry_ref)

    row = lax.broadcasted_iota(jnp.int32, (BLOCK, BLOCK), 0)
    col = lax.broadcasted_iota(jnp.int32, (BLOCK, BLOCK), 1)
    causal = row >= col
    tril = causal.astype(BF16)
    tril_sub = (causal & ((row // SUB) == (col // SUB))).astype(BF16)
    row_w = lax.broadcasted_iota(jnp.int32, (BLOCK, GLA_KWIDTH), 0)
    row_blk = row // SUB

    batches = range(q_ref.shape[0])
    problems = [(b, h) for b in batches for h in range(GLA_HEADS)]

    cb_all, cw_all = [], []
    for b in batches:
        cl = _dot_exact_lhs(tril, _split2(lf_ref[b])) + carry_ref[b]
        carry_ref[b] = cl[BLOCK - 1:BLOCK, :]
        cc_ref[b] = cl
        cr_ref[b] = cl.T[0:FOX_HEADS, :]
        la_parts = _split2(la_ref[b])
        cb_all.append(_dot_exact_lhs(tril, la_parts))
        cw_all.append(_dot_exact_lhs(tril_sub, la_parts))

    ops = {}
    for b in batches:
        cb, cw = cb_all[b], cw_all[b]
        last = cb[BLOCK - 1:BLOCK, :]
        q = q_ref[b]
        k = k_ref[b]
        q_glob = (q * jnp.exp(cb)).astype(BF16)
        q_loc = (q * jnp.exp(cw)).astype(BF16)
        k_end = (k * jnp.exp(last - cb)).astype(BF16)
        decay = jnp.exp(last)
        k_sub = []
        for i in range(N_SUB):
            ref = cb[i * SUB - 1:i * SUB, :] if i > 0 else jnp.zeros_like(last)
            e = jnp.exp(jnp.minimum(ref - cb, EXP_CAP))
            k_sub.append(jnp.where(row_w < (i + 1) * SUB, k * e, 0.0).astype(BF16))
        for pair in range(GLA_HEADS // 2):
            sl = slice(pair * LANES, (pair + 1) * LANES)
            k_cat = jnp.concatenate([ks[:, sl] for ks in k_sub], axis=1)
            for half in range(2):
                h = 2 * pair + half
                head_lanes = (col < GLA_DK) if half == 0 else (col >= GLA_DK)
                q_h = jnp.where(head_lanes, q_loc[:, sl], 0.0)
                q_parts = []
                for p in range(N_SUB // 2):
                    r = slice(2 * p * SUB, (2 * p + 2) * SUB)
                    q_parts.append(jnp.concatenate(
                        [jnp.where(row_blk[r] == 2 * p, q_h[r], 0.0), jnp.where(row_blk[r] == 2 * p + 1, q_h[r], 0.0)],
                        axis=1))
                ops[b, h] = (q_parts, k_cat, jnp.where(head_lanes, q_glob[:, sl], 0.0), k_end[:, sl],
                             decay[:, sl], v_ref[b, :, h * GLA_DV:(h + 1) * GLA_DV])

    att, inter, upd = {}, {}, {}
    for p in problems:
        q_parts, k_cat, q_g, k_e, _, v_h = ops[p]
        att[p] = jnp.concatenate(
            [_dot_nt(q_p, k_cat[:, 2 * i * LANES:(2 * i + 2) * LANES]) for i, q_p in enumerate(q_parts)],
            axis=0)
        inter[p] = _dot_nt(q_g, st_ref[p].astype(BF16))
        upd[p] = _dot_tn(v_h, k_e)

    for p in problems:
        b, h = p
        a = jnp.where(causal, att[p], 0.0).astype(BF16)
        o_ref[b, :, h * GLA_DV:(h + 1) * GLA_DV] = inter[p] + _dot(a, ops[p][5])
        st_ref[p] = st_ref[p] * ops[p][4] + upd[p]


def _gla(gq, gk, gv, la, lf, batch, seq_len):
    nc = seq_len // BLOCK
    chunk = lambda c: (c + nc - 1) % nc
    blk = lambda width: pl.BlockSpec((batch, BLOCK, width), lambda c: (0, chunk(c), 0))
    r3 = lambda a: a.reshape(batch, seq_len, a.shape[-1])
    return pl.pallas_call(
        _gla_kernel,
        grid=(nc,),
        in_specs=[blk(GLA_KWIDTH), blk(GLA_KWIDTH), blk(GLA_WIDTH), blk(GLA_KWIDTH), blk(LANES)],
        out_specs=[blk(GLA_WIDTH), blk(LANES),
                   pl.BlockSpec((batch, FOX_HEADS, BLOCK), lambda c: (0, 0, chunk(c)))],
        out_shape=[jax.ShapeDtypeStruct((batch, seq_len, GLA_WIDTH), F32),
                   jax.ShapeDtypeStruct((batch, seq_len, LANES), F32),
                   jax.ShapeDtypeStruct((batch, FOX_HEADS, seq_len), F32)],
        scratch_shapes=[pltpu.VMEM((batch, GLA_HEADS, GLA_DV, LANES), F32),
                        pltpu.VMEM((batch, 1, LANES), F32)],
        compiler_params=pltpu.CompilerParams(
            dimension_semantics=("arbitrary",), vmem_limit_bytes=VMEM_LIMIT),
    )(r3(gq), r3(gk), r3(gv), r3(la), r3(lf))


def _fox_kernel(q_ref, k_ref, v_ref, cc_ref, crp_ref, crm_ref, cb_ref, cs_ref, o_ref,
                ra_ref, rb_ref, ub_ref, plan_ref, m_ref, l_ref, acc_ref):
    t = FOX_T
    nt = crm_ref.shape[2]
    s_main = nt * t
    pre = slice(s_main, s_main + BLOCK)
    pair = pl.program_id(1)

    def stack_heads(x, lane_first):
        return jnp.concatenate([jnp.where(lane_first, x, 0.0), jnp.where(lane_first, 0.0, x)], axis=0)

    def head_cols(cc, lane):
        return jnp.concatenate(
            [jnp.sum(jnp.where(lane == 2 * pair + hh, cc, 0.0), axis=1, keepdims=True)
             for hh in range(2)], axis=0)

    lane_b = lax.broadcasted_iota(jnp.int32, (BLOCK, LANES), 1)
    first_b = lane_b < FOX_HEAD_DIM
    kp = k_ref[0, pre, :]
    vp = v_ref[0, pre, :]
    ckp = crp_ref[0, 0]
    row_b = lax.broadcasted_iota(jnp.int32, (BLOCK, BLOCK), 0)
    ok_b = (lane_b <= row_b) & ((lane_b >= N_PAD) | (lane_b == row_b))
    ok_b = jnp.concatenate([ok_b, ok_b], axis=0)
    s = _dot_nt(stack_heads(q_ref[0, pre, :], first_b), kp)
    z = jnp.concatenate([s[:BLOCK] - ckp[0:1], s[BLOCK:] - ckp[1:2]], axis=0)
    z = jnp.where(ok_b, z, NEG)
    cq2 = head_cols(cc_ref[0, pre, :], lane_b)
    m = jnp.max(z, axis=1, keepdims=True) + cq2
    p = jnp.exp2(z - (m - cq2))
    out = _dot(p.astype(BF16), vp) / jnp.sum(p, axis=1, keepdims=True)
    o_ref[0, pre, :] = jnp.where(first_b, out[:BLOCK], out[BLOCK:]).astype(o_ref.dtype)

    def v_with_ones(rows, head_b):
        v = v_ref[0, rows, :]
        lane_v = lax.broadcasted_iota(jnp.int32, v.shape, 1)
        if head_b:
            return jnp.where(lane_v >= FOX_HEAD_DIM, v, jnp.where(lane_v == 0, 1.0, 0.0).astype(BF16))
        return jnp.where(lane_v < FOX_HEAD_DIM, v, jnp.where(lane_v == FOX_HEAD_DIM, 1.0, 0.0).astype(BF16))

    lane = lax.broadcasted_iota(jnp.int32, (t, LANES), 1)
    first = lane < FOX_HEAD_DIM
    row_l = lax.broadcasted_iota(jnp.int32, (LANES, LANES), 0)
    row_w = lax.broadcasted_iota(jnp.int32, (LANES, 2 * LANES), 0)
    second = lax.broadcasted_iota(jnp.int32, (LANES, 2 * LANES), 1) >= LANES
    ind_ab = ((row_w >= FOX_HEAD_DIM) == second).astype(BF16)
    hot_ab = (row_w == 2 * pair + second.astype(jnp.int32)).astype(BF16)
    hot_la = (row_l == FOX_HEAD_DIM).astype(BF16)
    hot_lb = (row_l == 0).astype(BF16)
    reps = t // LANES
    th = t // 2

    sel = lax.broadcasted_iota(jnp.int32, (8, LANES), 0) == lax.broadcasted_iota(
        jnp.int32, (8, LANES), 1) // FOX_HEAD_DIM
    k_all = k_ref[0]
    k_max = jnp.sqrt(jnp.max(_dot_nt(sel.astype(BF16), k_all * k_all), axis=1, keepdims=True))

    def prepare(i, carry):
        rows = pl.ds(pl.multiple_of(i * t, t), t)
        q = q_ref[0, rows, :]
        nq = _dot(q * q, ind_ab)
        cq = _dot(cc_ref[0, rows, :].astype(BF16), hot_ab)
        ub_a = jnp.sqrt(nq[:, :LANES]) * (k_max[0:1] * NORM_UP)
        ub_b = jnp.sqrt(nq[:, LANES:]) * (k_max[1:2] * NORM_UP)
        ra_ref[rows, :] = ub_a - cq[:, :LANES]
        rb_ref[rows, :] = ub_b - cq[:, LANES:]
        ub_ref[pl.ds(i, 1), :] = jnp.max(jnp.maximum(ub_a, ub_b), axis=0, keepdims=True)
        return carry

    lax.fori_loop(0, nt, prepare, 0)

    ub_t = ub_ref[0:nt, :]
    fast_v = ub_t <= FAST_MAX_BITS
    bound = jnp.where(fast_v, PRUNE_BITS, PRUNE_BITS + 2.0 * ub_t)
    cb = cb_ref[0, 0]
    dead = ((cs_ref[0, 0, 0, 0, 0:nt, :] - cb[0:1]) < -bound) & ((cs_ref[0, 0, 1, 0, 0:nt, :] - cb[1:2]) < -bound)
    tile_q = lax.broadcasted_iota(jnp.int32, (nt, LANES), 0)
    tile_k = lax.broadcasted_iota(jnp.int32, (nt, LANES), 1)
    n_skip_v = jnp.sum(jnp.where(dead & (tile_k < tile_q), 1, 0), axis=1, keepdims=True)
    skip_pre_v = jnp.sum(jnp.where(dead & (tile_k == nt), 1, 0), axis=1, keepdims=True)
    dead_lo = ((cs_ref[0, 0, 0, 1, 0:nt, :] - cb[0:1]) < -bound) & ((cs_ref[0, 0, 1, 1, 0:nt, :] - cb[1:2]) < -bound)
    old_dead_v = jnp.sum(jnp.where(dead_lo & (tile_k == tile_q + (nt - 1)), 1, 0), axis=1, keepdims=True)
    fast_i = jnp.where(fast_v[:, 0:1], 1, 0)
    tile_1 = tile_q[:, 0:1]
    for i in range(nt):
        for row, vec in enumerate((fast_i, n_skip_v, skip_pre_v, old_dead_v)):
            plan_ref[row, i] = jnp.sum(jnp.where(tile_1 == i, vec, 0))

    def main_tile(i, carry):
        r0 = pl.multiple_of(i * t, t)
        q2 = stack_heads(q_ref[0, pl.ds(r0, t), :], first)
        fast = plan_ref[0, i] == 1
        n_skip = plan_ref[1, i]
        skip_prefix = plan_ref[2, i]
        old_half_dead = plan_ref[3, i] == 1

        def logits(kt, ck, offs):
            s = _dot_nt(q2, kt)
            return jnp.concatenate([s[:t] - ck[0:1] - offs[:t], s[t:] - ck[1:2] - offs[t:]], axis=0)

        def run(update, offs_p, offs_m, fuse_tail):
            acc_ref[...] = jnp.zeros_like(acc_ref)

            @pl.when(skip_prefix == 0)
            def _():
                z = logits(kp, ckp, offs_p)
                lane_k = lax.broadcasted_iota(jnp.int32, (2 * t, BLOCK), 1)
                update([jnp.where(lane_k >= N_PAD, z, NEG)], [pre])

            def tiles(js, diagonal, old_half_dead=False):
                split = diagonal and fuse_tail
                full = js[:-1] if split else js
                upper = None
                if old_half_dead:
                    full = full[1:]
                    base = pl.multiple_of(js[0] * t, t)
                    ck = crm_ref[0, 0, js[0]]
                    r_old = pl.ds(base, th)
                    q_up = jnp.concatenate([q2[0:th], q2[t:t + th]], axis=0)
                    o_up = jnp.concatenate([offs_m[0:th, :th], offs_m[t:t + th, :th]], axis=0)
                    s = _dot_nt(q_up, k_ref[0, r_old, :])
                    upper = (jnp.concatenate([s[:th] - ck[0:1, :th] - o_up[:th], s[th:] - ck[1:2, :th] - o_up[th:]],
                                             axis=0), r_old)
                    r_new = pl.ds(base + th, th)
                    z_new = logits(k_ref[0, r_new, :], ck[:, th:], offs_m[:, :th])
                rows = [pl.ds(pl.multiple_of(j * t, t), t) for j in full]
                zs = [logits(k_ref[0, r, :], crm_ref[0, 0, j], offs_m) for j, r in zip(full, rows)]
                if old_half_dead:
                    zs, rows = [z_new] + zs, [r_new] + rows
                if not diagonal:
                    update(zs, rows)
                    return
                if not split:
                    ok = lax.broadcasted_iota(jnp.int32, (t, t), 1) <= lax.broadcasted_iota(jnp.int32, (t, t), 0)
                    zs[-1] = jnp.where(jnp.concatenate([ok, ok], axis=0), zs[-1], NEG)
                    update(zs, rows)
                    return
                base = pl.multiple_of(js[-1] * t, t)
                ck = crm_ref[0, 0, js[-1]]
                tri = lax.broadcasted_iota(jnp.int32, (th, th), 1) <= lax.broadcasted_iota(jnp.int32, (th, th), 0)
                r_left = pl.ds(base, th)
                z_left = logits(k_ref[0, r_left, :], ck[:, :th], offs_m[:, :th])
                lower_rows = (lax.broadcasted_iota(jnp.int32, (2 * t, th), 0) // th) % 2 == 1
                z_left = jnp.where(lower_rows | jnp.concatenate([tri] * 4, axis=0), z_left, NEG)
                r_right = pl.ds(base + th, th)
                q_lo = jnp.concatenate([q2[th:t], q2[t + th:]], axis=0)
                o_lo = jnp.concatenate([offs_m[th:t, :th], offs_m[t + th:, :th]], axis=0)
                s = _dot_nt(q_lo, k_ref[0, r_right, :])
                z_right = jnp.concatenate([s[:th] - ck[0:1, th:] - o_lo[:th], s[th:] - ck[1:2, th:] - o_lo[th:]],
                                          axis=0)
                z_right = jnp.where(jnp.concatenate([tri, tri], axis=0), z_right, NEG)
                update(zs + [z_left], rows + [r_left], (z_right, r_right), upper)

            def body(j, c):
                tiles([j], False)
                return c

            if not fuse_tail:
                lax.fori_loop(n_skip, i, body, 0)
                tiles([i], True)
                return
            n_live = i - n_skip
            lax.fori_loop(n_skip, jnp.maximum(i - 2, n_skip), body, 0)

            @pl.when((n_live >= 2) & old_half_dead)
            def _():
                tiles([i - 2, i - 1, i], True, True)

            @pl.when((n_live >= 2) & jnp.logical_not(old_half_dead))
            def _():
                tiles([i - 2, i - 1, i], True)

            @pl.when(n_live == 1)
            def _():
                tiles([i - 1, i], True)

            @pl.when(n_live == 0)
            def _():
                tiles([i], True)

        def finish(l_a, l_b):
            acc = acc_ref[...]
            out = jnp.where(first, acc[:t] / l_a, acc[t:] / l_b)
            o_ref[0, pl.ds(r0, t), :] = out.astype(o_ref.dtype)

        @pl.when(fast)
        def _():
            ref = jnp.concatenate([ra_ref[pl.ds(r0, t), :], rb_ref[pl.ds(r0, t), :]], axis=0)

            def update(zs, rows, lower=None, upper=None):
                ps = [jnp.exp2(z).astype(BF16) for z in zs]
                p_lo = None if lower is None else jnp.exp2(lower[0]).astype(BF16)
                if upper is not None:
                    p_up = jnp.exp2(upper[0]).astype(BF16)
                    acc_ref[0:th] += _dot(p_up[:th], v_with_ones(upper[1], False))
                    acc_ref[t:t + th] += _dot(p_up[th:], v_with_ones(upper[1], True))
                for half, head_b in ((slice(0, t), False), (slice(t, 2 * t), True)):
                    pv = _dot(ps[0][half], v_with_ones(rows[0], head_b))
                    for p, r in zip(ps[1:], rows[1:]):
                        pv = pv + _dot(p[half], v_with_ones(r, head_b))
                    acc_ref[half] += pv
                if lower is not None:
                    acc_ref[th:t] += _dot(p_lo[:th], v_with_ones(lower[1], False))
                    acc_ref[t + th:] += _dot(p_lo[th:], v_with_ones(lower[1], True))

            run(update, ref, jnp.concatenate([ref] * reps, axis=1), True)

            def denominators(acc, hot):
                hi = acc.astype(BF16)
                lo = (acc - hi.astype(F32)).astype(BF16)
                return _dot(hi, hot) + _dot(lo, hot)

            finish(denominators(acc_ref[0:t], hot_la), denominators(acc_ref[t:], hot_lb))

        @pl.when(jnp.logical_not(fast))
        def _():
            cq2 = head_cols(cc_ref[0, pl.ds(r0, t), :], lane)
            m_ref[...] = jnp.full_like(m_ref, NEG)
            l_ref[...] = jnp.zeros_like(l_ref)

            def update(zs, rows):
                (z,), (r,) = zs, rows
                m_old = m_ref[...]
                m_new = jnp.maximum(m_old, jnp.max(z, axis=1, keepdims=True) + cq2)
                p = jnp.exp2(z - (m_new - cq2))
                alpha = jnp.exp2(m_old - m_new)
                l_ref[...] = alpha * l_ref[...] + jnp.sum(p, axis=1, keepdims=True)
                acc_ref[...] = alpha * acc_ref[...] + _dot(p.astype(BF16), v_ref[0, r, :])
                m_ref[...] = m_new

            run(update, jnp.zeros((2 * t, 1), F32), jnp.zeros((2 * t, 1), F32), False)
            finish(l_ref[0:t], l_ref[t:])

        return carry

    lax.fori_loop(0, nt, main_tile, 0)


def _fox(fq, fk, fv, cum_col, cum_row, batch, seq_len):
    t = FOX_T
    nt = (seq_len - BLOCK) // t
    npairs = FOX_HEADS // 2
    assert 2 * nt + 1 <= LANES
    s_main = nt * t
    r3 = lambda a: a.reshape(batch, seq_len, a.shape[-1])
    cr = cum_row.reshape(batch, npairs, 2, seq_len)
    cr_prefix = cr[..., s_main:]
    cr_main = cr[..., :s_main].reshape(batch, npairs, 2, nt, t).transpose(0, 1, 3, 2, 4)
    th = t // 2
    bounds = jnp.concatenate(
        [cr[..., t - 1:s_main:t], cr[..., seq_len - 1:], cr[..., th - 1:s_main:t],
         jnp.zeros((batch, npairs, 2, LANES - 2 * nt - 1), F32)], axis=-1)
    nt8 = -(-nt // 8) * 8
    starts = jnp.stack([cr[..., 0:s_main:t], cr[..., th:s_main:t]], axis=3)
    starts = jnp.pad(starts, ((0, 0),) * 4 + ((0, nt8 - nt),))
    starts = jnp.broadcast_to(starts[..., None], (batch, npairs, 2, 2, nt8, LANES))
    whole = pl.BlockSpec((1, seq_len, LANES), lambda b, p: (b, 0, p))
    out = pl.pallas_call(
        _fox_kernel,
        grid=(batch, npairs),
        in_specs=[whole, whole, whole,
                  pl.BlockSpec((1, seq_len, LANES), lambda b, p: (b, 0, 0)),
                  pl.BlockSpec((1, 1, 2, BLOCK), lambda b, p: (b, p, 0, 0)),
                  pl.BlockSpec((1, 1, nt, 2, t), lambda b, p: (b, p, 0, 0, 0)),
                  pl.BlockSpec((1, 1, 2, LANES), lambda b, p: (b, p, 0, 0)),
                  pl.BlockSpec((1, 1, 2, 2, nt8, LANES), lambda b, p: (b, p, 0, 0, 0, 0))],
        out_specs=whole,
        out_shape=jax.ShapeDtypeStruct((batch, seq_len, FOX_WIDTH), BF16),
        scratch_shapes=[pltpu.VMEM((s_main, LANES), F32), pltpu.VMEM((s_main, LANES), F32),
                        pltpu.VMEM((nt8, LANES), F32),
                        pltpu.SMEM((4, nt), jnp.int32),
                        pltpu.VMEM((2 * t, 1), F32), pltpu.VMEM((2 * t, 1), F32),
                        pltpu.VMEM((2 * t, LANES), F32)],
        compiler_params=pltpu.CompilerParams(
            dimension_semantics=("parallel", "parallel"), vmem_limit_bytes=VMEM_LIMIT),
    )(r3(fq), r3(fk), r3(fv), cum_col, cr_prefix, cr_main, bounds, starts)
    return out.reshape(batch * seq_len, FOX_WIDTH)


def _post_kernel(of_ref, og_ref, gr_ref, gma_ref, gmb_ref, h_ref, ggla_ref, wfo_ref, wgo_ref,
                 wout_ref, gpost_ref, o_ref):
    tm = h_ref.shape[0]
    halves = (slice(0, tm // 2), slice(tm // 2, tm))
    ggla = ggla_ref[...]
    a = [_dot(of_ref[hs, :], wfo_ref[...]) for hs in halves]
    b = []
    for hs in halves:
        og = og_ref[hs, :]
        heads = [_rms(og[:, hh * GLA_DV:(hh + 1) * GLA_DV], ggla[:, hh * GLA_DV:(hh + 1) * GLA_DV])
                 for hh in range(GLA_HEADS)]
        gr = gr_ref[hs, :].astype(F32)
        o_gla = (jnp.concatenate(heads, axis=1) * (gr * jax.nn.sigmoid(gr))).astype(BF16)
        b.append(_dot(o_gla, wgo_ref[...]))
    mix = []
    for hs, a_h, b_h in zip(halves, a, b):
        y = jax.nn.sigmoid(gma_ref[hs, :].astype(F32)) * a_h + jax.nn.sigmoid(gmb_ref[hs, :].astype(F32)) * b_h
        mix.append(_dot(y.astype(BF16), wout_ref[...]))
    for hs, m_h in zip(halves, mix):
        o_ref[hs, :] = h_ref[hs, :] + _rms(m_h, gpost_ref[...])


def _post(o_fox, o_gla, gr, gma, gmb, h, ggla, wfo, wgo, wout, gpost, layer):
    n = h.shape[0]
    tm = ROW_TM
    row = lambda a: pl.BlockSpec((tm, a.shape[1]), lambda i: (i, 0))
    full = lambda a: (_resident(a.shape[1:], layer) if a.ndim == 3
                      else pl.BlockSpec(a.shape, lambda i: (0, 0)))
    args = (o_fox, o_gla, gr, gma, gmb, h, ggla, wfo, wgo, wout, gpost)
    return pl.pallas_call(
        _post_kernel,
        grid=(n // tm,),
        in_specs=[row(a) for a in args[:6]] + [full(a) for a in args[6:]],
        out_specs=pl.BlockSpec((tm, D_MODEL), lambda i: (i, 0)),
        out_shape=jax.ShapeDtypeStruct((n, D_MODEL), F32),
        compiler_params=pltpu.CompilerParams(
            dimension_semantics=("parallel",), vmem_limit_bytes=VMEM_LIMIT),
    )(*args)


def kernel(x, meta_tokens, w_in, w_alpha_up, b_alpha, b_f, g_gla_out, w_fox_o, w_gla_o, w_out,
           w_ffn1_gu, w_ffn1_down, w_ffn2_gu, w_ffn2_down,
           g_pre_ffn1, g_post_ffn1, g_pre_mix, g_post_mix, g_pre_ffn2, g_post_ffn2):
    batch, seq, d = x.shape
    seq_len = seq + BLOCK
    depth = w_in.shape[0]
    assert d == D_MODEL and (seq_len - BLOCK) % FOX_T == 0 and seq_len % ROW_TM == 0
    assert seq_len % FFN_TM == 0 and seq % FFN_TM_MAIN == 0

    pad = jnp.zeros((batch, N_PAD, d), x.dtype)
    meta = jnp.broadcast_to(meta_tokens.astype(x.dtype)[None], (batch, N_META, d))
    prefix = jnp.concatenate([pad, meta], axis=1)
    flat = lambda a: a.reshape(batch * seq_len, a.shape[-1])

    row2 = lambda a: a.reshape(1, -1).astype(F32)
    w_perm = w_in.astype(BF16)
    w1_gu, w1_down, w2_gu, w2_down = w_ffn1_gu, w_ffn1_down, w_ffn2_gu, w_ffn2_down
    wfo, wgo, wout = w_fox_o.astype(BF16), w_gla_o.astype(BF16), w_out.astype(BF16)
    for l in range(depth):
        wa_pad = jnp.zeros((LANES, GLA_KWIDTH), F32).at[
            FOX_HEADS:FOX_HEADS + GLA_GATE_RANK].set(w_alpha_up[l]).astype(BF16)
        bf_pad = jnp.zeros((1, LANES), F32).at[0, :FOX_HEADS].set(b_f[l])

        if l == 0:
            h = _ffn_first(x, prefix, row2(g_pre_ffn1[l]), w1_gu, w1_down, row2(g_post_ffn1[l]), l)
        else:
            h = _ffn(h, row2(g_pre_ffn1[l]), w1_gu, w1_down, row2(g_post_ffn1[l]), l)
        fq, fk, fv, gq, gk, gv, gr, gma, gmb, la, lf = _proj(
            flat(h), row2(g_pre_mix[l]), w_perm, wa_pad, row2(b_alpha[l]), bf_pad, seq_len, l)
        o_gla, cum_col, cum_row = _gla(gq, gk, gv, la, lf, batch, seq_len)
        o_fox = _fox(fq, fk, fv, cum_col, cum_row, batch, seq_len)
        h = _post(o_fox, flat(o_gla), gr, gma, gmb, flat(h),
                  row2(g_gla_out[l]), wfo, wgo, wout, row2(g_post_mix[l]), l).reshape(batch, seq_len, d)
        h = _ffn(h, row2(g_pre_ffn2[l]), w2_gu, w2_down, row2(g_post_ffn2[l]), l,
                 main_rows=seq if l == depth - 1 else None)
    return h
```

```python
import functools

import jax
import jax.numpy as jnp
import numpy as np
from jax import lax
from jax.experimental import pallas as pl
from jax.experimental.pallas import tpu as pltpu

D_MODEL = 1024
N_META = 16
BLOCK = 128
N_PAD = BLOCK - N_META
FOX_HEADS = 8
FOX_HEAD_DIM = 64
FOX_WIDTH = FOX_HEADS * FOX_HEAD_DIM
GLA_HEADS = 4
GLA_WIDTH = 512
GLA_DV = 128
GLA_DK = 64
GLA_KWIDTH = GLA_HEADS * GLA_DK
GLA_GATE_RANK = 16
GLA_GATE_TEMP = 16.0
D_FF = 2816
RMS_EPS = 1e-6

LANES = 128
SUB = 16
N_SUB = BLOCK // SUB
EXP_CAP = 80.0
NEG = -1e30

FFN_TM = 640
FFN_TM_MAIN = 512
FFN_TF = 256
FFN_W_ROWS = 256
ROW_TM = 640
FOX_T = 512
LOG2E = 1.4426950408889634
PRUNE_BITS = 160.0
NORM_UP = 1.01
FAST_MAX_BITS = 50.0
VMEM_LIMIT = 56 * 1024 * 1024

BF16 = jnp.bfloat16
F32 = jnp.float32

C_FQ, C_FK, C_FV = 0, 512, 1024
C_GQ, C_GK, C_GV = 1536, 1792, 2048
C_GR, C_GMA, C_GMB = 2560, 3072, 4096
C_SMALL = 5120
N_IN = 5144
N_IN_PERM = 5248
W_IN_MOVES = ((0, 0, 1536), (1536, 1544, 1024), (2560, 2584, 2560), (5120, 1536, 8), (5128, 2568, 16))


def _dot(a, b):
    return jnp.dot(a, b, preferred_element_type=F32)


def _dot_nt(a, b):
    return lax.dot_general(a, b, (((1,), (1,)), ((), ())), preferred_element_type=F32)


def _dot_tn(a, b):
    return lax.dot_general(a, b, (((0,), (0,)), ((), ())), preferred_element_type=F32)


def _rms(x, g):
    return x * lax.rsqrt(jnp.mean(x * x, axis=-1, keepdims=True) + RMS_EPS) * g


def _log_sigmoid(x):
    return jnp.minimum(x, 0.0) - jnp.log1p(jnp.exp(-jnp.abs(x)))


def _split2(x):
    hi = x.astype(BF16)
    lo = (x - hi.astype(F32)).astype(BF16)
    return hi, lo


def _dot_exact_lhs(m, parts):
    return _dot(m, parts[0]) + _dot(m, parts[1])


def _ffn_rows(h, gpre_ref, wgu_ref, wd_ref, gpost_ref, xn_ref, a_ref):
    m = h.shape[0]
    parts = (slice(0, m // 2), slice(m // 2, m)) if m >= 2 * FFN_TF else (slice(0, m),)
    out = []
    for rows in parts:
        h_p = h[rows]
        xn_ref[rows] = _rms(h_p, gpre_ref[...]).astype(BF16)
        for c in range(D_FF // FFN_TF):
            lo = c * FFN_TF
            g = _dot(xn_ref[rows], wgu_ref[:, lo:lo + FFN_TF])
            u = _dot(xn_ref[rows], wgu_ref[:, D_FF + lo:D_FF + lo + FFN_TF])
            a_ref[rows, lo:lo + FFN_TF] = (g * jax.nn.sigmoid(g) * u).astype(BF16)
        y = _dot(a_ref[rows], wd_ref[...])
        out.append(h_p + 0.5 * _rms(y, gpost_ref[...]))
    return out[0] if len(out) == 1 else jnp.concatenate(out, axis=0)


def _stream_weight(w_hbm, layer, n_rows, stage_ref, sem, sem_row, consume):
    rows = stage_ref.shape[1]
    n = n_rows // rows

    def chunk(c):
        return pltpu.make_async_copy(w_hbm.at[layer, pl.ds(c * rows, rows), :], stage_ref.at[c % 2],
                                     sem.at[sem_row, c % 2])

    chunk(0).start()
    for c in range(n):
        if c + 1 < n:
            chunk(c + 1).start()
        chunk(c).wait()
        consume(slice(c * rows, (c + 1) * rows), stage_ref.at[c % 2])


def _cast_into(dst_ref):
    def consume(rows, staged):
        dst_ref[rows, :] = staged[...].astype(BF16)
    return consume


def _ffn_weights(wgu_hbm, wd_hbm, layer, wgu_ref, wd_ref, sgu_ref, sd_ref, sem):
    @pl.when((pl.program_id(0) == 0) & (pl.program_id(1) == 0))
    def _():
        _stream_weight(wgu_hbm, layer, D_MODEL, sgu_ref, sem, 0, _cast_into(wgu_ref))
        _stream_weight(wd_hbm, layer, D_FF, sd_ref, sem, 1, _cast_into(wd_ref))


def _ffn_kernel(h_ref, gpre_ref, wgu_hbm, wd_hbm, gpost_ref, o_ref,
                xn_ref, a_ref, wgu_ref, wd_ref, sgu_ref, sd_ref, sem, *, layer):
    _ffn_weights(wgu_hbm, wd_hbm, layer, wgu_ref, wd_ref, sgu_ref, sd_ref, sem)
    o_ref[...] = _ffn_rows(h_ref[...], gpre_ref, wgu_ref, wd_ref, gpost_ref, xn_ref, a_ref)


def _ffn_first_kernel(x_ref, p_ref, gpre_ref, wgu_hbm, wd_hbm, gpost_ref, o_ref,
                      xn_ref, a_ref, wgu_ref, wd_ref, sgu_ref, sd_ref, sem, *, layer):
    _ffn_weights(wgu_hbm, wd_hbm, layer, wgu_ref, wd_ref, sgu_ref, sd_ref, sem)
    last = pl.program_id(1) == pl.num_programs(1) - 1

    @pl.when(jnp.logical_not(last))
    def _():
        o_ref[...] = _ffn_rows(x_ref[...], gpre_ref, wgu_ref, wd_ref, gpost_ref, xn_ref, a_ref)

    @pl.when(last)
    def _():
        o_ref[0:BLOCK] = _ffn_rows(p_ref[...], gpre_ref, wgu_ref, wd_ref, gpost_ref, xn_ref, a_ref)


def _resident(shape, layer):
    return pl.BlockSpec((None,) + shape, lambda *_: (layer,) + (0,) * len(shape),
                        pipeline_mode=pl.Buffered(1))


def _ffn_scratch(tm):
    return [pltpu.VMEM((tm, D_MODEL), BF16), pltpu.VMEM((tm, D_FF), BF16),
            pltpu.VMEM((D_MODEL, 2 * D_FF), BF16), pltpu.VMEM((D_FF, D_MODEL), BF16),
            pltpu.VMEM((2, FFN_W_ROWS, 2 * D_FF), F32), pltpu.VMEM((2, FFN_W_ROWS, D_MODEL), F32),
            pltpu.SemaphoreType.DMA((2, 2))]


def _ffn(h, gpre, w_gu, w_down, gpost, layer, main_rows=None):
    batch, seq_len, _ = h.shape
    if main_rows is None:
        tm, rows = FFN_TM, seq_len
    else:
        tm, rows = FFN_TM_MAIN, main_rows
    tile = pl.BlockSpec((None, tm, D_MODEL), lambda b, i: (b, i, 0))
    vec = pl.BlockSpec((1, D_MODEL), lambda b, i: (0, 0))
    hbm = pl.BlockSpec(memory_space=pl.ANY)
    return pl.pallas_call(
        functools.partial(_ffn_kernel, layer=layer),
        grid=(batch, rows // tm),
        in_specs=[tile, vec, hbm, hbm, vec],
        out_specs=tile,
        out_shape=jax.ShapeDtypeStruct((batch, rows, D_MODEL), F32),
        scratch_shapes=_ffn_scratch(tm),
        compiler_params=pltpu.CompilerParams(
            dimension_semantics=("arbitrary", "arbitrary"), vmem_limit_bytes=VMEM_LIMIT),
    )(h, gpre, w_gu, w_down, gpost)


def _ffn_first(x, prefix, gpre, w_gu, w_down, gpost, layer):
    batch, seq, _ = x.shape
    tm = FFN_TM_MAIN
    n_main = seq // tm
    vec = pl.BlockSpec((1, D_MODEL), lambda b, i: (0, 0))
    hbm = pl.BlockSpec(memory_space=pl.ANY)
    return pl.pallas_call(
        functools.partial(_ffn_first_kernel, layer=layer),
        grid=(batch, n_main + 1),
        in_specs=[pl.BlockSpec((None, tm, D_MODEL), lambda b, i: (b, jnp.minimum(i, n_main - 1), 0)),
                  pl.BlockSpec((None, BLOCK, D_MODEL), lambda b, i: (b, 0, 0)),
                  vec, hbm, hbm, vec],
        out_specs=pl.BlockSpec((None, tm, D_MODEL), lambda b, i: (b, i, 0)),
        out_shape=jax.ShapeDtypeStruct((batch, seq + BLOCK, D_MODEL), F32),
        scratch_shapes=_ffn_scratch(tm),
        compiler_params=pltpu.CompilerParams(
            dimension_semantics=("arbitrary", "arbitrary"), vmem_limit_bytes=VMEM_LIMIT),
    )(x, prefix, gpre, w_gu, w_down, gpost)


def _proj_kernel(h_ref, g_ref, w_ref, wa_ref, ba_ref, bf_ref,
                 fq_ref, fk_ref, fv_ref, gq_ref, gk_ref, gv_ref, gr_ref, gma_ref, gmb_ref,
                 la_ref, lf_ref, ws_ref, *, tiles_per_seq, main_rows):
    @pl.when(pl.program_id(0) == 0)
    def _():
        for dst, src, width in W_IN_MOVES:
            ws_ref[:, dst:dst + width] = w_ref[:, src:src + width]
        ws_ref[:, N_IN:] = jnp.zeros((D_MODEL, N_IN_PERM - N_IN), BF16)

    w_ref = ws_ref
    tm = h_ref.shape[0]
    m = tm // 2
    halves = (slice(0, m), slice(m, tm))
    xns = {}
    for rows in halves:
        if rows.start not in xns:
            xns[rows.start] = _rms(h_ref[rows, :], g_ref[...]).astype(BF16)
        xn = xns[rows.start]
        pos = ((pl.program_id(0) % tiles_per_seq) * tm + rows.start
               + lax.broadcasted_iota(jnp.int32, (m, 1), 0))
        valid = jnp.logical_or(pos < main_rows, pos >= main_rows + N_PAD).astype(F32)

        def proj(c0, width):
            return _dot(xn, w_ref[:, c0:c0 + width])

        small = proj(C_SMALL, LANES)
        if rows.start == 0:
            xns[m] = _rms(h_ref[halves[1], :], g_ref[...]).astype(BF16)
        fq_ref[rows, :] = (proj(C_FQ, FOX_WIDTH) * (FOX_HEAD_DIM ** -0.5 * LOG2E)).astype(BF16)
        fk_ref[rows, :] = proj(C_FK, FOX_WIDTH).astype(BF16)
        lane = lax.broadcasted_iota(jnp.int32, (m, LANES), 1)
        lf = _log_sigmoid(small + bf_ref[...]) * (valid * LOG2E)
        lf_ref[rows, :] = jnp.where(lane < FOX_HEADS, lf, 0.0)
        xa = _dot(small.astype(BF16), wa_ref[...]) + ba_ref[...]
        la_ref[rows, :] = _log_sigmoid(xa) * (valid * (1.0 / GLA_GATE_TEMP))
        fv_ref[rows, :] = proj(C_FV, FOX_WIDTH).astype(BF16)
        gq_ref[rows, :] = proj(C_GQ, GLA_KWIDTH) * (GLA_DK ** -0.5)
        gk_ref[rows, :] = proj(C_GK, GLA_KWIDTH) * valid
        gv_ref[rows, :] = proj(C_GV, GLA_WIDTH).astype(BF16)
        gr_ref[rows, :] = proj(C_GR, GLA_WIDTH).astype(BF16)
        gma_ref[rows, :] = proj(C_GMA, D_MODEL).astype(BF16)
        gmb_ref[rows, :] = proj(C_GMB, D_MODEL).astype(BF16)


def _proj(h, g, w_perm, wa_pad, ba, bf_pad, seq_len, layer):
    n = h.shape[0]
    tm = ROW_TM
    row = lambda width: pl.BlockSpec((tm, width), lambda i: (i, 0))
    full = lambda a: (_resident(a.shape[1:], layer) if a.ndim == 3
                      else pl.BlockSpec(a.shape, lambda i: (0, 0)))
    outs = [
        (FOX_WIDTH, BF16), (FOX_WIDTH, BF16), (FOX_WIDTH, BF16),
        (GLA_KWIDTH, F32), (GLA_KWIDTH, F32), (GLA_WIDTH, BF16),
        (GLA_WIDTH, BF16), (D_MODEL, BF16), (D_MODEL, BF16),
        (GLA_KWIDTH, F32), (LANES, F32),
    ]
    return pl.pallas_call(
        functools.partial(_proj_kernel, tiles_per_seq=seq_len // tm, main_rows=seq_len - BLOCK),
        grid=(n // tm,),
        in_specs=[row(D_MODEL), full(g), full(w_perm), full(wa_pad), full(ba), full(bf_pad)],
        out_specs=[row(w) for w, _ in outs],
        out_shape=[jax.ShapeDtypeStruct((n, w), dt) for w, dt in outs],
        scratch_shapes=[pltpu.VMEM((D_MODEL, N_IN_PERM), BF16)],
        compiler_params=pltpu.CompilerParams(
            dimension_semantics=("arbitrary",), vmem_limit_bytes=VMEM_LIMIT),
    )(h, g, w_perm, wa_pad, ba, bf_pad)


def _gla_kernel(q_ref, k_ref, v_ref, la_ref, lf_ref, o_ref, cc_ref, cr_ref, st_ref, carry_ref):
    @pl.when(pl.program_id(0) == 0)
    def _():
        st_ref[...] = jnp.zeros_like(st_ref)
        carry_ref[...] = jnp.zeros_like(carry_ref)

    row = lax.broadcasted_iota(jnp.int32, (BLOCK, BLOCK), 0)
    col = lax.broadcasted_iota(jnp.int32, (BLOCK, BLOCK), 1)
    causal = row >= col
    tril = causal.astype(BF16)
    tril_sub = (causal & ((row // SUB) == (col // SUB))).astype(BF16)
    row_w = lax.broadcasted_iota(jnp.int32, (BLOCK, GLA_KWIDTH), 0)
    row_blk = row // SUB

    batches = range(q_ref.shape[0])
    problems = [(b, h) for b in batches for h in range(GLA_HEADS)]

    cb_all, cw_all = [], []
    for b in batches:
        cl = _dot_exact_lhs(tril, _split2(lf_ref[b])) + carry_ref[b]
        carry_ref[b] = cl[BLOCK - 1:BLOCK, :]
        cc_ref[b] = cl
        cr_ref[b] = cl.T[0:FOX_HEADS, :]
        la_parts = _split2(la_ref[b])
        cb_all.append(_dot_exact_lhs(tril, la_parts))
        cw_all.append(_dot_exact_lhs(tril_sub, la_parts))

    ops = {}
    for b in batches:
        cb, cw = cb_all[b], cw_all[b]
        last = cb[BLOCK - 1:BLOCK, :]
        q = q_ref[b]
        k = k_ref[b]
        q_glob = (q * jnp.exp(cb)).astype(BF16)
        q_loc = (q * jnp.exp(cw)).astype(BF16)
        k_end = (k * jnp.exp(last - cb)).astype(BF16)
        decay = jnp.exp(last)
        k_sub = []
        for i in range(N_SUB):
            ref = cb[i * SUB - 1:i * SUB, :] if i > 0 else jnp.zeros_like(last)
            e = jnp.exp(jnp.minimum(ref - cb, EXP_CAP))
            k_sub.append(jnp.where(row_w < (i + 1) * SUB, k * e, 0.0).astype(BF16))
        for pair in range(GLA_HEADS // 2):
            sl = slice(pair * LANES, (pair + 1) * LANES)
            k_cat = jnp.concatenate([ks[:, sl] for ks in k_sub], axis=1)
            for half in range(2):
                h = 2 * pair + half
                head_lanes = (col < GLA_DK) if half == 0 else (col >= GLA_DK)
                q_h = jnp.where(head_lanes, q_loc[:, sl], 0.0)
                q_parts = []
                for p in range(N_SUB // 2):
                    r = slice(2 * p * SUB, (2 * p + 2) * SUB)
                    q_parts.append(jnp.concatenate(
                        [jnp.where(row_blk[r] == 2 * p, q_h[r], 0.0), jnp.where(row_blk[r] == 2 * p + 1, q_h[r], 0.0)],
                        axis=1))
                ops[b, h] = (q_parts, k_cat, jnp.where(head_lanes, q_glob[:, sl], 0.0), k_end[:, sl],
                             decay[:, sl], v_ref[b, :, h * GLA_DV:(h + 1) * GLA_DV])

    att, inter, upd = {}, {}, {}
    for p in problems:
        q_parts, k_cat, q_g, k_e, _, v_h = ops[p]
        att[p] = jnp.concatenate(
            [_dot_nt(q_p, k_cat[:, 2 * i * LANES:(2 * i + 2) * LANES]) for i, q_p in enumerate(q_parts)],
            axis=0)
        inter[p] = _dot_nt(q_g, st_ref[p].astype(BF16))
        upd[p] = _dot_tn(v_h, k_e)

    for p in problems:
        b, h = p
        a = jnp.where(causal, att[p], 0.0).astype(BF16)
        o_ref[b, :, h * GLA_DV:(h + 1) * GLA_DV] = inter[p] + _dot(a, ops[p][5])
        st_ref[p] = st_ref[p] * ops[p][4] + upd[p]


def _gla(gq, gk, gv, la, lf, batch, seq_len):
    nc = seq_len // BLOCK
    chunk = lambda c: (c + nc - 1) % nc
    blk = lambda width: pl.BlockSpec((batch, BLOCK, width), lambda c: (0, chunk(c), 0))
    r3 = lambda a: a.reshape(batch, seq_len, a.shape[-1])
    return pl.pallas_call(
        _gla_kernel,
        grid=(nc,),
        in_specs=[blk(GLA_KWIDTH), blk(GLA_KWIDTH), blk(GLA_WIDTH), blk(GLA_KWIDTH), blk(LANES)],
        out_specs=[blk(GLA_WIDTH), blk(LANES),
                   pl.BlockSpec((batch, FOX_HEADS, BLOCK), lambda c: (0, 0, chunk(c)))],
        out_shape=[jax.ShapeDtypeStruct((batch, seq_len, GLA_WIDTH), F32),
                   jax.ShapeDtypeStruct((batch, seq_len, LANES), F32),
                   jax.ShapeDtypeStruct((batch, FOX_HEADS, seq_len), F32)],
        scratch_shapes=[pltpu.VMEM((batch, GLA_HEADS, GLA_DV, LANES), F32),
                        pltpu.VMEM((batch, 1, LANES), F32)],
        compiler_params=pltpu.CompilerParams(
            dimension_semantics=("arbitrary",), vmem_limit_bytes=VMEM_LIMIT),
    )(r3(gq), r3(gk), r3(gv), r3(la), r3(lf))


def _fox_kernel(q_ref, k_ref, v_ref, cc_ref, crp_ref, crm_ref, cb_ref, cs_ref, o_ref,
                ra_ref, rb_ref, ub_ref, plan_ref, m_ref, l_ref, acc_ref):
    t = FOX_T
    nt = crm_ref.shape[2]
    s_main = nt * t
    pre = slice(s_main, s_main + BLOCK)
    pair = pl.program_id(1)

    def stack_heads(x, lane_first):
        return jnp.concatenate([jnp.where(lane_first, x, 0.0), jnp.where(lane_first, 0.0, x)], axis=0)

    def head_cols(cc, lane):
        return jnp.concatenate(
            [jnp.sum(jnp.where(lane == 2 * pair + hh, cc, 0.0), axis=1, keepdims=True)
             for hh in range(2)], axis=0)

    lane_b = lax.broadcasted_iota(jnp.int32, (BLOCK, LANES), 1)
    first_b = lane_b < FOX_HEAD_DIM
    kp = k_ref[0, pre, :]
    vp = v_ref[0, pre, :]
    ckp = crp_ref[0, 0]
    row_b = lax.broadcasted_iota(jnp.int32, (BLOCK, BLOCK), 0)
    ok_b = (lane_b <= row_b) & ((lane_b >= N_PAD) | (lane_b == row_b))
    ok_b = jnp.concatenate([ok_b, ok_b], axis=0)
    s = _dot_nt(stack_heads(q_ref[0, pre, :], first_b), kp)
    z = jnp.concatenate([s[:BLOCK] - ckp[0:1], s[BLOCK:] - ckp[1:2]], axis=0)
    z = jnp.where(ok_b, z, NEG)
    cq2 = head_cols(cc_ref[0, pre, :], lane_b)
    m = jnp.max(z, axis=1, keepdims=True) + cq2
    p = jnp.exp2(z - (m - cq2))
    out = _dot(p.astype(BF16), vp) / jnp.sum(p, axis=1, keepdims=True)
    o_ref[0, pre, :] = jnp.where(first_b, out[:BLOCK], out[BLOCK:]).astype(o_ref.dtype)

    def v_with_ones(rows, head_b):
        v = v_ref[0, rows, :]
        lane_v = lax.broadcasted_iota(jnp.int32, v.shape, 1)
        if head_b:
            return jnp.where(lane_v >= FOX_HEAD_DIM, v, jnp.where(lane_v == 0, 1.0, 0.0).astype(BF16))
        return jnp.where(lane_v < FOX_HEAD_DIM, v, jnp.where(lane_v == FOX_HEAD_DIM, 1.0, 0.0).astype(BF16))

    lane = lax.broadcasted_iota(jnp.int32, (t, LANES), 1)
    first = lane < FOX_HEAD_DIM
    row_l = lax.broadcasted_iota(jnp.int32, (LANES, LANES), 0)
    row_w = lax.broadcasted_iota(jnp.int32, (LANES, 2 * LANES), 0)
    second = lax.broadcasted_iota(jnp.int32, (LANES, 2 * LANES), 1) >= LANES
    ind_ab = ((row_w >= FOX_HEAD_DIM) == second).astype(BF16)
    hot_ab = (row_w == 2 * pair + second.astype(jnp.int32)).astype(BF16)
    hot_la = (row_l == FOX_HEAD_DIM).astype(BF16)
    hot_lb = (row_l == 0).astype(BF16)
    reps = t // LANES
    th = t // 2

    sel = lax.broadcasted_iota(jnp.int32, (8, LANES), 0) == lax.broadcasted_iota(
        jnp.int32, (8, LANES), 1) // FOX_HEAD_DIM
    k_all = k_ref[0]
    k_max = jnp.sqrt(jnp.max(_dot_nt(sel.astype(BF16), k_all * k_all), axis=1, keepdims=True))

    per_step = 2 if nt % 2 == 0 else 1

    def prepare(step, carry):
        tiles_here = [step * per_step + u for u in range(per_step)]
        rows = [pl.ds(pl.multiple_of(i * t, t), t) for i in tiles_here]
        nqs = [_dot(q_ref[0, r, :] * q_ref[0, r, :], ind_ab) for r in rows]
        cqs = [_dot(cc_ref[0, r, :].astype(BF16), hot_ab) for r in rows]
        for i, r, nq, cq in zip(tiles_here, rows, nqs, cqs):
            ub_a = jnp.sqrt(nq[:, :LANES]) * (k_max[0:1] * NORM_UP)
            ub_b = jnp.sqrt(nq[:, LANES:]) * (k_max[1:2] * NORM_UP)
            ra_ref[r, :] = ub_a - cq[:, :LANES]
            rb_ref[r, :] = ub_b - cq[:, LANES:]
            ub_ref[pl.ds(i, 1), :] = jnp.max(jnp.maximum(ub_a, ub_b), axis=0, keepdims=True)
        return carry

    lax.fori_loop(0, nt // per_step, prepare, 0)

    ub_t = ub_ref[0:nt, :]
    fast_v = ub_t <= FAST_MAX_BITS
    bound = jnp.where(fast_v, PRUNE_BITS, PRUNE_BITS + 2.0 * ub_t)
    cb = cb_ref[0, 0]
    dead = ((cs_ref[0, 0, 0, 0, 0:nt, :] - cb[0:1]) < -bound) & ((cs_ref[0, 0, 1, 0, 0:nt, :] - cb[1:2]) < -bound)
    tile_q = lax.broadcasted_iota(jnp.int32, (nt, LANES), 0)
    tile_k = lax.broadcasted_iota(jnp.int32, (nt, LANES), 1)
    n_skip_v = jnp.sum(jnp.where(dead & (tile_k < tile_q), 1, 0), axis=1, keepdims=True)
    skip_pre_v = jnp.sum(jnp.where(dead & (tile_k == nt), 1, 0), axis=1, keepdims=True)
    dead_lo = ((cs_ref[0, 0, 0, 1, 0:nt, :] - cb[0:1]) < -bound) & ((cs_ref[0, 0, 1, 1, 0:nt, :] - cb[1:2]) < -bound)
    old_dead_v = jnp.sum(jnp.where(dead_lo & (tile_k == tile_q + (nt - 1)), 1, 0), axis=1, keepdims=True)
    fast_i = jnp.where(fast_v[:, 0:1], 1, 0)
    tile_1 = tile_q[:, 0:1]
    for i in range(nt):
        for row, vec in enumerate((fast_i, n_skip_v, skip_pre_v, old_dead_v)):
            plan_ref[row, i] = jnp.sum(jnp.where(tile_1 == i, vec, 0))

    def main_tile(i, carry):
        r0 = pl.multiple_of(i * t, t)
        q2 = stack_heads(q_ref[0, pl.ds(r0, t), :], first)
        fast = plan_ref[0, i] == 1
        n_skip = plan_ref[1, i]
        skip_prefix = plan_ref[2, i]
        old_half_dead = plan_ref[3, i] == 1

        def logits(kt, ck, offs):
            s = _dot_nt(q2, kt)
            return jnp.concatenate([s[:t] - ck[0:1] - offs[:t], s[t:] - ck[1:2] - offs[t:]], axis=0)

        def run(update, offs_p, offs_m, fuse_tail):
            acc_ref[...] = jnp.zeros_like(acc_ref)

            @pl.when(skip_prefix == 0)
            def _():
                z = logits(kp, ckp, offs_p)
                lane_k = lax.broadcasted_iota(jnp.int32, (2 * t, BLOCK), 1)
                update([jnp.where(lane_k >= N_PAD, z, NEG)], [pre])

            def tiles(js, diagonal, old_half_dead=False):
                split = diagonal and fuse_tail
                full = js[:-1] if split else js
                upper = None
                if old_half_dead:
                    full = full[1:]
                    base = pl.multiple_of(js[0] * t, t)
                    ck = crm_ref[0, 0, js[0]]
                    r_old = pl.ds(base, th)
                    q_up = jnp.concatenate([q2[0:th], q2[t:t + th]], axis=0)
                    o_up = jnp.concatenate([offs_m[0:th, :th], offs_m[t:t + th, :th]], axis=0)
                    s = _dot_nt(q_up, k_ref[0, r_old, :])
                    upper = (jnp.concatenate([s[:th] - ck[0:1, :th] - o_up[:th], s[th:] - ck[1:2, :th] - o_up[th:]],
                                             axis=0), r_old)
                    r_new = pl.ds(base + th, th)
                    z_new = logits(k_ref[0, r_new, :], ck[:, th:], offs_m[:, :th])
                rows = [pl.ds(pl.multiple_of(j * t, t), t) for j in full]
                zs = [logits(k_ref[0, r, :], crm_ref[0, 0, j], offs_m) for j, r in zip(full, rows)]
                if old_half_dead:
                    zs, rows = [z_new] + zs, [r_new] + rows
                if not diagonal:
                    update(zs, rows)
                    return
                if not split:
                    ok = lax.broadcasted_iota(jnp.int32, (t, t), 1) <= lax.broadcasted_iota(jnp.int32, (t, t), 0)
                    zs[-1] = jnp.where(jnp.concatenate([ok, ok], axis=0), zs[-1], NEG)
                    update(zs, rows)
                    return
                base = pl.multiple_of(js[-1] * t, t)
                ck = crm_ref[0, 0, js[-1]]
                tri = lax.broadcasted_iota(jnp.int32, (th, th), 1) <= lax.broadcasted_iota(jnp.int32, (th, th), 0)
                r_left = pl.ds(base, th)
                z_left = logits(k_ref[0, r_left, :], ck[:, :th], offs_m[:, :th])
                lower_rows = (lax.broadcasted_iota(jnp.int32, (2 * t, th), 0) // th) % 2 == 1
                z_left = jnp.where(lower_rows | jnp.concatenate([tri] * 4, axis=0), z_left, NEG)
                r_right = pl.ds(base + th, th)
                q_lo = jnp.concatenate([q2[th:t], q2[t + th:]], axis=0)
                o_lo = jnp.concatenate([offs_m[th:t, :th], offs_m[t + th:, :th]], axis=0)
                s = _dot_nt(q_lo, k_ref[0, r_right, :])
                z_right = jnp.concatenate([s[:th] - ck[0:1, th:] - o_lo[:th], s[th:] - ck[1:2, th:] - o_lo[th:]],
                                          axis=0)
                z_right = jnp.where(jnp.concatenate([tri, tri], axis=0), z_right, NEG)
                update(zs + [z_left], rows + [r_left], (z_right, r_right), upper)

            def body(j, c):
                tiles([j], False)
                return c

            if not fuse_tail:
                lax.fori_loop(n_skip, i, body, 0)
                tiles([i], True)
                return
            n_live = i - n_skip
            lax.fori_loop(n_skip, jnp.maximum(i - 2, n_skip), body, 0)

            @pl.when((n_live >= 2) & old_half_dead)
            def _():
                tiles([i - 2, i - 1, i], True, True)

            @pl.when((n_live >= 2) & jnp.logical_not(old_half_dead))
            def _():
                tiles([i - 2, i - 1, i], True)

            @pl.when(n_live == 1)
            def _():
                tiles([i - 1, i], True)

            @pl.when(n_live == 0)
            def _():
                tiles([i], True)

        def finish(l_a, l_b):
            acc = acc_ref[...]
            out = jnp.where(first, acc[:t] / l_a, acc[t:] / l_b)
            o_ref[0, pl.ds(r0, t), :] = out.astype(o_ref.dtype)

        @pl.when(fast)
        def _():
            ref = jnp.concatenate([ra_ref[pl.ds(r0, t), :], rb_ref[pl.ds(r0, t), :]], axis=0)

            def update(zs, rows, lower=None, upper=None):
                ps = [jnp.exp2(z).astype(BF16) for z in zs]
                p_lo = None if lower is None else jnp.exp2(lower[0]).astype(BF16)
                if upper is not None:
                    p_up = jnp.exp2(upper[0]).astype(BF16)
                    acc_ref[0:th] += _dot(p_up[:th], v_with_ones(upper[1], False))
                    acc_ref[t:t + th] += _dot(p_up[th:], v_with_ones(upper[1], True))
                for half, head_b in ((slice(0, t), False), (slice(t, 2 * t), True)):
                    pv = _dot(ps[0][half], v_with_ones(rows[0], head_b))
                    for p, r in zip(ps[1:], rows[1:]):
                        pv = pv + _dot(p[half], v_with_ones(r, head_b))
                    acc_ref[half] += pv
                if lower is not None:
                    acc_ref[th:t] += _dot(p_lo[:th], v_with_ones(lower[1], False))
                    acc_ref[t + th:] += _dot(p_lo[th:], v_with_ones(lower[1], True))

            run(update, ref, jnp.concatenate([ref] * reps, axis=1), True)

            def denominators(acc, hot):
                hi = acc.astype(BF16)
                lo = (acc - hi.astype(F32)).astype(BF16)
                return _dot(hi, hot) + _dot(lo, hot)

            finish(denominators(acc_ref[0:t], hot_la), denominators(acc_ref[t:], hot_lb))

        @pl.when(jnp.logical_not(fast))
        def _():
            cq2 = head_cols(cc_ref[0, pl.ds(r0, t), :], lane)
            m_ref[...] = jnp.full_like(m_ref, NEG)
            l_ref[...] = jnp.zeros_like(l_ref)

            def update(zs, rows):
                (z,), (r,) = zs, rows
                m_old = m_ref[...]
                m_new = jnp.maximum(m_old, jnp.max(z, axis=1, keepdims=True) + cq2)
                p = jnp.exp2(z - (m_new - cq2))
                alpha = jnp.exp2(m_old - m_new)
                l_ref[...] = alpha * l_ref[...] + jnp.sum(p, axis=1, keepdims=True)
                acc_ref[...] = alpha * acc_ref[...] + _dot(p.astype(BF16), v_ref[0, r, :])
                m_ref[...] = m_new

            run(update, jnp.zeros((2 * t, 1), F32), jnp.zeros((2 * t, 1), F32), False)
            finish(l_ref[0:t], l_ref[t:])

        return carry

    lax.fori_loop(0, nt, main_tile, 0)


def _fox(fq, fk, fv, cum_col, cum_row, batch, seq_len):
    t = FOX_T
    nt = (seq_len - BLOCK) // t
    npairs = FOX_HEADS // 2
    assert 2 * nt + 1 <= LANES
    s_main = nt * t
    r3 = lambda a: a.reshape(batch, seq_len, a.shape[-1])
    cr = cum_row.reshape(batch, npairs, 2, seq_len)
    cr_prefix = cr[..., s_main:]
    cr_main = cr[..., :s_main].reshape(batch, npairs, 2, nt, t).transpose(0, 1, 3, 2, 4)
    th = t // 2
    bounds = jnp.concatenate(
        [cr[..., t - 1:s_main:t], cr[..., seq_len - 1:], cr[..., th - 1:s_main:t],
         jnp.zeros((batch, npairs, 2, LANES - 2 * nt - 1), F32)], axis=-1)
    nt8 = -(-nt // 8) * 8
    starts = jnp.stack([cr[..., 0:s_main:t], cr[..., th:s_main:t]], axis=3)
    starts = jnp.pad(starts, ((0, 0),) * 4 + ((0, nt8 - nt),))
    starts = jnp.broadcast_to(starts[..., None], (batch, npairs, 2, 2, nt8, LANES))
    whole = pl.BlockSpec((1, seq_len, LANES), lambda b, p: (b, 0, p))
    out = pl.pallas_call(
        _fox_kernel,
        grid=(batch, npairs),
        in_specs=[whole, whole, whole,
                  pl.BlockSpec((1, seq_len, LANES), lambda b, p: (b, 0, 0)),
                  pl.BlockSpec((1, 1, 2, BLOCK), lambda b, p: (b, p, 0, 0)),
                  pl.BlockSpec((1, 1, nt, 2, t), lambda b, p: (b, p, 0, 0, 0)),
                  pl.BlockSpec((1, 1, 2, LANES), lambda b, p: (b, p, 0, 0)),
                  pl.BlockSpec((1, 1, 2, 2, nt8, LANES), lambda b, p: (b, p, 0, 0, 0, 0))],
        out_specs=whole,
        out_shape=jax.ShapeDtypeStruct((batch, seq_len, FOX_WIDTH), BF16),
        scratch_shapes=[pltpu.VMEM((s_main, LANES), F32), pltpu.VMEM((s_main, LANES), F32),
                        pltpu.VMEM((nt8, LANES), F32),
                        pltpu.SMEM((4, nt), jnp.int32),
                        pltpu.VMEM((2 * t, 1), F32), pltpu.VMEM((2 * t, 1), F32),
                        pltpu.VMEM((2 * t, LANES), F32)],
        compiler_params=pltpu.CompilerParams(
            dimension_semantics=("parallel", "parallel"), vmem_limit_bytes=VMEM_LIMIT),
    )(r3(fq), r3(fk), r3(fv), cum_col, cr_prefix, cr_main, bounds, starts)
    return out.reshape(batch * seq_len, FOX_WIDTH)


def _post_kernel(of_ref, og_ref, gr_ref, gma_ref, gmb_ref, h_ref, ggla_ref, wfo_ref, wgo_ref,
                 wout_ref, gpost_ref, o_ref):
    tm = h_ref.shape[0]
    halves = (slice(0, tm // 2), slice(tm // 2, tm))
    ggla = ggla_ref[...]
    a = [_dot(of_ref[hs, :], wfo_ref[...]) for hs in halves]
    b = []
    for hs in halves:
        og = og_ref[hs, :]
        heads = [_rms(og[:, hh * GLA_DV:(hh + 1) * GLA_DV], ggla[:, hh * GLA_DV:(hh + 1) * GLA_DV])
                 for hh in range(GLA_HEADS)]
        gr = gr_ref[hs, :].astype(F32)
        o_gla = (jnp.concatenate(heads, axis=1) * (gr * jax.nn.sigmoid(gr))).astype(BF16)
        b.append(_dot(o_gla, wgo_ref[...]))
    mix = []
    for hs, a_h, b_h in zip(halves, a, b):
        y = jax.nn.sigmoid(gma_ref[hs, :].astype(F32)) * a_h + jax.nn.sigmoid(gmb_ref[hs, :].astype(F32)) * b_h
        mix.append(_dot(y.astype(BF16), wout_ref[...]))
    for hs, m_h in zip(halves, mix):
        o_ref[hs, :] = h_ref[hs, :] + _rms(m_h, gpost_ref[...])


def _post(o_fox, o_gla, gr, gma, gmb, h, ggla, wfo, wgo, wout, gpost, layer):
    n = h.shape[0]
    tm = ROW_TM
    row = lambda a: pl.BlockSpec((tm, a.shape[1]), lambda i: (i, 0))
    full = lambda a: (_resident(a.shape[1:], layer) if a.ndim == 3
                      else pl.BlockSpec(a.shape, lambda i: (0, 0)))
    args = (o_fox, o_gla, gr, gma, gmb, h, ggla, wfo, wgo, wout, gpost)
    return pl.pallas_call(
        _post_kernel,
        grid=(n // tm,),
        in_specs=[row(a) for a in args[:6]] + [full(a) for a in args[6:]],
        out_specs=pl.BlockSpec((tm, D_MODEL), lambda i: (i, 0)),
        out_shape=jax.ShapeDtypeStruct((n, D_MODEL), F32),
        compiler_params=pltpu.CompilerParams(
            dimension_semantics=("parallel",), vmem_limit_bytes=VMEM_LIMIT),
    )(*args)


def kernel(x, meta_tokens, w_in, w_alpha_up, b_alpha, b_f, g_gla_out, w_fox_o, w_gla_o, w_out,
           w_ffn1_gu, w_ffn1_down, w_ffn2_gu, w_ffn2_down,
           g_pre_ffn1, g_post_ffn1, g_pre_mix, g_post_mix, g_pre_ffn2, g_post_ffn2):
    batch, seq, d = x.shape
    seq_len = seq + BLOCK
    depth = w_in.shape[0]
    assert d == D_MODEL and (seq_len - BLOCK) % FOX_T == 0 and seq_len % ROW_TM == 0
    assert seq_len % FFN_TM == 0 and seq % FFN_TM_MAIN == 0

    pad = jnp.zeros((batch, N_PAD, d), x.dtype)
    meta = jnp.broadcast_to(meta_tokens.astype(x.dtype)[None], (batch, N_META, d))
    prefix = jnp.concatenate([pad, meta], axis=1)
    flat = lambda a: a.reshape(batch * seq_len, a.shape[-1])

    row2 = lambda a: a.reshape(1, -1).astype(F32)
    w_perm = w_in.astype(BF16)
    w1_gu, w1_down, w2_gu, w2_down = w_ffn1_gu, w_ffn1_down, w_ffn2_gu, w_ffn2_down
    wfo, wgo, wout = w_fox_o.astype(BF16), w_gla_o.astype(BF16), w_out.astype(BF16)
    for l in range(depth):
        wa_pad = jnp.zeros((LANES, GLA_KWIDTH), F32).at[
            FOX_HEADS:FOX_HEADS + GLA_GATE_RANK].set(w_alpha_up[l]).astype(BF16)
        bf_pad = jnp.zeros((1, LANES), F32).at[0, :FOX_HEADS].set(b_f[l])

        if l == 0:
            h = _ffn_first(x, prefix, row2(g_pre_ffn1[l]), w1_gu, w1_down, row2(g_post_ffn1[l]), l)
        else:
            h = _ffn(h, row2(g_pre_ffn1[l]), w1_gu, w1_down, row2(g_post_ffn1[l]), l)
        fq, fk, fv, gq, gk, gv, gr, gma, gmb, la, lf = _proj(
            flat(h), row2(g_pre_mix[l]), w_perm, wa_pad, row2(b_alpha[l]), bf_pad, seq_len, l)
        o_gla, cum_col, cum_row = _gla(gq, gk, gv, la, lf, batch, seq_len)
        o_fox = _fox(fq, fk, fv, cum_col, cum_row, batch, seq_len)
        h = _post(o_fox, flat(o_gla), gr, gma, gmb, flat(h),
                  row2(g_gla_out[l]), wfo, wgo, wout, row2(g_post_mix[l]), l).reshape(batch, seq_len, d)
        h = _ffn(h, row2(g_pre_ffn2[l]), w2_gu, w2_down, row2(g_post_ffn2[l]), l,
                 main_rows=seq if l == depth - 1 else None)
    return h
```

```python
import functools

import jax
import jax.numpy as jnp
import numpy as np
from jax import lax
from jax.experimental import pallas as pl
from jax.experimental.pallas import tpu as pltpu

D_MODEL = 1024
N_META = 16
BLOCK = 128
N_PAD = BLOCK - N_META
FOX_HEADS = 8
FOX_HEAD_DIM = 64
FOX_WIDTH = FOX_HEADS * FOX_HEAD_DIM
GLA_HEADS = 4
GLA_WIDTH = 512
GLA_DV = 128
GLA_DK = 64
GLA_KWIDTH = GLA_HEADS * GLA_DK
GLA_GATE_RANK = 16
GLA_GATE_TEMP = 16.0
D_FF = 2816
RMS_EPS = 1e-6

LANES = 128
SUB = 16
N_SUB = BLOCK // SUB
EXP_CAP = 80.0
NEG = -1e30

FFN_TM = 640
FFN_TM_MAIN = 512
FFN_TF = 256
FFN_W_ROWS = 256
ROW_TM = 640
FOX_T = 512
LOG2E = 1.4426950408889634
PRUNE_BITS = 160.0
NORM_UP = 1.01
FAST_MAX_BITS = 50.0
VMEM_LIMIT = 56 * 1024 * 1024

BF16 = jnp.bfloat16
F32 = jnp.float32

C_FQ, C_FK, C_FV = 0, 512, 1024
C_GQ, C_GK, C_GV = 1536, 1792, 2048
C_GR, C_GMA, C_GMB = 2560, 3072, 4096
C_SMALL = 5120
N_IN = 5144
N_IN_PERM = 5248
W_IN_MOVES = ((0, 0, 1536), (1536, 1544, 1024), (2560, 2584, 2560), (5120, 1536, 8), (5128, 2568, 16))


def _dot(a, b):
    return jnp.dot(a, b, preferred_element_type=F32)


def _dot_nt(a, b):
    return lax.dot_general(a, b, (((1,), (1,)), ((), ())), preferred_element_type=F32)


def _dot_tn(a, b):
    return lax.dot_general(a, b, (((0,), (0,)), ((), ())), preferred_element_type=F32)


def _rms(x, g):
    return x * lax.rsqrt(jnp.mean(x * x, axis=-1, keepdims=True) + RMS_EPS) * g


def _log_sigmoid(x):
    return jnp.minimum(x, 0.0) - jnp.log1p(jnp.exp(-jnp.abs(x)))


def _split2(x):
    hi = x.astype(BF16)
    lo = (x - hi.astype(F32)).astype(BF16)
    return hi, lo


def _dot_exact_lhs(m, parts):
    return _dot(m, parts[0]) + _dot(m, parts[1])


def _ffn_rows(h, gpre_ref, wgu_ref, wd_ref, gpost_ref, xn_ref, a_ref):
    m = h.shape[0]
    parts = (slice(0, m // 2), slice(m // 2, m)) if m >= 2 * FFN_TF else (slice(0, m),)
    out = []
    for rows in parts:
        h_p = h[rows]
        xn_ref[rows] = _rms(h_p, gpre_ref[...]).astype(BF16)
        for c in range(D_FF // FFN_TF):
            lo = c * FFN_TF
            g = _dot(xn_ref[rows], wgu_ref[:, lo:lo + FFN_TF])
            u = _dot(xn_ref[rows], wgu_ref[:, D_FF + lo:D_FF + lo + FFN_TF])
            a_ref[rows, lo:lo + FFN_TF] = (g * jax.nn.sigmoid(g) * u).astype(BF16)
        y = _dot(a_ref[rows], wd_ref[...])
        out.append(h_p + 0.5 * _rms(y, gpost_ref[...]))
    return out[0] if len(out) == 1 else jnp.concatenate(out, axis=0)


def _stream_weight(w_hbm, layer, n_rows, stage_ref, sem, sem_row, consume):
    rows = stage_ref.shape[1]
    n = n_rows // rows

    def chunk(c):
        return pltpu.make_async_copy(w_hbm.at[layer, pl.ds(c * rows, rows), :], stage_ref.at[c % 2],
                                     sem.at[sem_row, c % 2])

    chunk(0).start()
    for c in range(n):
        if c + 1 < n:
            chunk(c + 1).start()
        chunk(c).wait()
        consume(slice(c * rows, (c + 1) * rows), stage_ref.at[c % 2])


def _cast_into(dst_ref):
    def consume(rows, staged):
        dst_ref[rows, :] = staged[...].astype(BF16)
    return consume


def _ffn_weights(wgu_hbm, wd_hbm, layer, wgu_ref, wd_ref, sgu_ref, sd_ref, sem):
    @pl.when((pl.program_id(0) == 0) & (pl.program_id(1) == 0))
    def _():
        _stream_weight(wgu_hbm, layer, D_MODEL, sgu_ref, sem, 0, _cast_into(wgu_ref))
        _stream_weight(wd_hbm, layer, D_FF, sd_ref, sem, 1, _cast_into(wd_ref))


def _ffn_kernel(h_ref, gpre_ref, wgu_hbm, wd_hbm, gpost_ref, o_ref,
                xn_ref, a_ref, wgu_ref, wd_ref, sgu_ref, sd_ref, sem, *, layer):
    _ffn_weights(wgu_hbm, wd_hbm, layer, wgu_ref, wd_ref, sgu_ref, sd_ref, sem)
    o_ref[...] = _ffn_rows(h_ref[...], gpre_ref, wgu_ref, wd_ref, gpost_ref, xn_ref, a_ref)


def _ffn_first_kernel(x_ref, p_ref, gpre_ref, wgu_hbm, wd_hbm, gpost_ref, o_ref,
                      xn_ref, a_ref, wgu_ref, wd_ref, sgu_ref, sd_ref, sem, *, layer):
    _ffn_weights(wgu_hbm, wd_hbm, layer, wgu_ref, wd_ref, sgu_ref, sd_ref, sem)
    last = pl.program_id(1) == pl.num_programs(1) - 1

    @pl.when(jnp.logical_not(last))
    def _():
        o_ref[...] = _ffn_rows(x_ref[...], gpre_ref, wgu_ref, wd_ref, gpost_ref, xn_ref, a_ref)

    @pl.when(last)
    def _():
        o_ref[0:BLOCK] = _ffn_rows(p_ref[...], gpre_ref, wgu_ref, wd_ref, gpost_ref, xn_ref, a_ref)


def _resident(shape, layer):
    return pl.BlockSpec((None,) + shape, lambda *_: (layer,) + (0,) * len(shape),
                        pipeline_mode=pl.Buffered(1))


def _ffn_scratch(tm):
    return [pltpu.VMEM((tm, D_MODEL), BF16), pltpu.VMEM((tm, D_FF), BF16),
            pltpu.VMEM((D_MODEL, 2 * D_FF), BF16), pltpu.VMEM((D_FF, D_MODEL), BF16),
            pltpu.VMEM((2, FFN_W_ROWS, 2 * D_FF), F32), pltpu.VMEM((2, FFN_W_ROWS, D_MODEL), F32),
            pltpu.SemaphoreType.DMA((2, 2))]


def _ffn(h, gpre, w_gu, w_down, gpost, layer, main_rows=None):
    batch, seq_len, _ = h.shape
    if main_rows is None:
        tm, rows = FFN_TM, seq_len
    else:
        tm, rows = FFN_TM_MAIN, main_rows
    tile = pl.BlockSpec((None, tm, D_MODEL), lambda b, i: (b, i, 0))
    vec = pl.BlockSpec((1, D_MODEL), lambda b, i: (0, 0))
    hbm = pl.BlockSpec(memory_space=pl.ANY)
    return pl.pallas_call(
        functools.partial(_ffn_kernel, layer=layer),
        grid=(batch, rows // tm),
        in_specs=[tile, vec, hbm, hbm, vec],
        out_specs=tile,
        out_shape=jax.ShapeDtypeStruct((batch, rows, D_MODEL), F32),
        scratch_shapes=_ffn_scratch(tm),
        compiler_params=pltpu.CompilerParams(
            dimension_semantics=("arbitrary", "arbitrary"), vmem_limit_bytes=VMEM_LIMIT),
    )(h, gpre, w_gu, w_down, gpost)


def _ffn_first(x, prefix, gpre, w_gu, w_down, gpost, layer):
    batch, seq, _ = x.shape
    tm = FFN_TM_MAIN
    n_main = seq // tm
    vec = pl.BlockSpec((1, D_MODEL), lambda b, i: (0, 0))
    hbm = pl.BlockSpec(memory_space=pl.ANY)
    return pl.pallas_call(
        functools.partial(_ffn_first_kernel, layer=layer),
        grid=(batch, n_main + 1),
        in_specs=[pl.BlockSpec((None, tm, D_MODEL), lambda b, i: (b, jnp.minimum(i, n_main - 1), 0)),
                  pl.BlockSpec((None, BLOCK, D_MODEL), lambda b, i: (b, 0, 0)),
                  vec, hbm, hbm, vec],
        out_specs=pl.BlockSpec((None, tm, D_MODEL), lambda b, i: (b, i, 0)),
        out_shape=jax.ShapeDtypeStruct((batch, seq + BLOCK, D_MODEL), F32),
        scratch_shapes=_ffn_scratch(tm),
        compiler_params=pltpu.CompilerParams(
            dimension_semantics=("arbitrary", "arbitrary"), vmem_limit_bytes=VMEM_LIMIT),
    )(x, prefix, gpre, w_gu, w_down, gpost)


def _proj_kernel(h_ref, g_ref, w_ref, wa_ref, ba_ref, bf_ref,
                 fq_ref, fk_ref, fv_ref, gq_ref, gk_ref, gv_ref, gr_ref, gma_ref, gmb_ref,
                 la_ref, lf_ref, ws_ref, *, tiles_per_seq, main_rows):
    @pl.when(pl.program_id(0) == 0)
    def _():
        for dst, src, width in W_IN_MOVES:
            ws_ref[:, dst:dst + width] = w_ref[:, src:src + width]
        ws_ref[:, N_IN:] = jnp.zeros((D_MODEL, N_IN_PERM - N_IN), BF16)

    w_ref = ws_ref
    tm = h_ref.shape[0]
    m = tm // 2
    halves = (slice(0, m), slice(m, tm))
    xns = {}
    for rows in halves:
        if rows.start not in xns:
            xns[rows.start] = _rms(h_ref[rows, :], g_ref[...]).astype(BF16)
        xn = xns[rows.start]
        pos = ((pl.program_id(0) % tiles_per_seq) * tm + rows.start
               + lax.broadcasted_iota(jnp.int32, (m, 1), 0))
        valid = jnp.logical_or(pos < main_rows, pos >= main_rows + N_PAD).astype(F32)

        def proj(c0, width):
            return _dot(xn, w_ref[:, c0:c0 + width])

        small = proj(C_SMALL, LANES)
        if rows.start == 0:
            xns[m] = _rms(h_ref[halves[1], :], g_ref[...]).astype(BF16)
        fq_ref[rows, :] = (proj(C_FQ, FOX_WIDTH) * (FOX_HEAD_DIM ** -0.5 * LOG2E)).astype(BF16)
        fk_ref[rows, :] = proj(C_FK, FOX_WIDTH).astype(BF16)
        lane = lax.broadcasted_iota(jnp.int32, (m, LANES), 1)
        lf = _log_sigmoid(small + bf_ref[...]) * (valid * LOG2E)
        lf_ref[rows, :] = jnp.where(lane < FOX_HEADS, lf, 0.0)
        xa = _dot(small.astype(BF16), wa_ref[...]) + ba_ref[...]
        la_ref[rows, :] = _log_sigmoid(xa) * (valid * (1.0 / GLA_GATE_TEMP))
        fv_ref[rows, :] = proj(C_FV, FOX_WIDTH).astype(BF16)
        gq_ref[rows, :] = proj(C_GQ, GLA_KWIDTH) * (GLA_DK ** -0.5)
        gk_ref[rows, :] = proj(C_GK, GLA_KWIDTH) * valid
        gv_ref[rows, :] = proj(C_GV, GLA_WIDTH).astype(BF16)
        gr_ref[rows, :] = proj(C_GR, GLA_WIDTH).astype(BF16)
        gma_ref[rows, :] = proj(C_GMA, D_MODEL).astype(BF16)
        gmb_ref[rows, :] = proj(C_GMB, D_MODEL).astype(BF16)


def _proj(h, g, w_perm, wa_pad, ba, bf_pad, seq_len, layer):
    n = h.shape[0]
    tm = ROW_TM
    row = lambda width: pl.BlockSpec((tm, width), lambda i: (i, 0))
    full = lambda a: (_resident(a.shape[1:], layer) if a.ndim == 3
                      else pl.BlockSpec(a.shape, lambda i: (0, 0)))
    outs = [
        (FOX_WIDTH, BF16), (FOX_WIDTH, BF16), (FOX_WIDTH, BF16),
        (GLA_KWIDTH, F32), (GLA_KWIDTH, F32), (GLA_WIDTH, BF16),
        (GLA_WIDTH, BF16), (D_MODEL, BF16), (D_MODEL, BF16),
        (GLA_KWIDTH, F32), (LANES, F32),
    ]
    return pl.pallas_call(
        functools.partial(_proj_kernel, tiles_per_seq=seq_len // tm, main_rows=seq_len - BLOCK),
        grid=(n // tm,),
        in_specs=[row(D_MODEL), full(g), full(w_perm), full(wa_pad), full(ba), full(bf_pad)],
        out_specs=[row(w) for w, _ in outs],
        out_shape=[jax.ShapeDtypeStruct((n, w), dt) for w, dt in outs],
        scratch_shapes=[pltpu.VMEM((D_MODEL, N_IN_PERM), BF16)],
        compiler_params=pltpu.CompilerParams(
            dimension_semantics=("arbitrary",), vmem_limit_bytes=VMEM_LIMIT),
    )(h, g, w_perm, wa_pad, ba, bf_pad)


def _gla_kernel(q_ref, k_ref, v_ref, la_ref, lf_ref, o_ref, cc_ref, cr_ref, st_ref, carry_ref):
    @pl.when(pl.program_id(0) == 0)
    def _():
        st_ref[...] = jnp.zeros_like(st_ref)
        carry_ref[...] = jnp.zeros_like(carry_ref)

    row = lax.broadcasted_iota(jnp.int32, (BLOCK, BLOCK), 0)
    col = lax.broadcasted_iota(jnp.int32, (BLOCK, BLOCK), 1)
    causal = row >= col
    tril = causal.astype(BF16)
    tril_sub = (causal & ((row // SUB) == (col // SUB))).astype(BF16)
    row_w = lax.broadcasted_iota(jnp.int32, (BLOCK, GLA_KWIDTH), 0)
    row_blk = row // SUB

    batches = range(q_ref.shape[0])
    problems = [(b, h) for b in batches for h in range(GLA_HEADS)]

    cb_all, cw_all = [], []
    for b in batches:
        cl = _dot_exact_lhs(tril, _split2(lf_ref[b])) + carry_ref[b]
        carry_ref[b] = cl[BLOCK - 1:BLOCK, :]
        cc_ref[b] = cl
        cr_ref[b] = cl.T[0:FOX_HEADS, :]
        la_parts = _split2(la_ref[b])
        cb_all.append(_dot_exact_lhs(tril, la_parts))
        cw_all.append(_dot_exact_lhs(tril_sub, la_parts))

    ops = {}
    for b in batches:
        cb, cw = cb_all[b], cw_all[b]
        last = cb[BLOCK - 1:BLOCK, :]
        q = q_ref[b]
        k = k_ref[b]
        q_glob = (q * jnp.exp(cb)).astype(BF16)
        q_loc = (q * jnp.exp(cw)).astype(BF16)
        k_end = (k * jnp.exp(last - cb)).astype(BF16)
        decay = jnp.exp(last)
        k_sub = []
        for i in range(N_SUB):
            ref = cb[i * SUB - 1:i * SUB, :] if i > 0 else jnp.zeros_like(last)
            e = jnp.exp(jnp.minimum(ref - cb, EXP_CAP))
            k_sub.append(jnp.where(row_w < (i + 1) * SUB, k * e, 0.0).astype(BF16))
        for pair in range(GLA_HEADS // 2):
            sl = slice(pair * LANES, (pair + 1) * LANES)
            k_cat = jnp.concatenate([ks[:, sl] for ks in k_sub], axis=1)
            for half in range(2):
                h = 2 * pair + half
                head_lanes = (col < GLA_DK) if half == 0 else (col >= GLA_DK)
                q_h = jnp.where(head_lanes, q_loc[:, sl], 0.0)
                q_parts = []
                for p in range(N_SUB // 2):
                    r = slice(2 * p * SUB, (2 * p + 2) * SUB)
                    q_parts.append(jnp.concatenate(
                        [jnp.where(row_blk[r] == 2 * p, q_h[r], 0.0), jnp.where(row_blk[r] == 2 * p + 1, q_h[r], 0.0)],
                        axis=1))
                ops[b, h] = (q_parts, k_cat, jnp.where(head_lanes, q_glob[:, sl], 0.0), k_end[:, sl],
                             decay[:, sl], v_ref[b, :, h * GLA_DV:(h + 1) * GLA_DV])

    att, inter, upd = {}, {}, {}
    for p in problems:
        q_parts, k_cat, q_g, k_e, _, v_h = ops[p]
        att[p] = jnp.concatenate(
            [_dot_nt(q_p, k_cat[:, 2 * i * LANES:(2 * i + 2) * LANES]) for i, q_p in enumerate(q_parts)],
            axis=0)
        inter[p] = _dot_nt(q_g, st_ref[p].astype(BF16))
        upd[p] = _dot_tn(v_h, k_e)

    for p in problems:
        b, h = p
        a = jnp.where(causal, att[p], 0.0).astype(BF16)
        o_ref[b, :, h * GLA_DV:(h + 1) * GLA_DV] = (inter[p] + _dot(a, ops[p][5])).astype(o_ref.dtype)
        st_ref[p] = st_ref[p] * ops[p][4] + upd[p]


def _gla(gq, gk, gv, la, lf, batch, seq_len):
    nc = seq_len // BLOCK
    chunk = lambda c: (c + nc - 1) % nc
    blk = lambda width: pl.BlockSpec((batch, BLOCK, width), lambda c: (0, chunk(c), 0))
    r3 = lambda a: a.reshape(batch, seq_len, a.shape[-1])
    return pl.pallas_call(
        _gla_kernel,
        grid=(nc,),
        in_specs=[blk(GLA_KWIDTH), blk(GLA_KWIDTH), blk(GLA_WIDTH), blk(GLA_KWIDTH), blk(LANES)],
        out_specs=[blk(GLA_WIDTH), blk(LANES),
                   pl.BlockSpec((batch, FOX_HEADS, BLOCK), lambda c: (0, 0, chunk(c)))],
        out_shape=[jax.ShapeDtypeStruct((batch, seq_len, GLA_WIDTH), BF16),
                   jax.ShapeDtypeStruct((batch, seq_len, LANES), F32),
                   jax.ShapeDtypeStruct((batch, FOX_HEADS, seq_len), F32)],
        scratch_shapes=[pltpu.VMEM((batch, GLA_HEADS, GLA_DV, LANES), F32),
                        pltpu.VMEM((batch, 1, LANES), F32)],
        compiler_params=pltpu.CompilerParams(
            dimension_semantics=("arbitrary",), vmem_limit_bytes=VMEM_LIMIT),
    )(r3(gq), r3(gk), r3(gv), r3(la), r3(lf))


def _fox_kernel(q_ref, k_ref, v_ref, cc_ref, crp_ref, crm_ref, cb_ref, cs_ref, o_ref,
                ra_ref, rb_ref, ub_ref, plan_ref, m_ref, l_ref, acc_ref):
    t = FOX_T
    nt = crm_ref.shape[2]
    s_main = nt * t
    pre = slice(s_main, s_main + BLOCK)
    pair = pl.program_id(1)

    def stack_heads(x, lane_first):
        return jnp.concatenate([jnp.where(lane_first, x, 0.0), jnp.where(lane_first, 0.0, x)], axis=0)

    def head_cols(cc, lane):
        return jnp.concatenate(
            [jnp.sum(jnp.where(lane == 2 * pair + hh, cc, 0.0), axis=1, keepdims=True)
             for hh in range(2)], axis=0)

    lane_b = lax.broadcasted_iota(jnp.int32, (BLOCK, LANES), 1)
    first_b = lane_b < FOX_HEAD_DIM
    kp = k_ref[0, pre, :]
    vp = v_ref[0, pre, :]
    ckp = crp_ref[0, 0]
    row_b = lax.broadcasted_iota(jnp.int32, (BLOCK, BLOCK), 0)
    ok_b = (lane_b <= row_b) & ((lane_b >= N_PAD) | (lane_b == row_b))
    ok_b = jnp.concatenate([ok_b, ok_b], axis=0)
    s = _dot_nt(stack_heads(q_ref[0, pre, :], first_b), kp)
    z = jnp.concatenate([s[:BLOCK] - ckp[0:1], s[BLOCK:] - ckp[1:2]], axis=0)
    z = jnp.where(ok_b, z, NEG)
    cq2 = head_cols(cc_ref[0, pre, :], lane_b)
    m = jnp.max(z, axis=1, keepdims=True) + cq2
    p = jnp.exp2(z - (m - cq2))
    out = _dot(p.astype(BF16), vp) / jnp.sum(p, axis=1, keepdims=True)
    o_ref[0, pre, :] = jnp.where(first_b, out[:BLOCK], out[BLOCK:]).astype(o_ref.dtype)

    def v_with_ones(rows, head_b):
        v = v_ref[0, rows, :]
        lane_v = lax.broadcasted_iota(jnp.int32, v.shape, 1)
        if head_b:
            return jnp.where(lane_v >= FOX_HEAD_DIM, v, jnp.where(lane_v == 0, 1.0, 0.0).astype(BF16))
        return jnp.where(lane_v < FOX_HEAD_DIM, v, jnp.where(lane_v == FOX_HEAD_DIM, 1.0, 0.0).astype(BF16))

    lane = lax.broadcasted_iota(jnp.int32, (t, LANES), 1)
    first = lane < FOX_HEAD_DIM
    row_l = lax.broadcasted_iota(jnp.int32, (LANES, LANES), 0)
    row_w = lax.broadcasted_iota(jnp.int32, (LANES, 2 * LANES), 0)
    second = lax.broadcasted_iota(jnp.int32, (LANES, 2 * LANES), 1) >= LANES
    ind_ab = ((row_w >= FOX_HEAD_DIM) == second).astype(BF16)
    hot_ab = (row_w == 2 * pair + second.astype(jnp.int32)).astype(BF16)
    hot_la = (row_l == FOX_HEAD_DIM).astype(BF16)
    hot_lb = (row_l == 0).astype(BF16)
    reps = t // LANES
    th = t // 2

    sel = lax.broadcasted_iota(jnp.int32, (8, LANES), 0) == lax.broadcasted_iota(
        jnp.int32, (8, LANES), 1) // FOX_HEAD_DIM
    k_all = k_ref[0]
    k_max = jnp.sqrt(jnp.max(_dot_nt(sel.astype(BF16), k_all * k_all), axis=1, keepdims=True))

    per_step = next(u for u in (4, 2, 1) if nt % u == 0)

    def prepare(step, carry):
        tiles_here = [step * per_step + u for u in range(per_step)]
        rows = [pl.ds(pl.multiple_of(i * t, t), t) for i in tiles_here]
        nqs = [_dot(q_ref[0, r, :] * q_ref[0, r, :], ind_ab) for r in rows]
        cqs = [_dot(cc_ref[0, r, :].astype(BF16), hot_ab) for r in rows]
        for i, r, nq, cq in zip(tiles_here, rows, nqs, cqs):
            ub_a = jnp.sqrt(nq[:, :LANES]) * (k_max[0:1] * NORM_UP)
            ub_b = jnp.sqrt(nq[:, LANES:]) * (k_max[1:2] * NORM_UP)
            ra_ref[r, :] = ub_a - cq[:, :LANES]
            rb_ref[r, :] = ub_b - cq[:, LANES:]
            ub_ref[pl.ds(i, 1), :] = jnp.max(jnp.maximum(ub_a, ub_b), axis=0, keepdims=True)
        return carry

    lax.fori_loop(0, nt // per_step, prepare, 0)

    ub_t = ub_ref[0:nt, :]
    fast_v = ub_t <= FAST_MAX_BITS
    bound = jnp.where(fast_v, PRUNE_BITS, PRUNE_BITS + 2.0 * ub_t)
    cb = cb_ref[0, 0]
    dead = ((cs_ref[0, 0, 0, 0, 0:nt, :] - cb[0:1]) < -bound) & ((cs_ref[0, 0, 1, 0, 0:nt, :] - cb[1:2]) < -bound)
    tile_q = lax.broadcasted_iota(jnp.int32, (nt, LANES), 0)
    tile_k = lax.broadcasted_iota(jnp.int32, (nt, LANES), 1)
    n_skip_v = jnp.sum(jnp.where(dead & (tile_k < tile_q), 1, 0), axis=1, keepdims=True)
    skip_pre_v = jnp.sum(jnp.where(dead & (tile_k == nt), 1, 0), axis=1, keepdims=True)
    dead_lo = ((cs_ref[0, 0, 0, 1, 0:nt, :] - cb[0:1]) < -bound) & ((cs_ref[0, 0, 1, 1, 0:nt, :] - cb[1:2]) < -bound)
    old_dead_v = jnp.sum(jnp.where(dead_lo & (tile_k == tile_q + (nt - 1)), 1, 0), axis=1, keepdims=True)
    fast_i = jnp.where(fast_v[:, 0:1], 1, 0)
    tile_1 = tile_q[:, 0:1]
    for i in range(nt):
        for row, vec in enumerate((fast_i, n_skip_v, skip_pre_v, old_dead_v)):
            plan_ref[row, i] = jnp.sum(jnp.where(tile_1 == i, vec, 0))

    def main_tile(i, carry):
        r0 = pl.multiple_of(i * t, t)
        q2 = stack_heads(q_ref[0, pl.ds(r0, t), :], first)
        fast = plan_ref[0, i] == 1
        n_skip = plan_ref[1, i]
        skip_prefix = plan_ref[2, i]
        old_half_dead = plan_ref[3, i] == 1

        def logits(kt, ck, offs):
            s = _dot_nt(q2, kt)
            return jnp.concatenate([s[:t] - ck[0:1] - offs[:t], s[t:] - ck[1:2] - offs[t:]], axis=0)

        def run(update, offs_p, offs_m, fuse_tail):
            acc_ref[...] = jnp.zeros_like(acc_ref)

            @pl.when(skip_prefix == 0)
            def _():
                z = logits(kp, ckp, offs_p)
                lane_k = lax.broadcasted_iota(jnp.int32, (2 * t, BLOCK), 1)
                update([jnp.where(lane_k >= N_PAD, z, NEG)], [pre])

            def tiles(js, diagonal, old_half_dead=False):
                split = diagonal and fuse_tail
                full = js[:-1] if split else js
                upper = None
                if old_half_dead:
                    full = full[1:]
                    base = pl.multiple_of(js[0] * t, t)
                    ck = crm_ref[0, 0, js[0]]
                    r_old = pl.ds(base, th)
                    q_up = jnp.concatenate([q2[0:th], q2[t:t + th]], axis=0)
                    o_up = jnp.concatenate([offs_m[0:th, :th], offs_m[t:t + th, :th]], axis=0)
                    s = _dot_nt(q_up, k_ref[0, r_old, :])
                    upper = (jnp.concatenate([s[:th] - ck[0:1, :th] - o_up[:th], s[th:] - ck[1:2, :th] - o_up[th:]],
                                             axis=0), r_old)
                    r_new = pl.ds(base + th, th)
                    z_new = logits(k_ref[0, r_new, :], ck[:, th:], offs_m[:, :th])
                rows = [pl.ds(pl.multiple_of(j * t, t), t) for j in full]
                zs = [logits(k_ref[0, r, :], crm_ref[0, 0, j], offs_m) for j, r in zip(full, rows)]
                if old_half_dead:
                    zs, rows = [z_new] + zs, [r_new] + rows
                if not diagonal:
                    update(zs, rows)
                    return
                if not split:
                    ok = lax.broadcasted_iota(jnp.int32, (t, t), 1) <= lax.broadcasted_iota(jnp.int32, (t, t), 0)
                    zs[-1] = jnp.where(jnp.concatenate([ok, ok], axis=0), zs[-1], NEG)
                    update(zs, rows)
                    return
                base = pl.multiple_of(js[-1] * t, t)
                ck = crm_ref[0, 0, js[-1]]
                tri = lax.broadcasted_iota(jnp.int32, (th, th), 1) <= lax.broadcasted_iota(jnp.int32, (th, th), 0)
                r_left = pl.ds(base, th)
                z_left = logits(k_ref[0, r_left, :], ck[:, :th], offs_m[:, :th])
                lower_rows = (lax.broadcasted_iota(jnp.int32, (2 * t, th), 0) // th) % 2 == 1
                z_left = jnp.where(lower_rows | jnp.concatenate([tri] * 4, axis=0), z_left, NEG)
                r_right = pl.ds(base + th, th)
                q_lo = jnp.concatenate([q2[th:t], q2[t + th:]], axis=0)
                o_lo = jnp.concatenate([offs_m[th:t, :th], offs_m[t + th:, :th]], axis=0)
                s = _dot_nt(q_lo, k_ref[0, r_right, :])
                z_right = jnp.concatenate([s[:th] - ck[0:1, th:] - o_lo[:th], s[th:] - ck[1:2, th:] - o_lo[th:]],
                                          axis=0)
                z_right = jnp.where(jnp.concatenate([tri, tri], axis=0), z_right, NEG)
                update(zs + [z_left], rows + [r_left], (z_right, r_right), upper)

            def body(j, c):
                tiles([j], False)
                return c

            if not fuse_tail:
                lax.fori_loop(n_skip, i, body, 0)
                tiles([i], True)
                return
            n_live = i - n_skip
            lax.fori_loop(n_skip, jnp.maximum(i - 2, n_skip), body, 0)

            @pl.when((n_live >= 2) & old_half_dead)
            def _():
                tiles([i - 2, i - 1, i], True, True)

            @pl.when((n_live >= 2) & jnp.logical_not(old_half_dead))
            def _():
                tiles([i - 2, i - 1, i], True)

            @pl.when(n_live == 1)
            def _():
                tiles([i - 1, i], True)

            @pl.when(n_live == 0)
            def _():
                tiles([i], True)

        def finish(l_a, l_b):
            acc = acc_ref[...]
            out = jnp.where(first, acc[:t] / l_a, acc[t:] / l_b)
            o_ref[0, pl.ds(r0, t), :] = out.astype(o_ref.dtype)

        @pl.when(fast)
        def _():
            ref = jnp.concatenate([ra_ref[pl.ds(r0, t), :], rb_ref[pl.ds(r0, t), :]], axis=0)

            def update(zs, rows, lower=None, upper=None):
                ps = [jnp.exp2(z).astype(BF16) for z in zs]
                p_lo = None if lower is None else jnp.exp2(lower[0]).astype(BF16)
                if upper is not None:
                    p_up = jnp.exp2(upper[0]).astype(BF16)
                    acc_ref[0:th] += _dot(p_up[:th], v_with_ones(upper[1], False))
                    acc_ref[t:t + th] += _dot(p_up[th:], v_with_ones(upper[1], True))
                for half, head_b in ((slice(0, t), False), (slice(t, 2 * t), True)):
                    pv = _dot(ps[0][half], v_with_ones(rows[0], head_b))
                    for p, r in zip(ps[1:], rows[1:]):
                        pv = pv + _dot(p[half], v_with_ones(r, head_b))
                    acc_ref[half] += pv
                if lower is not None:
                    acc_ref[th:t] += _dot(p_lo[:th], v_with_ones(lower[1], False))
                    acc_ref[t + th:] += _dot(p_lo[th:], v_with_ones(lower[1], True))

            run(update, ref, jnp.concatenate([ref] * reps, axis=1), True)

            def denominators(acc, hot):
                hi = acc.astype(BF16)
                lo = (acc - hi.astype(F32)).astype(BF16)
                return _dot(hi, hot) + _dot(lo, hot)

            finish(denominators(acc_ref[0:t], hot_la), denominators(acc_ref[t:], hot_lb))

        @pl.when(jnp.logical_not(fast))
        def _():
            cq2 = head_cols(cc_ref[0, pl.ds(r0, t), :], lane)
            m_ref[...] = jnp.full_like(m_ref, NEG)
            l_ref[...] = jnp.zeros_like(l_ref)

            def update(zs, rows):
                (z,), (r,) = zs, rows
                m_old = m_ref[...]
                m_new = jnp.maximum(m_old, jnp.max(z, axis=1, keepdims=True) + cq2)
                p = jnp.exp2(z - (m_new - cq2))
                alpha = jnp.exp2(m_old - m_new)
                l_ref[...] = alpha * l_ref[...] + jnp.sum(p, axis=1, keepdims=True)
                acc_ref[...] = alpha * acc_ref[...] + _dot(p.astype(BF16), v_ref[0, r, :])
                m_ref[...] = m_new

            run(update, jnp.zeros((2 * t, 1), F32), jnp.zeros((2 * t, 1), F32), False)
            finish(l_ref[0:t], l_ref[t:])

        return carry

    lax.fori_loop(0, nt, main_tile, 0)


def _fox(fq, fk, fv, cum_col, cum_row, batch, seq_len):
    t = FOX_T
    nt = (seq_len - BLOCK) // t
    npairs = FOX_HEADS // 2
    assert 2 * nt + 1 <= LANES
    s_main = nt * t
    r3 = lambda a: a.reshape(batch, seq_len, a.shape[-1])
    cr = cum_row.reshape(batch, npairs, 2, seq_len)
    cr_prefix = cr[..., s_main:]
    cr_main = cr[..., :s_main].reshape(batch, npairs, 2, nt, t).transpose(0, 1, 3, 2, 4)
    th = t // 2
    bounds = jnp.concatenate(
        [cr[..., t - 1:s_main:t], cr[..., seq_len - 1:], cr[..., th - 1:s_main:t],
         jnp.zeros((batch, npairs, 2, LANES - 2 * nt - 1), F32)], axis=-1)
    nt8 = -(-nt // 8) * 8
    starts = jnp.stack([cr[..., 0:s_main:t], cr[..., th:s_main:t]], axis=3)
    starts = jnp.pad(starts, ((0, 0),) * 4 + ((0, nt8 - nt),))
    starts = jnp.broadcast_to(starts[..., None], (batch, npairs, 2, 2, nt8, LANES))
    whole = pl.BlockSpec((1, seq_len, LANES), lambda b, p: (b, 0, p))
    out = pl.pallas_call(
        _fox_kernel,
        grid=(batch, npairs),
        in_specs=[whole, whole, whole,
                  pl.BlockSpec((1, seq_len, LANES), lambda b, p: (b, 0, 0)),
                  pl.BlockSpec((1, 1, 2, BLOCK), lambda b, p: (b, p, 0, 0)),
                  pl.BlockSpec((1, 1, nt, 2, t), lambda b, p: (b, p, 0, 0, 0)),
                  pl.BlockSpec((1, 1, 2, LANES), lambda b, p: (b, p, 0, 0)),
                  pl.BlockSpec((1, 1, 2, 2, nt8, LANES), lambda b, p: (b, p, 0, 0, 0, 0))],
        out_specs=whole,
        out_shape=jax.ShapeDtypeStruct((batch, seq_len, FOX_WIDTH), BF16),
        scratch_shapes=[pltpu.VMEM((s_main, LANES), F32), pltpu.VMEM((s_main, LANES), F32),
                        pltpu.VMEM((nt8, LANES), F32),
                        pltpu.SMEM((4, nt), jnp.int32),
                        pltpu.VMEM((2 * t, 1), F32), pltpu.VMEM((2 * t, 1), F32),
                        pltpu.VMEM((2 * t, LANES), F32)],
        compiler_params=pltpu.CompilerParams(
            dimension_semantics=("parallel", "parallel"), vmem_limit_bytes=VMEM_LIMIT),
    )(r3(fq), r3(fk), r3(fv), cum_col, cr_prefix, cr_main, bounds, starts)
    return out.reshape(batch * seq_len, FOX_WIDTH)


def _post_kernel(of_ref, og_ref, gr_ref, gma_ref, gmb_ref, h_ref, ggla_ref, wfo_ref, wgo_ref,
                 wout_ref, gpost_ref, o_ref):
    tm = h_ref.shape[0]
    halves = (slice(0, tm // 2), slice(tm // 2, tm))
    ggla = ggla_ref[...]
    a = [_dot(of_ref[hs, :], wfo_ref[...]) for hs in halves]
    b = []
    for hs in halves:
        og = og_ref[hs, :].astype(F32)
        heads = [_rms(og[:, hh * GLA_DV:(hh + 1) * GLA_DV], ggla[:, hh * GLA_DV:(hh + 1) * GLA_DV])
                 for hh in range(GLA_HEADS)]
        gr = gr_ref[hs, :].astype(F32)
        o_gla = (jnp.concatenate(heads, axis=1) * (gr * jax.nn.sigmoid(gr))).astype(BF16)
        b.append(_dot(o_gla, wgo_ref[...]))
    mix = []
    for hs, a_h, b_h in zip(halves, a, b):
        y = jax.nn.sigmoid(gma_ref[hs, :].astype(F32)) * a_h + jax.nn.sigmoid(gmb_ref[hs, :].astype(F32)) * b_h
        mix.append(_dot(y.astype(BF16), wout_ref[...]))
    for hs, m_h in zip(halves, mix):
        o_ref[hs, :] = h_ref[hs, :] + _rms(m_h, gpost_ref[...])


def _post(o_fox, o_gla, gr, gma, gmb, h, ggla, wfo, wgo, wout, gpost, layer):
    n = h.shape[0]
    tm = ROW_TM
    row = lambda a: pl.BlockSpec((tm, a.shape[1]), lambda i: (i, 0))
    full = lambda a: (_resident(a.shape[1:], layer) if a.ndim == 3
                      else pl.BlockSpec(a.shape, lambda i: (0, 0)))
    args = (o_fox, o_gla, gr, gma, gmb, h, ggla, wfo, wgo, wout, gpost)
    return pl.pallas_call(
        _post_kernel,
        grid=(n // tm,),
        in_specs=[row(a) for a in args[:6]] + [full(a) for a in args[6:]],
        out_specs=pl.BlockSpec((tm, D_MODEL), lambda i: (i, 0)),
        out_shape=jax.ShapeDtypeStruct((n, D_MODEL), F32),
        compiler_params=pltpu.CompilerParams(
            dimension_semantics=("parallel",), vmem_limit_bytes=VMEM_LIMIT),
    )(*args)


def kernel(x, meta_tokens, w_in, w_alpha_up, b_alpha, b_f, g_gla_out, w_fox_o, w_gla_o, w_out,
           w_ffn1_gu, w_ffn1_down, w_ffn2_gu, w_ffn2_down,
           g_pre_ffn1, g_post_ffn1, g_pre_mix, g_post_mix, g_pre_ffn2, g_post_ffn2):
    batch, seq, d = x.shape
    seq_len = seq + BLOCK
    depth = w_in.shape[0]
    assert d == D_MODEL and (seq_len - BLOCK) % FOX_T == 0 and seq_len % ROW_TM == 0
    assert seq_len % FFN_TM == 0 and seq % FFN_TM_MAIN == 0

    pad = jnp.zeros((batch, N_PAD, d), x.dtype)
    meta = jnp.broadcast_to(meta_tokens.astype(x.dtype)[None], (batch, N_META, d))
    prefix = jnp.concatenate([pad, meta], axis=1)
    flat = lambda a: a.reshape(batch * seq_len, a.shape[-1])

    row2 = lambda a: a.reshape(1, -1).astype(F32)
    w_perm = w_in.astype(BF16)
    w1_gu, w1_down, w2_gu, w2_down = w_ffn1_gu, w_ffn1_down, w_ffn2_gu, w_ffn2_down
    wfo, wgo, wout = w_fox_o.astype(BF16), w_gla_o.astype(BF16), w_out.astype(BF16)
    for l in range(depth):
        wa_pad = jnp.zeros((LANES, GLA_KWIDTH), F32).at[
            FOX_HEADS:FOX_HEADS + GLA_GATE_RANK].set(w_alpha_up[l]).astype(BF16)
        bf_pad = jnp.zeros((1, LANES), F32).at[0, :FOX_HEADS].set(b_f[l])

        if l == 0:
            h = _ffn_first(x, prefix, row2(g_pre_ffn1[l]), w1_gu, w1_down, row2(g_post_ffn1[l]), l)
        else:
            h = _ffn(h, row2(g_pre_ffn1[l]), w1_gu, w1_down, row2(g_post_ffn1[l]), l)
        fq, fk, fv, gq, gk, gv, gr, gma, gmb, la, lf = _proj(
            flat(h), row2(g_pre_mix[l]), w_perm, wa_pad, row2(b_alpha[l]), bf_pad, seq_len, l)
        o_gla, cum_col, cum_row = _gla(gq, gk, gv, la, lf, batch, seq_len)
        o_fox = _fox(fq, fk, fv, cum_col, cum_row, batch, seq_len)
        h = _post(o_fox, flat(o_gla), gr, gma, gmb, flat(h),
                  row2(g_gla_out[l]), wfo, wgo, wout, row2(g_post_mix[l]), l).reshape(batch, seq_len, d)
        h = _ffn(h, row2(g_pre_ffn2[l]), w2_gu, w2_down, row2(g_post_ffn2[l]), l,
                 main_rows=seq if l == depth - 1 else None)
    return h
```

```python
import functools

import jax
import jax.numpy as jnp
import numpy as np
from jax import lax
from jax.experimental import pallas as pl
from jax.experimental.pallas import tpu as pltpu

D_MODEL = 1024
N_META = 16
BLOCK = 128
N_PAD = BLOCK - N_META
FOX_HEADS = 8
FOX_HEAD_DIM = 64
FOX_WIDTH = FOX_HEADS * FOX_HEAD_DIM
GLA_HEADS = 4
GLA_WIDTH = 512
GLA_DV = 128
GLA_DK = 64
GLA_KWIDTH = GLA_HEADS * GLA_DK
GLA_GATE_RANK = 16
GLA_GATE_TEMP = 16.0
D_FF = 2816
RMS_EPS = 1e-6

LANES = 128
SUB = 16
N_SUB = BLOCK // SUB
EXP_CAP = 80.0
NEG = -1e30

FFN_TM = 640
FFN_TM_MAIN = 512
FFN_TF = 256
FFN_W_ROWS = 256
ROW_TM = 640
MERGE_TN = 256
FOX_T = 512
LOG2E = 1.4426950408889634
PRUNE_BITS = 160.0
NORM_UP = 1.01
FAST_MAX_BITS = 50.0
VMEM_LIMIT = 56 * 1024 * 1024

BF16 = jnp.bfloat16
F32 = jnp.float32

C_FQ, C_FK, C_FV = 0, 512, 1024
C_GQ, C_GK, C_GV = 1536, 1792, 2048
C_GR, C_GMA, C_GMB = 2560, 3072, 4096
C_SMALL = 5120
N_IN = 5144
N_IN_PERM = 5248
W_IN_MOVES = ((0, 0, 1536), (1536, 1544, 1024), (2560, 2584, 2560), (5120, 1536, 8), (5128, 2568, 16))


def _dot(a, b):
    return jnp.dot(a, b, preferred_element_type=F32)


def _dot_nt(a, b):
    return lax.dot_general(a, b, (((1,), (1,)), ((), ())), preferred_element_type=F32)


def _dot_tn(a, b):
    return lax.dot_general(a, b, (((0,), (0,)), ((), ())), preferred_element_type=F32)


def _rms(x, g):
    return x * lax.rsqrt(jnp.mean(x * x, axis=-1, keepdims=True) + RMS_EPS) * g


def _log_sigmoid(x):
    return jnp.minimum(x, 0.0) - jnp.log1p(jnp.exp(-jnp.abs(x)))


def _split2(x):
    hi = x.astype(BF16)
    lo = (x - hi.astype(F32)).astype(BF16)
    return hi, lo


def _dot_exact_lhs(m, parts):
    return _dot(m, parts[0]) + _dot(m, parts[1])


def _ffn_rows(h, gpre_ref, wgu_ref, wd_ref, gpost_ref, xn_ref, a_ref):
    m = h.shape[0]
    parts = (slice(0, m // 2), slice(m // 2, m)) if m >= 2 * FFN_TF else (slice(0, m),)
    out = []
    for rows in parts:
        h_p = h[rows]
        xn_ref[rows] = _rms(h_p, gpre_ref[...]).astype(BF16)
        for c in range(D_FF // FFN_TF):
            lo = c * FFN_TF
            g = _dot(xn_ref[rows], wgu_ref[:, lo:lo + FFN_TF])
            u = _dot(xn_ref[rows], wgu_ref[:, D_FF + lo:D_FF + lo + FFN_TF])
            a_ref[rows, lo:lo + FFN_TF] = (g * jax.nn.sigmoid(g) * u).astype(BF16)
        y = _dot(a_ref[rows], wd_ref[...])
        out.append(h_p + 0.5 * _rms(y, gpost_ref[...]))
    return out[0] if len(out) == 1 else jnp.concatenate(out, axis=0)


def _stream_weight(w_hbm, layer, n_rows, stage_ref, sem, sem_row, consume):
    rows = stage_ref.shape[1]
    n = n_rows // rows

    def chunk(c):
        return pltpu.make_async_copy(w_hbm.at[layer, pl.ds(c * rows, rows), :], stage_ref.at[c % 2],
                                     sem.at[sem_row, c % 2])

    chunk(0).start()
    for c in range(n):
        if c + 1 < n:
            chunk(c + 1).start()
        chunk(c).wait()
        consume(slice(c * rows, (c + 1) * rows), stage_ref.at[c % 2])


def _cast_into(dst_ref):
    def consume(rows, staged):
        dst_ref[rows, :] = staged[...].astype(BF16)
    return consume


def _ffn_weights(wgu_hbm, wd_hbm, layer, wgu_ref, wd_ref, sgu_ref, sd_ref, sem):
    @pl.when((pl.program_id(0) == 0) & (pl.program_id(1) == 0))
    def _():
        _stream_weight(wgu_hbm, layer, D_MODEL, sgu_ref, sem, 0, _cast_into(wgu_ref))
        _stream_weight(wd_hbm, layer, D_FF, sd_ref, sem, 1, _cast_into(wd_ref))


def _ffn_kernel(h_ref, gpre_ref, wgu_hbm, wd_hbm, gpost_ref, o_ref,
                xn_ref, a_ref, wgu_ref, wd_ref, sgu_ref, sd_ref, sem, *, layer):
    _ffn_weights(wgu_hbm, wd_hbm, layer, wgu_ref, wd_ref, sgu_ref, sd_ref, sem)
    o_ref[...] = _ffn_rows(h_ref[...], gpre_ref, wgu_ref, wd_ref, gpost_ref, xn_ref, a_ref)


def _ffn_first_kernel(x_ref, p_ref, gpre_ref, wgu_hbm, wd_hbm, gpost_ref, o_ref,
                      xn_ref, a_ref, wgu_ref, wd_ref, sgu_ref, sd_ref, sem, *, layer):
    _ffn_weights(wgu_hbm, wd_hbm, layer, wgu_ref, wd_ref, sgu_ref, sd_ref, sem)
    last = pl.program_id(1) == pl.num_programs(1) - 1

    @pl.when(jnp.logical_not(last))
    def _():
        o_ref[...] = _ffn_rows(x_ref[...], gpre_ref, wgu_ref, wd_ref, gpost_ref, xn_ref, a_ref)

    @pl.when(last)
    def _():
        o_ref[0:BLOCK] = _ffn_rows(p_ref[...], gpre_ref, wgu_ref, wd_ref, gpost_ref, xn_ref, a_ref)


def _resident(shape, layer):
    return pl.BlockSpec((None,) + shape, lambda *_: (layer,) + (0,) * len(shape),
                        pipeline_mode=pl.Buffered(1))


def _ffn_scratch(tm):
    return [pltpu.VMEM((tm, D_MODEL), BF16), pltpu.VMEM((tm, D_FF), BF16),
            pltpu.VMEM((D_MODEL, 2 * D_FF), BF16), pltpu.VMEM((D_FF, D_MODEL), BF16),
            pltpu.VMEM((2, FFN_W_ROWS, 2 * D_FF), F32), pltpu.VMEM((2, FFN_W_ROWS, D_MODEL), F32),
            pltpu.SemaphoreType.DMA((2, 2))]


def _ffn(h, gpre, w_gu, w_down, gpost, layer, main_rows=None):
    batch, seq_len, _ = h.shape
    if main_rows is None:
        tm, rows = FFN_TM, seq_len
    else:
        tm, rows = FFN_TM_MAIN, main_rows
    tile = pl.BlockSpec((None, tm, D_MODEL), lambda b, i: (b, i, 0))
    vec = pl.BlockSpec((1, D_MODEL), lambda b, i: (0, 0))
    hbm = pl.BlockSpec(memory_space=pl.ANY)
    return pl.pallas_call(
        functools.partial(_ffn_kernel, layer=layer),
        grid=(batch, rows // tm),
        in_specs=[tile, vec, hbm, hbm, vec],
        out_specs=tile,
        out_shape=jax.ShapeDtypeStruct((batch, rows, D_MODEL), F32),
        scratch_shapes=_ffn_scratch(tm),
        compiler_params=pltpu.CompilerParams(
            dimension_semantics=("arbitrary", "arbitrary"), vmem_limit_bytes=VMEM_LIMIT),
    )(h, gpre, w_gu, w_down, gpost)


def _ffn_first(x, prefix, gpre, w_gu, w_down, gpost, layer):
    batch, seq, _ = x.shape
    tm = FFN_TM_MAIN
    n_main = seq // tm
    vec = pl.BlockSpec((1, D_MODEL), lambda b, i: (0, 0))
    hbm = pl.BlockSpec(memory_space=pl.ANY)
    return pl.pallas_call(
        functools.partial(_ffn_first_kernel, layer=layer),
        grid=(batch, n_main + 1),
        in_specs=[pl.BlockSpec((None, tm, D_MODEL), lambda b, i: (b, jnp.minimum(i, n_main - 1), 0)),
                  pl.BlockSpec((None, BLOCK, D_MODEL), lambda b, i: (b, 0, 0)),
                  vec, hbm, hbm, vec],
        out_specs=pl.BlockSpec((None, tm, D_MODEL), lambda b, i: (b, i, 0)),
        out_shape=jax.ShapeDtypeStruct((batch, seq + BLOCK, D_MODEL), F32),
        scratch_shapes=_ffn_scratch(tm),
        compiler_params=pltpu.CompilerParams(
            dimension_semantics=("arbitrary", "arbitrary"), vmem_limit_bytes=VMEM_LIMIT),
    )(x, prefix, gpre, w_gu, w_down, gpost)


def _proj_kernel(h_ref, g_ref, w_ref, wa_ref, ba_ref, bf_ref,
                 fq_ref, fk_ref, fv_ref, gq_ref, gk_ref, gv_ref, gr_ref, gma_ref, gmb_ref,
                 la_ref, lf_ref, ws_ref, *, tiles_per_seq, main_rows):
    @pl.when(pl.program_id(0) == 0)
    def _():
        for dst, src, width in W_IN_MOVES:
            ws_ref[:, dst:dst + width] = w_ref[:, src:src + width]
        ws_ref[:, N_IN:] = jnp.zeros((D_MODEL, N_IN_PERM - N_IN), BF16)

    w_ref = ws_ref
    tm = h_ref.shape[0]
    m = tm // 2
    halves = (slice(0, m), slice(m, tm))
    xns = {}
    for rows in halves:
        if rows.start not in xns:
            xns[rows.start] = _rms(h_ref[rows, :], g_ref[...]).astype(BF16)
        xn = xns[rows.start]
        pos = ((pl.program_id(0) % tiles_per_seq) * tm + rows.start
               + lax.broadcasted_iota(jnp.int32, (m, 1), 0))
        valid = jnp.logical_or(pos < main_rows, pos >= main_rows + N_PAD).astype(F32)

        def proj(c0, width):
            return _dot(xn, w_ref[:, c0:c0 + width])

        small = proj(C_SMALL, LANES)
        if rows.start == 0:
            xns[m] = _rms(h_ref[halves[1], :], g_ref[...]).astype(BF16)
        fq_ref[rows, :] = (proj(C_FQ, FOX_WIDTH) * (FOX_HEAD_DIM ** -0.5 * LOG2E)).astype(BF16)
        fk_ref[rows, :] = proj(C_FK, FOX_WIDTH).astype(BF16)
        lane = lax.broadcasted_iota(jnp.int32, (m, LANES), 1)
        lf = _log_sigmoid(small + bf_ref[...]) * (valid * LOG2E)
        lf_ref[rows, :] = jnp.where(lane < FOX_HEADS, lf, 0.0)
        xa = _dot(small.astype(BF16), wa_ref[...]) + ba_ref[...]
        la_ref[rows, :] = _log_sigmoid(xa) * (valid * (1.0 / GLA_GATE_TEMP))
        fv_ref[rows, :] = proj(C_FV, FOX_WIDTH).astype(BF16)
        gq_ref[rows, :] = proj(C_GQ, GLA_KWIDTH) * (GLA_DK ** -0.5)
        gk_ref[rows, :] = proj(C_GK, GLA_KWIDTH) * valid
        gv_ref[rows, :] = proj(C_GV, GLA_WIDTH).astype(BF16)
        gr_ref[rows, :] = proj(C_GR, GLA_WIDTH).astype(BF16)
        gma_ref[rows, :] = proj(C_GMA, D_MODEL).astype(BF16)
        gmb_ref[rows, :] = proj(C_GMB, D_MODEL).astype(BF16)


def _proj(h, g, w_perm, wa_pad, ba, bf_pad, seq_len, layer):
    n = h.shape[0]
    tm = ROW_TM
    row = lambda width: pl.BlockSpec((tm, width), lambda i: (i, 0))
    full = lambda a: (_resident(a.shape[1:], layer) if a.ndim == 3
                      else pl.BlockSpec(a.shape, lambda i: (0, 0)))
    outs = [
        (FOX_WIDTH, BF16), (FOX_WIDTH, BF16), (FOX_WIDTH, BF16),
        (GLA_KWIDTH, F32), (GLA_KWIDTH, F32), (GLA_WIDTH, BF16),
        (GLA_WIDTH, BF16), (D_MODEL, BF16), (D_MODEL, BF16),
        (GLA_KWIDTH, F32), (LANES, F32),
    ]
    return pl.pallas_call(
        functools.partial(_proj_kernel, tiles_per_seq=seq_len // tm, main_rows=seq_len - BLOCK),
        grid=(n // tm,),
        in_specs=[row(D_MODEL), full(g), full(w_perm), full(wa_pad), full(ba), full(bf_pad)],
        out_specs=[row(w) for w, _ in outs],
        out_shape=[jax.ShapeDtypeStruct((n, w), dt) for w, dt in outs],
        scratch_shapes=[pltpu.VMEM((D_MODEL, N_IN_PERM), BF16)],
        compiler_params=pltpu.CompilerParams(
            dimension_semantics=("arbitrary",), vmem_limit_bytes=VMEM_LIMIT),
    )(h, g, w_perm, wa_pad, ba, bf_pad)


def _gla_kernel(q_ref, k_ref, v_ref, la_ref, lf_ref, o_ref, cc_ref, cr_ref, st_ref, carry_ref):
    @pl.when(pl.program_id(0) == 0)
    def _():
        st_ref[...] = jnp.zeros_like(st_ref)
        carry_ref[...] = jnp.zeros_like(carry_ref)

    row = lax.broadcasted_iota(jnp.int32, (BLOCK, BLOCK), 0)
    col = lax.broadcasted_iota(jnp.int32, (BLOCK, BLOCK), 1)
    causal = row >= col
    tril = causal.astype(BF16)
    tril_sub = (causal & ((row // SUB) == (col // SUB))).astype(BF16)
    row_w = lax.broadcasted_iota(jnp.int32, (BLOCK, GLA_KWIDTH), 0)
    row_blk = row // SUB

    batches = range(q_ref.shape[0])
    problems = [(b, h) for b in batches for h in range(GLA_HEADS)]

    cb_all, cw_all = [], []
    for b in batches:
        cl = _dot_exact_lhs(tril, _split2(lf_ref[b])) + carry_ref[b]
        carry_ref[b] = cl[BLOCK - 1:BLOCK, :]
        cc_ref[b] = cl
        cr_ref[b] = cl.T[0:FOX_HEADS, :]
        la_parts = _split2(la_ref[b])
        cb_all.append(_dot_exact_lhs(tril, la_parts))
        cw_all.append(_dot_exact_lhs(tril_sub, la_parts))

    ops = {}
    for b in batches:
        cb, cw = cb_all[b], cw_all[b]
        last = cb[BLOCK - 1:BLOCK, :]
        q = q_ref[b]
        k = k_ref[b]
        q_glob = (q * jnp.exp(cb)).astype(BF16)
        q_loc = (q * jnp.exp(cw)).astype(BF16)
        k_end = (k * jnp.exp(last - cb)).astype(BF16)
        decay = jnp.exp(last)
        k_sub = []
        for i in range(N_SUB):
            ref = cb[i * SUB - 1:i * SUB, :] if i > 0 else jnp.zeros_like(last)
            e = jnp.exp(jnp.minimum(ref - cb, EXP_CAP))
            k_sub.append(jnp.where(row_w < (i + 1) * SUB, k * e, 0.0).astype(BF16))
        for pair in range(GLA_HEADS // 2):
            sl = slice(pair * LANES, (pair + 1) * LANES)
            k_cat = jnp.concatenate([ks[:, sl] for ks in k_sub], axis=1)
            for half in range(2):
                h = 2 * pair + half
                head_lanes = (col < GLA_DK) if half == 0 else (col >= GLA_DK)
                q_h = jnp.where(head_lanes, q_loc[:, sl], 0.0)
                q_parts = []
                for p in range(N_SUB // 2):
                    r = slice(2 * p * SUB, (2 * p + 2) * SUB)
                    q_parts.append(jnp.concatenate(
                        [jnp.where(row_blk[r] == 2 * p, q_h[r], 0.0), jnp.where(row_blk[r] == 2 * p + 1, q_h[r], 0.0)],
                        axis=1))
                ops[b, h] = (q_parts, k_cat, jnp.where(head_lanes, q_glob[:, sl], 0.0), k_end[:, sl],
                             decay[:, sl], v_ref[b, :, h * GLA_DV:(h + 1) * GLA_DV])

    att, inter, upd = {}, {}, {}
    for p in problems:
        q_parts, k_cat, q_g, k_e, _, v_h = ops[p]
        att[p] = jnp.concatenate(
            [_dot_nt(q_p, k_cat[:, 2 * i * LANES:(2 * i + 2) * LANES]) for i, q_p in enumerate(q_parts)],
            axis=0)
        inter[p] = _dot_nt(q_g, st_ref[p].astype(BF16))
        upd[p] = _dot_tn(v_h, k_e)

    for p in problems:
        b, h = p
        a = jnp.where(causal, att[p], 0.0).astype(BF16)
        o_ref[b, :, h * GLA_DV:(h + 1) * GLA_DV] = (inter[p] + _dot(a, ops[p][5])).astype(o_ref.dtype)
        st_ref[p] = st_ref[p] * ops[p][4] + upd[p]


def _gla(gq, gk, gv, la, lf, batch, seq_len):
    nc = seq_len // BLOCK
    chunk = lambda c: (c + nc - 1) % nc
    blk = lambda width: pl.BlockSpec((batch, BLOCK, width), lambda c: (0, chunk(c), 0))
    r3 = lambda a: a.reshape(batch, seq_len, a.shape[-1])
    return pl.pallas_call(
        _gla_kernel,
        grid=(nc,),
        in_specs=[blk(GLA_KWIDTH), blk(GLA_KWIDTH), blk(GLA_WIDTH), blk(GLA_KWIDTH), blk(LANES)],
        out_specs=[blk(GLA_WIDTH), blk(LANES),
                   pl.BlockSpec((batch, FOX_HEADS, BLOCK), lambda c: (0, 0, chunk(c)))],
        out_shape=[jax.ShapeDtypeStruct((batch, seq_len, GLA_WIDTH), BF16),
                   jax.ShapeDtypeStruct((batch, seq_len, LANES), F32),
                   jax.ShapeDtypeStruct((batch, FOX_HEADS, seq_len), F32)],
        scratch_shapes=[pltpu.VMEM((batch, GLA_HEADS, GLA_DV, LANES), F32),
                        pltpu.VMEM((batch, 1, LANES), F32)],
        compiler_params=pltpu.CompilerParams(
            dimension_semantics=("arbitrary",), vmem_limit_bytes=VMEM_LIMIT),
    )(r3(gq), r3(gk), r3(gv), r3(la), r3(lf))


def _fox_kernel(q_ref, k_ref, v_ref, cc_ref, crp_ref, crm_ref, cb_ref, cs_ref, o_ref,
                ra_ref, rb_ref, ub_ref, plan_ref, m_ref, l_ref, acc_ref):
    t = FOX_T
    nt = crm_ref.shape[2]
    s_main = nt * t
    pre = slice(s_main, s_main + BLOCK)
    pair = pl.program_id(1)

    def stack_heads(x, lane_first):
        return jnp.concatenate([jnp.where(lane_first, x, 0.0), jnp.where(lane_first, 0.0, x)], axis=0)

    def head_cols(cc, lane):
        return jnp.concatenate(
            [jnp.sum(jnp.where(lane == 2 * pair + hh, cc, 0.0), axis=1, keepdims=True)
             for hh in range(2)], axis=0)

    lane_b = lax.broadcasted_iota(jnp.int32, (BLOCK, LANES), 1)
    first_b = lane_b < FOX_HEAD_DIM
    kp = k_ref[0, pre, :]
    vp = v_ref[0, pre, :]
    ckp = crp_ref[0, 0]
    row_b = lax.broadcasted_iota(jnp.int32, (BLOCK, BLOCK), 0)
    ok_b = (lane_b <= row_b) & ((lane_b >= N_PAD) | (lane_b == row_b))
    ok_b = jnp.concatenate([ok_b, ok_b], axis=0)
    s = _dot_nt(stack_heads(q_ref[0, pre, :], first_b), kp)
    z = jnp.concatenate([s[:BLOCK] - ckp[0:1], s[BLOCK:] - ckp[1:2]], axis=0)
    z = jnp.where(ok_b, z, NEG)
    cq2 = head_cols(cc_ref[0, pre, :], lane_b)
    m = jnp.max(z, axis=1, keepdims=True) + cq2
    p = jnp.exp2(z - (m - cq2))
    out = _dot(p.astype(BF16), vp) / jnp.sum(p, axis=1, keepdims=True)
    o_ref[0, pre, :] = jnp.where(first_b, out[:BLOCK], out[BLOCK:]).astype(o_ref.dtype)

    def v_with_ones(rows, head_b):
        v = v_ref[0, rows, :]
        lane_v = lax.broadcasted_iota(jnp.int32, v.shape, 1)
        if head_b:
            return jnp.where(lane_v >= FOX_HEAD_DIM, v, jnp.where(lane_v == 0, 1.0, 0.0).astype(BF16))
        return jnp.where(lane_v < FOX_HEAD_DIM, v, jnp.where(lane_v == FOX_HEAD_DIM, 1.0, 0.0).astype(BF16))

    lane = lax.broadcasted_iota(jnp.int32, (t, LANES), 1)
    first = lane < FOX_HEAD_DIM
    row_l = lax.broadcasted_iota(jnp.int32, (LANES, LANES), 0)
    row_w = lax.broadcasted_iota(jnp.int32, (LANES, 2 * LANES), 0)
    second = lax.broadcasted_iota(jnp.int32, (LANES, 2 * LANES), 1) >= LANES
    ind_ab = ((row_w >= FOX_HEAD_DIM) == second).astype(BF16)
    hot_ab = (row_w == 2 * pair + second.astype(jnp.int32)).astype(BF16)
    hot_la = (row_l == FOX_HEAD_DIM).astype(BF16)
    hot_lb = (row_l == 0).astype(BF16)
    reps = t // LANES
    th = t // 2

    sel = lax.broadcasted_iota(jnp.int32, (8, LANES), 0) == lax.broadcasted_iota(
        jnp.int32, (8, LANES), 1) // FOX_HEAD_DIM
    k_all = k_ref[0]
    k_max = jnp.sqrt(jnp.max(_dot_nt(sel.astype(BF16), k_all * k_all), axis=1, keepdims=True))

    per_step = next(u for u in (4, 2, 1) if nt % u == 0)

    def prepare(step, carry):
        tiles_here = [step * per_step + u for u in range(per_step)]
        rows = [pl.ds(pl.multiple_of(i * t, t), t) for i in tiles_here]
        nqs = [_dot(q_ref[0, r, :] * q_ref[0, r, :], ind_ab) for r in rows]
        cqs = [_dot(cc_ref[0, r, :].astype(BF16), hot_ab) for r in rows]
        for i, r, nq, cq in zip(tiles_here, rows, nqs, cqs):
            ub_a = jnp.sqrt(nq[:, :LANES]) * (k_max[0:1] * NORM_UP)
            ub_b = jnp.sqrt(nq[:, LANES:]) * (k_max[1:2] * NORM_UP)
            ra_ref[r, :] = ub_a - cq[:, :LANES]
            rb_ref[r, :] = ub_b - cq[:, LANES:]
            ub_ref[pl.ds(i, 1), :] = jnp.max(jnp.maximum(ub_a, ub_b), axis=0, keepdims=True)
        return carry

    lax.fori_loop(0, nt // per_step, prepare, 0)

    ub_t = ub_ref[0:nt, :]
    fast_v = ub_t <= FAST_MAX_BITS
    bound = jnp.where(fast_v, PRUNE_BITS, PRUNE_BITS + 2.0 * ub_t)
    cb = cb_ref[0, 0]
    dead = ((cs_ref[0, 0, 0, 0, 0:nt, :] - cb[0:1]) < -bound) & ((cs_ref[0, 0, 1, 0, 0:nt, :] - cb[1:2]) < -bound)
    tile_q = lax.broadcasted_iota(jnp.int32, (nt, LANES), 0)
    tile_k = lax.broadcasted_iota(jnp.int32, (nt, LANES), 1)
    n_skip_v = jnp.sum(jnp.where(dead & (tile_k < tile_q), 1, 0), axis=1, keepdims=True)
    skip_pre_v = jnp.sum(jnp.where(dead & (tile_k == nt), 1, 0), axis=1, keepdims=True)
    dead_lo = ((cs_ref[0, 0, 0, 1, 0:nt, :] - cb[0:1]) < -bound) & ((cs_ref[0, 0, 1, 1, 0:nt, :] - cb[1:2]) < -bound)
    old_dead_v = jnp.sum(jnp.where(dead_lo & (tile_k == tile_q + (nt - 1)), 1, 0), axis=1, keepdims=True)
    fast_i = jnp.where(fast_v[:, 0:1], 1, 0)
    tile_1 = tile_q[:, 0:1]
    for i in range(nt):
        for row, vec in enumerate((fast_i, n_skip_v, skip_pre_v, old_dead_v)):
            plan_ref[row, i] = jnp.sum(jnp.where(tile_1 == i, vec, 0))

    def main_tile(i, carry):
        r0 = pl.multiple_of(i * t, t)
        q2 = stack_heads(q_ref[0, pl.ds(r0, t), :], first)
        fast = plan_ref[0, i] == 1
        n_skip = plan_ref[1, i]
        skip_prefix = plan_ref[2, i]
        old_half_dead = plan_ref[3, i] == 1

        def logits(kt, ck, offs):
            s = _dot_nt(q2, kt)
            return jnp.concatenate([s[:t] - ck[0:1] - offs[:t], s[t:] - ck[1:2] - offs[t:]], axis=0)

        def run(update, offs_p, offs_m, fuse_tail):
            acc_ref[...] = jnp.zeros_like(acc_ref)

            @pl.when(skip_prefix == 0)
            def _():
                z = logits(kp, ckp, offs_p)
                lane_k = lax.broadcasted_iota(jnp.int32, (2 * t, BLOCK), 1)
                update([jnp.where(lane_k >= N_PAD, z, NEG)], [pre])

            def tiles(js, diagonal, old_half_dead=False):
                split = diagonal and fuse_tail
                full = js[:-1] if split else js
                upper = None
                if old_half_dead:
                    full = full[1:]
                    base = pl.multiple_of(js[0] * t, t)
                    ck = crm_ref[0, 0, js[0]]
                    r_old = pl.ds(base, th)
                    q_up = jnp.concatenate([q2[0:th], q2[t:t + th]], axis=0)
                    o_up = jnp.concatenate([offs_m[0:th, :th], offs_m[t:t + th, :th]], axis=0)
                    s = _dot_nt(q_up, k_ref[0, r_old, :])
                    upper = (jnp.concatenate([s[:th] - ck[0:1, :th] - o_up[:th], s[th:] - ck[1:2, :th] - o_up[th:]],
                                             axis=0), r_old)
                    r_new = pl.ds(base + th, th)
                    z_new = logits(k_ref[0, r_new, :], ck[:, th:], offs_m[:, :th])
                rows = [pl.ds(pl.multiple_of(j * t, t), t) for j in full]
                zs = [logits(k_ref[0, r, :], crm_ref[0, 0, j], offs_m) for j, r in zip(full, rows)]
                if old_half_dead:
                    zs, rows = [z_new] + zs, [r_new] + rows
                if not diagonal:
                    update(zs, rows)
                    return
                if not split:
                    ok = lax.broadcasted_iota(jnp.int32, (t, t), 1) <= lax.broadcasted_iota(jnp.int32, (t, t), 0)
                    zs[-1] = jnp.where(jnp.concatenate([ok, ok], axis=0), zs[-1], NEG)
                    update(zs, rows)
                    return
                base = pl.multiple_of(js[-1] * t, t)
                ck = crm_ref[0, 0, js[-1]]
                tri = lax.broadcasted_iota(jnp.int32, (th, th), 1) <= lax.broadcasted_iota(jnp.int32, (th, th), 0)
                r_left = pl.ds(base, th)
                z_left = logits(k_ref[0, r_left, :], ck[:, :th], offs_m[:, :th])
                lower_rows = (lax.broadcasted_iota(jnp.int32, (2 * t, th), 0) // th) % 2 == 1
                z_left = jnp.where(lower_rows | jnp.concatenate([tri] * 4, axis=0), z_left, NEG)
                r_right = pl.ds(base + th, th)
                q_lo = jnp.concatenate([q2[th:t], q2[t + th:]], axis=0)
                o_lo = jnp.concatenate([offs_m[th:t, :th], offs_m[t + th:, :th]], axis=0)
                s = _dot_nt(q_lo, k_ref[0, r_right, :])
                z_right = jnp.concatenate([s[:th] - ck[0:1, th:] - o_lo[:th], s[th:] - ck[1:2, th:] - o_lo[th:]],
                                          axis=0)
                z_right = jnp.where(jnp.concatenate([tri, tri], axis=0), z_right, NEG)
                update(zs + [z_left], rows + [r_left], (z_right, r_right), upper)

            def body(j, c):
                tiles([j], False)
                return c

            if not fuse_tail:
                lax.fori_loop(n_skip, i, body, 0)
                tiles([i], True)
                return
            n_live = i - n_skip
            lax.fori_loop(n_skip, jnp.maximum(i - 2, n_skip), body, 0)

            @pl.when((n_live >= 2) & old_half_dead)
            def _():
                tiles([i - 2, i - 1, i], True, True)

            @pl.when((n_live >= 2) & jnp.logical_not(old_half_dead))
            def _():
                tiles([i - 2, i - 1, i], True)

            @pl.when(n_live == 1)
            def _():
                tiles([i - 1, i], True)

            @pl.when(n_live == 0)
            def _():
                tiles([i], True)

        def finish(l_a, l_b):
            acc = acc_ref[...]
            out = jnp.where(first, acc[:t] / l_a, acc[t:] / l_b)
            o_ref[0, pl.ds(r0, t), :] = out.astype(o_ref.dtype)

        @pl.when(fast)
        def _():
            ref = jnp.concatenate([ra_ref[pl.ds(r0, t), :], rb_ref[pl.ds(r0, t), :]], axis=0)

            def update(zs, rows, lower=None, upper=None):
                ps = [jnp.exp2(z).astype(BF16) for z in zs]
                p_lo = None if lower is None else jnp.exp2(lower[0]).astype(BF16)
                if upper is not None:
                    p_up = jnp.exp2(upper[0]).astype(BF16)
                    acc_ref[0:th] += _dot(p_up[:th], v_with_ones(upper[1], False))
                    acc_ref[t:t + th] += _dot(p_up[th:], v_with_ones(upper[1], True))
                for half, head_b in ((slice(0, t), False), (slice(t, 2 * t), True)):
                    pv = _dot(ps[0][half], v_with_ones(rows[0], head_b))
                    for p, r in zip(ps[1:], rows[1:]):
                        pv = pv + _dot(p[half], v_with_ones(r, head_b))
                    acc_ref[half] += pv
                if lower is not None:
                    acc_ref[th:t] += _dot(p_lo[:th], v_with_ones(lower[1], False))
                    acc_ref[t + th:] += _dot(p_lo[th:], v_with_ones(lower[1], True))

            run(update, ref, jnp.concatenate([ref] * reps, axis=1), True)

            def denominators(acc, hot):
                hi = acc.astype(BF16)
                lo = (acc - hi.astype(F32)).astype(BF16)
                return _dot(hi, hot) + _dot(lo, hot)

            finish(denominators(acc_ref[0:t], hot_la), denominators(acc_ref[t:], hot_lb))

        @pl.when(jnp.logical_not(fast))
        def _():
            cq2 = head_cols(cc_ref[0, pl.ds(r0, t), :], lane)
            m_ref[...] = jnp.full_like(m_ref, NEG)
            l_ref[...] = jnp.zeros_like(l_ref)

            def update(zs, rows):
                (z,), (r,) = zs, rows
                m_old = m_ref[...]
                m_new = jnp.maximum(m_old, jnp.max(z, axis=1, keepdims=True) + cq2)
                p = jnp.exp2(z - (m_new - cq2))
                alpha = jnp.exp2(m_old - m_new)
                l_ref[...] = alpha * l_ref[...] + jnp.sum(p, axis=1, keepdims=True)
                acc_ref[...] = alpha * acc_ref[...] + _dot(p.astype(BF16), v_ref[0, r, :])
                m_ref[...] = m_new

            run(update, jnp.zeros((2 * t, 1), F32), jnp.zeros((2 * t, 1), F32), False)
            finish(l_ref[0:t], l_ref[t:])

        return carry

    lax.fori_loop(0, nt, main_tile, 0)


def _fox(fq, fk, fv, cum_col, cum_row, batch, seq_len):
    t = FOX_T
    nt = (seq_len - BLOCK) // t
    npairs = FOX_HEADS // 2
    assert 2 * nt + 1 <= LANES
    s_main = nt * t
    r3 = lambda a: a.reshape(batch, seq_len, a.shape[-1])
    cr = cum_row.reshape(batch, npairs, 2, seq_len)
    cr_prefix = cr[..., s_main:]
    cr_main = cr[..., :s_main].reshape(batch, npairs, 2, nt, t).transpose(0, 1, 3, 2, 4)
    th = t // 2
    bounds = jnp.concatenate(
        [cr[..., t - 1:s_main:t], cr[..., seq_len - 1:], cr[..., th - 1:s_main:t],
         jnp.zeros((batch, npairs, 2, LANES - 2 * nt - 1), F32)], axis=-1)
    nt8 = -(-nt // 8) * 8
    starts = jnp.stack([cr[..., 0:s_main:t], cr[..., th:s_main:t]], axis=3)
    starts = jnp.pad(starts, ((0, 0),) * 4 + ((0, nt8 - nt),))
    starts = jnp.broadcast_to(starts[..., None], (batch, npairs, 2, 2, nt8, LANES))
    whole = pl.BlockSpec((1, seq_len, LANES), lambda b, p: (b, 0, p))
    out = pl.pallas_call(
        _fox_kernel,
        grid=(batch, npairs),
        in_specs=[whole, whole, whole,
                  pl.BlockSpec((1, seq_len, LANES), lambda b, p: (b, 0, 0)),
                  pl.BlockSpec((1, 1, 2, BLOCK), lambda b, p: (b, p, 0, 0)),
                  pl.BlockSpec((1, 1, nt, 2, t), lambda b, p: (b, p, 0, 0, 0)),
                  pl.BlockSpec((1, 1, 2, LANES), lambda b, p: (b, p, 0, 0)),
                  pl.BlockSpec((1, 1, 2, 2, nt8, LANES), lambda b, p: (b, p, 0, 0, 0, 0))],
        out_specs=whole,
        out_shape=jax.ShapeDtypeStruct((batch, seq_len, FOX_WIDTH), BF16),
        scratch_shapes=[pltpu.VMEM((s_main, LANES), F32), pltpu.VMEM((s_main, LANES), F32),
                        pltpu.VMEM((nt8, LANES), F32),
                        pltpu.SMEM((4, nt), jnp.int32),
                        pltpu.VMEM((2 * t, 1), F32), pltpu.VMEM((2 * t, 1), F32),
                        pltpu.VMEM((2 * t, LANES), F32)],
        compiler_params=pltpu.CompilerParams(
            dimension_semantics=("parallel", "parallel"), vmem_limit_bytes=VMEM_LIMIT),
    )(r3(fq), r3(fk), r3(fv), cum_col, cr_prefix, cr_main, bounds, starts)
    return out.reshape(batch * seq_len, FOX_WIDTH)


def _post_kernel(of_ref, og_ref, gr_ref, gma_ref, gmb_ref, h_ref, ggla_ref, wfo_ref, wgo_ref,
                 wout_ref, gpost_ref, o_ref):
    tm = h_ref.shape[0]
    halves = (slice(0, tm // 2), slice(tm // 2, tm))
    chunks = [slice(c, c + MERGE_TN) for c in range(0, D_MODEL, MERGE_TN)]
    ggla = ggla_ref[...]
    o_glas = []
    for hs in halves:
        og = og_ref[hs, :].astype(F32)
        heads = [_rms(og[:, hh * GLA_DV:(hh + 1) * GLA_DV], ggla[:, hh * GLA_DV:(hh + 1) * GLA_DV])
                 for hh in range(GLA_HEADS)]
        gr = gr_ref[hs, :].astype(F32)
        o_glas.append((jnp.concatenate(heads, axis=1) * (gr * jax.nn.sigmoid(gr))).astype(BF16))
    for hs, o_gla in zip(halves, o_glas):
        o_fox = of_ref[hs, :]
        branch = [(_dot(o_fox, wfo_ref[:, cs]), _dot(o_gla, wgo_ref[:, cs])) for cs in chunks[:1]]
        mix = None
        for c, cs in enumerate(chunks):
            if c + 1 < len(chunks):
                nxt = chunks[c + 1]
                branch.append((_dot(o_fox, wfo_ref[:, nxt]), _dot(o_gla, wgo_ref[:, nxt])))
            a_c, b_c = branch[c]
            y = (jax.nn.sigmoid(gma_ref[hs, cs].astype(F32)) * a_c
                 + jax.nn.sigmoid(gmb_ref[hs, cs].astype(F32)) * b_c).astype(BF16)
            part = _dot(y, wout_ref[cs, :])
            mix = part if mix is None else mix + part
        o_ref[hs, :] = h_ref[hs, :] + _rms(mix, gpost_ref[...])


def _post(o_fox, o_gla, gr, gma, gmb, h, ggla, wfo, wgo, wout, gpost, layer):
    n = h.shape[0]
    tm = ROW_TM
    row = lambda a: pl.BlockSpec((tm, a.shape[1]), lambda i: (i, 0))
    full = lambda a: (_resident(a.shape[1:], layer) if a.ndim == 3
                      else pl.BlockSpec(a.shape, lambda i: (0, 0)))
    args = (o_fox, o_gla, gr, gma, gmb, h, ggla, wfo, wgo, wout, gpost)
    return pl.pallas_call(
        _post_kernel,
        grid=(n // tm,),
        in_specs=[row(a) for a in args[:6]] + [full(a) for a in args[6:]],
        out_specs=pl.BlockSpec((tm, D_MODEL), lambda i: (i, 0)),
        out_shape=jax.ShapeDtypeStruct((n, D_MODEL), F32),
        compiler_params=pltpu.CompilerParams(
            dimension_semantics=("parallel",), vmem_limit_bytes=VMEM_LIMIT),
    )(*args)


def kernel(x, meta_tokens, w_in, w_alpha_up, b_alpha, b_f, g_gla_out, w_fox_o, w_gla_o, w_out,
           w_ffn1_gu, w_ffn1_down, w_ffn2_gu, w_ffn2_down,
           g_pre_ffn1, g_post_ffn1, g_pre_mix, g_post_mix, g_pre_ffn2, g_post_ffn2):
    batch, seq, d = x.shape
    seq_len = seq + BLOCK
    depth = w_in.shape[0]
    assert d == D_MODEL and (seq_len - BLOCK) % FOX_T == 0 and seq_len % ROW_TM == 0
    assert seq_len % FFN_TM == 0 and seq % FFN_TM_MAIN == 0

    pad = jnp.zeros((batch, N_PAD, d), x.dtype)
    meta = jnp.broadcast_to(meta_tokens.astype(x.dtype)[None], (batch, N_META, d))
    prefix = jnp.concatenate([pad, meta], axis=1)
    flat = lambda a: a.reshape(batch * seq_len, a.shape[-1])

    row2 = lambda a: a.reshape(1, -1).astype(F32)
    w_perm = w_in.astype(BF16)
    w1_gu, w1_down, w2_gu, w2_down = w_ffn1_gu, w_ffn1_down, w_ffn2_gu, w_ffn2_down
    wfo, wgo, wout = w_fox_o.astype(BF16), w_gla_o.astype(BF16), w_out.astype(BF16)
    for l in range(depth):
        wa_pad = jnp.zeros((LANES, GLA_KWIDTH), F32).at[
            FOX_HEADS:FOX_HEADS + GLA_GATE_RANK].set(w_alpha_up[l]).astype(BF16)
        bf_pad = jnp.zeros((1, LANES), F32).at[0, :FOX_HEADS].set(b_f[l])

        if l == 0:
            h = _ffn_first(x, prefix, row2(g_pre_ffn1[l]), w1_gu, w1_down, row2(g_post_ffn1[l]), l)
        else:
            h = _ffn(h, row2(g_pre_ffn1[l]), w1_gu, w1_down, row2(g_post_ffn1[l]), l)
        fq, fk, fv, gq, gk, gv, gr, gma, gmb, la, lf = _proj(
            flat(h), row2(g_pre_mix[l]), w_perm, wa_pad, row2(b_alpha[l]), bf_pad, seq_len, l)
        o_gla, cum_col, cum_row = _gla(gq, gk, gv, la, lf, batch, seq_len)
        o_fox = _fox(fq, fk, fv, cum_col, cum_row, batch, seq_len)
        h = _post(o_fox, flat(o_gla), gr, gma, gmb, flat(h),
                  row2(g_gla_out[l]), wfo, wgo, wout, row2(g_post_mix[l]), l).reshape(batch, seq_len, d)
        h = _ffn(h, row2(g_pre_ffn2[l]), w2_gu, w2_down, row2(g_post_ffn2[l]), l,
                 main_rows=seq if l == depth - 1 else None)
    return h
```
